```python
import math
import jax
import jax.numpy as jnp
from jax import lax
import numpy as np

D_MODEL = 2048
BATCH = 2
SEQ = 16384
DEPTH = 4

N_META = 16
BLOCK = 128
N_PAD = BLOCK - N_META
NORM_EPS = 1e-6

MIX_DIM = D_MODEL
ATTN_DIM = MIX_DIM // 2
HEAD_DIM = 64
ATTN_HEADS = ATTN_DIM // HEAD_DIM
ATTN_KV_HEADS = ATTN_HEADS // 4
ATTN_GROUP = ATTN_HEADS // ATTN_KV_HEADS
WINDOW = 128
QK_EPS = 1e-6

RWKV_DIM = MIX_DIM - ATTN_DIM
RWKV_HEAD = 64
RWKV_HEADS = RWKV_DIM // RWKV_HEAD
DECAY_LORA = max(32, int(round(1.8 * RWKV_DIM ** 0.5 / 32)) * 32)
AAA_LORA = max(32, int(round(1.8 * RWKV_DIM ** 0.5 / 32)) * 32)
GATE_LORA = max(32, int(round(0.6 * RWKV_DIM ** 0.8 / 32)) * 32)
RWKV_GN_EPS = 64e-5

Q_COLS = ATTN_HEADS * HEAD_DIM
KV_COLS = ATTN_KV_HEADS * HEAD_DIM
ATTN_COLS = Q_COLS + 2 * KV_COLS
RWKV_COLS = 3 * RWKV_DIM + DECAY_LORA + AAA_LORA + GATE_LORA
AR_IN_COLS = ATTN_COLS + RWKV_COLS

SSD_INNER = 2 * D_MODEL
SSD_HEAD_DIM = 64
SSD_HEADS = SSD_INNER // SSD_HEAD_DIM
SSD_GROUPS = 8
SSD_HPG = SSD_HEADS // SSD_GROUPS
SSD_STATE = 128
SSD_CONV = 4
SSD_CONV_DIM = SSD_INNER + 2 * SSD_GROUPS * SSD_STATE
SSD_IN_COLS = SSD_INNER + SSD_CONV_DIM + SSD_HEADS
SSD_NORM_EPS = 1e-5

FFN_DIM = 256 * (-(-8 * D_MODEL // (3 * 256)))
FFN_CONV = 3

N_EVEN = (DEPTH + 1) // 2
N_ODD = DEPTH // 2

kernel_name = "hybrid_swa_rwkv7_ssd_convffn"


def rms_norm(x, w, eps):
    xf = x.astype(jnp.float32)
    y = xf * lax.rsqrt(jnp.mean(xf * xf, axis=-1, keepdims=True) + eps)
    return (y * w.astype(jnp.float32)).astype(x.dtype)


def causal_dwconv(x, w, b):
    K = w.shape[0]
    T = x.shape[1]
    xp = jnp.pad(x, ((0, 0), (K - 1, 0), (0, 0)))
    y = b
    for j in range(K):
        y = y + w[j] * xp[:, j:j + T]
    return y


def _banded(t, nb):
    tb = t.reshape((t.shape[0], nb, BLOCK) + t.shape[2:])
    prev = jnp.pad(tb[:, :-1], [(0, 0), (1, 0)] + [(0, 0)] * (tb.ndim - 2))
    return jnp.concatenate([prev, tb], axis=2)


def swa_sink_attention(q, k, v, sinks, valid):
    f32 = jnp.float32
    bsz, P = q.shape[0], q.shape[1]
    nb = P // BLOCK
    qb = q.astype(f32).reshape(bsz, nb, BLOCK, ATTN_KV_HEADS, ATTN_GROUP, HEAD_DIM)
    kb = _banded(k.astype(f32), nb)
    vb = _banded(v.astype(f32), nb)
    kvalid = _banded(valid[None], nb)[0]
    km = k[:, N_PAD:BLOCK].astype(f32)
    vm = v[:, N_PAD:BLOCK].astype(f32)
    qpos = jnp.arange(nb)[:, None] * BLOCK + jnp.arange(BLOCK)[None, :]
    kpos = jnp.arange(nb)[:, None] * BLOCK - BLOCK + jnp.arange(2 * BLOCK)[None, :]
    dist = qpos[:, :, None] - kpos[:, None, :]
    band = (dist >= 0) & (dist < WINDOW) & kvalid[:, None, :]
    mdist = qpos[:, :, None] - (N_PAD + jnp.arange(N_META))[None, None, :]
    mvis = mdist >= WINDOW
    scale = HEAD_DIM ** -0.5
    s = jnp.einsum('bnqhgd,bnkhd->bnhgqk', qb, kb) * scale
    sm = jnp.einsum('bnqhgd,bmhd->bnhgqm', qb, km) * scale
    s = jnp.where(band[None, :, None, None], s, -jnp.inf)
    sm = jnp.where(mvis[None, :, None, None], sm, -jnp.inf)
    sink = sinks.astype(f32).reshape(1, 1, ATTN_KV_HEADS, ATTN_GROUP, 1, 1)
    mx = jnp.maximum(jnp.maximum(s.max(-1, keepdims=True), sm.max(-1, keepdims=True)), sink)
    p = jnp.exp(s - mx)
    pm = jnp.exp(sm - mx)
    inv = 1.0 / (p.sum(-1, keepdims=True) + pm.sum(-1, keepdims=True) + jnp.exp(sink - mx))
    o = (jnp.einsum('bnhgqk,bnkhd->bnqhgd', p * inv, vb)
         + jnp.einsum('bnhgqm,bmhd->bnqhgd', pm * inv, vm))
    return o.reshape(bsz, P, ATTN_HEADS * HEAD_DIM)


def rwkv7_time_mix(zr, valid, w0, w_up, a0, a_up, g_up, k_k, k_a, r_k, ln_w, ln_b):
    f32 = jnp.float32
    bsz, P = zr.shape[0], zr.shape[1]
    zr = zr.astype(f32)
    r, k, v, xw, xa, xg = jnp.split(
        zr, [RWKV_DIM, 2 * RWKV_DIM, 3 * RWKV_DIM, 3 * RWKV_DIM + DECAY_LORA,
             3 * RWKV_DIM + DECAY_LORA + AAA_LORA], axis=-1)
    w_log = -jnp.exp(-jax.nn.softplus(-(w0.astype(f32) + jnp.tanh(xw) @ w_up.astype(f32))) - 0.5)
    w_log = jnp.where(valid[None, :, None], w_log, 0.0)
    a = jax.nn.sigmoid(a0.astype(f32) + xa @ a_up.astype(f32))
    g = jax.nn.sigmoid(xg) @ g_up.astype(f32)
    heads = lambda t: t.reshape(bsz, P, RWKV_HEADS, RWKV_HEAD)
    hp = lambda t: t.astype(f32).reshape(RWKV_HEADS, RWKV_HEAD)
    kk = heads(k * k_k.astype(f32))
    kk = kk / jnp.maximum(jnp.linalg.norm(kk, axis=-1, keepdims=True), 1e-12)
    k = k * (1.0 + (a - 1.0) * k_a.astype(f32))
    r, k, v, a, w = heads(r), heads(k), heads(v), heads(a), heads(jnp.exp(w_log))

    def step(S, inp):
        r_t, w_t, k_t, v_t, kk_t, a_t = inp
        sa = jnp.einsum('bhij,bhj->bhi', S, -kk_t)
        S = (S * w_t[:, :, None, :] + sa[..., None] * (kk_t * a_t)[:, :, None, :]
             + v_t[..., None] * k_t[:, :, None, :])
        return S, jnp.einsum('bhij,bhj->bhi', S, r_t)

    seq_first = lambda t: jnp.moveaxis(t, 1, 0)
    S0 = jnp.zeros((bsz, RWKV_HEADS, RWKV_HEAD, RWKV_HEAD), f32)
    _, y = lax.scan(step, S0, tuple(seq_first(t) for t in (r, w, k, v, kk, a)))
    y = jnp.moveaxis(y, 0, 1)
    mu = jnp.mean(y, axis=-1, keepdims=True)
    var = jnp.mean(jnp.square(y - mu), axis=-1, keepdims=True)
    y = (y - mu) * lax.rsqrt(var + RWKV_GN_EPS) * hp(ln_w) + hp(ln_b)
    y = y + jnp.sum(r * k * r_k.astype(f32), axis=-1, keepdims=True) * v
    return y.reshape(bsz, P, RWKV_DIM) * g


def attn_rwkv_mixer(h, valid, w_in, shift_mu, q_norm_w, k_norm_w, sinks, w0, w_up, a0, a_up,
                    g_up, k_k, k_a, r_k, ln_w, ln_b, w_out):
    bsz, P, _ = h.shape
    zin = h @ w_in
    za, zr = zin[..., :ATTN_COLS], zin[..., ATTN_COLS:]
    q, k, v = jnp.split(za, [Q_COLS, Q_COLS + KV_COLS], axis=-1)
    q = rms_norm(q.reshape(bsz, P, ATTN_HEADS, HEAD_DIM), q_norm_w, QK_EPS)
    k = rms_norm(k.reshape(bsz, P, ATTN_KV_HEADS, HEAD_DIM), k_norm_w, QK_EPS)
    v = v.reshape(bsz, P, ATTN_KV_HEADS, HEAD_DIM)
    attn = swa_sink_attention(q, k, v, sinks, valid)
    zr_prev = jnp.pad(zr[:, :-1], ((0, 0), (1, 0), (0, 0)))
    zr = zr + (zr_prev - zr) * shift_mu
    tm = rwkv7_time_mix(zr, valid, w0, w_up, a0, a_up, g_up, k_k, k_a, r_k, ln_w, ln_b)
    return jnp.concatenate([attn.astype(h.dtype), tm.astype(h.dtype)], axis=-1) @ w_out


def mamba2_ssd_mixer(h, valid, w_in, conv_w, conv_b, dt_bias, a_log, d_skip, norm_w, w_out):
    f32 = jnp.float32
    bsz, P, _ = h.shape
    nc = P // BLOCK
    zxbcdt = h @ w_in
    z, xbc, dt = jnp.split(zxbcdt, [SSD_INNER, SSD_INNER + SSD_CONV_DIM], axis=-1)
    xbc = jax.nn.silu(causal_dwconv(xbc, conv_w, conv_b)).astype(f32) * valid[None, :, None]
    x, bm, cm = jnp.split(xbc, [SSD_INNER, SSD_INNER + SSD_GROUPS * SSD_STATE], axis=-1)
    dt = jax.nn.softplus(dt.astype(f32) + dt_bias.astype(f32)) * valid[None, :, None]
    a = -jnp.exp(a_log.astype(f32)).reshape(SSD_GROUPS, SSD_HPG)
    x = x.reshape(bsz, P, SSD_GROUPS, SSD_HPG, SSD_HEAD_DIM)
    dtg = dt.reshape(bsz, P, SSD_GROUPS, SSD_HPG)
    adt = dtg * a
    xdt = x * dtg[..., None]
    bm = bm.reshape(bsz, P, SSD_GROUPS, SSD_STATE)
    cm = cm.reshape(bsz, P, SSD_GROUPS, SSD_STATE)
    chunks = lambda t: jnp.moveaxis(t.reshape((bsz, nc, BLOCK) + t.shape[2:]), 1, 0)
    causal = jnp.tril(jnp.ones((BLOCK, BLOCK), bool))

    def step(state, inp):
        xc, ac, bc, cc = inp
        cum = jnp.cumsum(ac, axis=1)
        seg = cum[:, :, None] - cum[:, None, :]
        decay = jnp.exp(jnp.where(causal[None, :, :, None, None], seg, -jnp.inf))
        cb = jnp.einsum('btgn,bsgn->btsg', cc, bc)
        y = jnp.einsum('btsg,btsgj,bsgjp->btgjp', cb, decay, xc)
        y = y + jnp.einsum('btgn,bgjpn->btgjp', cc, state) * jnp.exp(cum)[..., None]
        to_end = jnp.exp(cum[:, -1:] - cum)
        state = (state * jnp.exp(cum[:, -1])[..., None, None]
                 + jnp.einsum('bsgn,bsgj,bsgjp->bgjpn', bc, to_end, xc))
        return state, y

    state0 = jnp.zeros((bsz, SSD_GROUPS, SSD_HPG, SSD_HEAD_DIM, SSD_STATE), f32)
    _, y = lax.scan(step, state0, (chunks(xdt), chunks(adt), chunks(bm), chunks(cm)))
    y = jnp.moveaxis(y, 0, 1).reshape(bsz, P, SSD_GROUPS, SSD_HPG, SSD_HEAD_DIM)
    y = y + d_skip.astype(f32).reshape(SSD_GROUPS, SSD_HPG)[..., None] * x
    y = y.reshape(bsz, P, SSD_INNER) * jax.nn.silu(z.astype(f32))
    yg = y.reshape(bsz, P, SSD_GROUPS, SSD_INNER // SSD_GROUPS)
    yg = yg * lax.rsqrt(jnp.mean(yg * yg, axis=-1, keepdims=True) + SSD_NORM_EPS)
    y = yg.reshape(bsz, P, SSD_INNER) * norm_w.astype(f32)
    return y.astype(h.dtype) @ w_out


def conv_glu_ffn(h, w_up, conv_w, conv_b, w_down):
    gate, val = jnp.split(h @ w_up, 2, axis=-1)
    gate = causal_dwconv(gate, conv_w, conv_b)
    return (jax.nn.silu(gate) * val) @ w_down


def setup_inputs(seed: int = 0) -> dict:
    key = jax.random.key(seed)
    ks = iter(jax.random.split(key, 40))
    nrm = lambda shape, scale: scale * jax.random.normal(next(ks), shape, jnp.float32)
    uni = lambda shape, lo, hi: jax.random.uniform(next(ks), shape, jnp.float32, lo, hi)
    E, O = N_EVEN, N_ODD
    out_scale = (2 * DEPTH) ** -0.5
    dt0 = jnp.exp(uni((O, SSD_HEADS), math.log(1e-3), math.log(1e-1)))
    return {
        "x": nrm((BATCH, SEQ, D_MODEL), 1.0),
        "meta_tokens": nrm((N_META, D_MODEL), 1.0),
        "mix_norm_w": 1.0 + nrm((DEPTH, D_MODEL), 0.05),
        "ffn_norm_w": 1.0 + nrm((DEPTH, D_MODEL), 0.05),
        "ar_w_in": nrm((E, D_MODEL, AR_IN_COLS), D_MODEL ** -0.5),
        "ar_shift_mu": uni((E, RWKV_COLS), 0.0, 1.0),
        "attn_q_norm_w": 1.0 + nrm((E, HEAD_DIM), 0.05),
        "attn_k_norm_w": 1.0 + nrm((E, HEAD_DIM), 0.05),
        "attn_sinks": nrm((E, ATTN_HEADS), 0.5),
        "rwkv_w0": uni((E, RWKV_DIM), -6.0, -1.0),
        "rwkv_w_up": nrm((E, DECAY_LORA, RWKV_DIM), DECAY_LORA ** -0.5),
        "rwkv_a0": nrm((E, RWKV_DIM), 0.1),
        "rwkv_a_up": nrm((E, AAA_LORA, RWKV_DIM), AAA_LORA ** -0.5),
        "rwkv_g_up": nrm((E, GATE_LORA, RWKV_DIM), GATE_LORA ** -0.5),
        "rwkv_k_k": 0.85 + nrm((E, RWKV_DIM), 0.05),
        "rwkv_k_a": 1.0 + nrm((E, RWKV_DIM), 0.05),
        "rwkv_r_k": nrm((E, RWKV_HEADS, RWKV_HEAD), 0.1),
        "rwkv_ln_w": 1.0 + nrm((E, RWKV_DIM), 0.05),
        "rwkv_ln_b": nrm((E, RWKV_DIM), 0.02),
        "ar_w_out": nrm((E, MIX_DIM, D_MODEL), MIX_DIM ** -0.5 * out_scale),
        "ssd_w_in": nrm((O, D_MODEL, SSD_IN_COLS), D_MODEL ** -0.5),
        "ssd_conv_w": nrm((O, SSD_CONV, SSD_CONV_DIM), SSD_CONV ** -0.5),
        "ssd_conv_b": nrm((O, SSD_CONV_DIM), 0.02),
        "ssd_dt_bias": dt0 + jnp.log(-jnp.expm1(-dt0)),
        "ssd_a_log": jnp.log(uni((O, SSD_HEADS), 1.0, 16.0)),
        "ssd_d": 1.0 + nrm((O, SSD_HEADS), 0.1),
        "ssd_norm_w": 1.0 + nrm((O, SSD_INNER), 0.05),
        "ssd_w_out": nrm((O, SSD_INNER, D_MODEL), SSD_INNER ** -0.5 * out_scale),
        "ffn_w_up": nrm((DEPTH, D_MODEL, 2 * FFN_DIM), D_MODEL ** -0.5),
        "ffn_conv_w": nrm((DEPTH, FFN_CONV, FFN_DIM), FFN_CONV ** -0.5),
        "ffn_conv_b": nrm((DEPTH, FFN_DIM), 0.02),
        "ffn_w_down": nrm((DEPTH, FFN_DIM, D_MODEL), FFN_DIM ** -0.5 * out_scale),
    }


def reference(x, meta_tokens, mix_norm_w, ffn_norm_w, ar_w_in, ar_shift_mu, attn_q_norm_w,
              attn_k_norm_w, attn_sinks, rwkv_w0, rwkv_w_up, rwkv_a0, rwkv_a_up, rwkv_g_up,
              rwkv_k_k, rwkv_k_a, rwkv_r_k, rwkv_ln_w, rwkv_ln_b, ar_w_out, ssd_w_in, ssd_conv_w,
              ssd_conv_b, ssd_dt_bias, ssd_a_log, ssd_d, ssd_norm_w, ssd_w_out, ffn_w_up,
              ffn_conv_w, ffn_conv_b, ffn_w_down):
    bsz, seq, _ = x.shape
    P = N_PAD + N_META + seq
    valid = jnp.arange(P) >= N_PAD
    vmask = valid[None, :, None].astype(x.dtype)
    res = jnp.concatenate([
        jnp.zeros((bsz, N_PAD, D_MODEL), x.dtype),
        jnp.broadcast_to(meta_tokens.astype(x.dtype)[None], (bsz, N_META, D_MODEL)),
        x], axis=1)
    for layer in range(DEPTH):
        i = layer // 2
        h = rms_norm(res, mix_norm_w[layer], NORM_EPS) * vmask
        if layer % 2 == 0:
            mix = attn_rwkv_mixer(h, valid, ar_w_in[i], ar_shift_mu[i], attn_q_norm_w[i],
                                  attn_k_norm_w[i], attn_sinks[i], rwkv_w0[i], rwkv_w_up[i],
                                  rwkv_a0[i], rwkv_a_up[i], rwkv_g_up[i], rwkv_k_k[i], rwkv_k_a[i],
                                  rwkv_r_k[i], rwkv_ln_w[i], rwkv_ln_b[i], ar_w_out[i])
        else:
            mix = mamba2_ssd_mixer(h, valid, ssd_w_in[i], ssd_conv_w[i], ssd_conv_b[i],
                                   ssd_dt_bias[i], ssd_a_log[i], ssd_d[i], ssd_norm_w[i],
                                   ssd_w_out[i])
        res = res + mix.astype(res.dtype)
        h = rms_norm(res, ffn_norm_w[layer], NORM_EPS) * vmask
        res = res + conv_glu_ffn(h, ffn_w_up[layer], ffn_conv_w[layer], ffn_conv_b[layer],
                                 ffn_w_down[layer]).astype(res.dtype)
    return res[:, N_PAD + N_META:]
```

```python
import functools

import jax
import jax.numpy as jnp
from jax import lax
from jax.experimental import pallas as pl
from jax.experimental.pallas import tpu as pltpu

F32 = jnp.float32
BF16 = jnp.bfloat16

D_MODEL = 2048
N_META = 16
BLOCK = 128
N_PAD = BLOCK - N_META
NORM_EPS = 1e-6

HEAD_DIM = 64
ATTN_HEADS = 16
ATTN_KV_HEADS = 4
ATTN_GROUP = 4
WINDOW = 128
QK_EPS = 1e-6
Q_COLS = ATTN_HEADS * HEAD_DIM
KV_COLS = ATTN_KV_HEADS * HEAD_DIM
ATTN_COLS = Q_COLS + 2 * KV_COLS

RWKV_DIM = 1024
RWKV_HEAD = 64
DECAY_LORA = 64
AAA_LORA = 64
GATE_LORA = 160
LORA_COLS = DECAY_LORA + AAA_LORA + GATE_LORA
LORA_PAD = 512
RWKV_GN_EPS = 64e-5
RWKV_CHUNK = 64
AR_COLS = 3 * RWKV_DIM + Q_COLS + 2 * KV_COLS + LORA_PAD

SSD_INNER = 4096
SSD_HEAD_DIM = 64
SSD_HEADS = 64
SSD_GROUPS = 8
SSD_HPG = 8
SSD_STATE = 128
SSD_CONV = 4
SSD_GROUP_COLS = SSD_INNER // SSD_GROUPS
SSD_BC_COLS = SSD_GROUPS * SSD_STATE
SSD_DT_PAD = 128
SSD_COLS = 2 * SSD_INNER + 2 * SSD_BC_COLS + SSD_DT_PAD
SSD_NORM_EPS = 1e-5

FFN_DIM = 5632
FFN_CONV = 3
FFN_CHUNK = 512

LANES = 128
SUBLANES = 8
VMEM_LIMIT = 56 * 1024 * 1024


def _params(*sem):
    return pltpu.CompilerParams(dimension_semantics=sem, vmem_limit_bytes=VMEM_LIMIT)


def _row_tile(total, target):
    best = None
    for t in range(16, min(total, target) + 1, 16):
        if total % t == 0:
            best = t
    assert best is not None, (total, target)
    return best


def _dot(a, b):
    return jnp.dot(a.astype(BF16), b.astype(BF16), preferred_element_type=F32)


def _dot_nt(a, b):
    return lax.dot_general(a.astype(BF16), b.astype(BF16), (((1,), (1,)), ((), ())),
                           preferred_element_type=F32)


def _dot_tn(a, b):
    return lax.dot_general(a.astype(BF16), b.astype(BF16), (((0,), (0,)), ((), ())),
                           preferred_element_type=F32)


def _split3(x):
    hi = x.astype(BF16)
    r1 = x - hi.astype(F32)
    mid = r1.astype(BF16)
    lo = (r1 - mid.astype(F32)).astype(BF16)
    return hi, mid, lo


def _dot_exact_rhs(sel_bf16, x):
    hi, mid, lo = _split3(x)
    d = lambda p: jnp.dot(sel_bf16, p, preferred_element_type=F32)
    return d(hi) + d(mid) + d(lo)


def _dot_exact_lhs(x, sel_bf16):
    hi, mid, lo = _split3(x)
    d = lambda p: jnp.dot(p, sel_bf16, preferred_element_type=F32)
    return d(hi) + d(mid) + d(lo)


def _sigmoid(x):
    return 1.0 / (1.0 + jnp.exp(-x))


def _softplus(x):
    return jnp.maximum(x, 0.0) + jnp.log(1.0 + jnp.exp(-jnp.abs(x)))


def _valid_rows(row0, nrows, seq_p, nbatch):
    rows = row0 + lax.broadcasted_iota(jnp.int32, (nrows, 1), 0)
    pos = rows
    for b in range(1, nbatch):
        pos = jnp.where(rows >= b * seq_p, rows - b * seq_p, pos)
    return pos >= N_PAD


def _masked_rmsnorm(x, w, row0, seq_p, nbatch):
    ms = jnp.mean(x * x, axis=-1, keepdims=True)
    y = x * lax.rsqrt(ms + NORM_EPS) * w
    return jnp.where(_valid_rows(row0, x.shape[0], seq_p, nbatch), y, 0.0)


def _norm_mm_kernel(x_ref, nw_ref, w_ref, o_ref, h_scr, *, tm, seq_p, nbatch):
    @pl.when(pl.program_id(1) == 0)
    def _():
        h = _masked_rmsnorm(x_ref[...], nw_ref[...], pl.program_id(0) * tm, seq_p, nbatch)
        h_scr[...] = h.astype(BF16)

    o_ref[...] = jnp.dot(h_scr[...], w_ref[...], preferred_element_type=F32)


def _norm_matmul(res, norm_w, w_bf16, *, tn, seq_p, nbatch):
    T, D = res.shape
    N = w_bf16.shape[1]
    tm = _row_tile(T, 768)
    assert N % tn == 0
    return pl.pallas_call(
        functools.partial(_norm_mm_kernel, tm=tm, seq_p=seq_p, nbatch=nbatch),
        grid=(T // tm, N // tn),
        in_specs=[pl.BlockSpec((tm, D), lambda m, n: (m, 0)),
                  pl.BlockSpec((1, D), lambda m, n: (0, 0)),
                  pl.BlockSpec((D, tn), lambda m, n: (0, n))],
        out_specs=pl.BlockSpec((tm, tn), lambda m, n: (m, n)),
        out_shape=jax.ShapeDtypeStruct((T, N), F32),
        scratch_shapes=[pltpu.VMEM((tm, D), BF16)],
        compiler_params=_params("arbitrary", "arbitrary"),
        name="norm_in_proj",
    )(res, norm_w.reshape(1, D), w_bf16)


def _proj_res_kernel(*refs, n_lhs):
    lhs = refs[:n_lhs]
    ws = refs[n_lhs:2 * n_lhs]
    res_ref, o_ref = refs[2 * n_lhs], refs[2 * n_lhs + 1]
    acc = res_ref[...]
    for l, w in zip(lhs, ws):
        acc = acc + jnp.dot(l[...], w[...], preferred_element_type=F32)
    o_ref[...] = acc


def _proj_residual(lhs_list, w_bf16, res, *, tn=512):
    T, D = res.shape
    n_lhs = len(lhs_list)
    kw = lhs_list[0].shape[1]
    assert all(l.shape == (T, kw) for l in lhs_list) and w_bf16.shape == (n_lhs * kw, D)
    tm = _row_tile(T, 768)
    in_specs = [pl.BlockSpec((tm, kw), lambda m, n: (m, 0)) for _ in range(n_lhs)]
    in_specs += [pl.BlockSpec((kw, tn), functools.partial(lambda i, m, n: (i, n), i)) for i in range(n_lhs)]
    in_specs += [pl.BlockSpec((tm, tn), lambda m, n: (m, n))]
    return pl.pallas_call(
        functools.partial(_proj_res_kernel, n_lhs=n_lhs),
        grid=(T // tm, D // tn),
        in_specs=in_specs,
        out_specs=pl.BlockSpec((tm, tn), lambda m, n: (m, n)),
        out_shape=jax.ShapeDtypeStruct((T, D), F32),
        compiler_params=_params("arbitrary", "arbitrary"),
        name="out_proj_residual",
    )(*lhs_list, *([w_bf16] * n_lhs), res)


def _ffn_kernel(res_ref, nw_ref, wg_ref, wv_ref, cw_ref, cb_ref, wd_ref, o_ref, h_scr, carry_scr,
                *, tm, seq_p, nbatch):
    m = pl.program_id(0)
    f = pl.program_id(1)

    @pl.when(f == 0)
    def _():
        h = _masked_rmsnorm(res_ref[...], nw_ref[...], m * tm, seq_p, nbatch)
        h_scr[...] = h.astype(BF16)

    @pl.when(m == 0)
    def _():
        carry_scr[f] = jnp.zeros(carry_scr.shape[1:], F32)

    h = h_scr[...]
    gate = jnp.dot(h, wg_ref[...], preferred_element_type=F32)
    val = jnp.dot(h, wv_ref[...], preferred_element_type=F32)
    prev = carry_scr[f]
    rows = lax.broadcasted_iota(jnp.int32, (tm, 1), 0)
    g1 = jnp.where(rows == 0, prev[7:8], pltpu.roll(gate, 1, axis=0))
    g2 = jnp.where(rows == 0, prev[6:7], jnp.where(rows == 1, prev[7:8], pltpu.roll(gate, 2, axis=0)))
    carry_scr[f] = gate[tm - SUBLANES:]
    cw = cw_ref[...]
    pre = cb_ref[...] + cw[0:1] * g2 + cw[1:2] * g1 + cw[2:3] * gate
    act = pre * _sigmoid(pre) * val
    contrib = jnp.dot(act.astype(BF16), wd_ref[...], preferred_element_type=F32)

    @pl.when(f == 0)
    def _():
        o_ref[...] = res_ref[...] + contrib

    @pl.when(f > 0)
    def _():
        o_ref[...] += contrib


def _ffn(res, norm_w, w_up_bf16, conv_w, conv_b, w_down_bf16, *, seq_p, nbatch):
    T, D = res.shape
    tm = _row_tile(T, 768)
    tf = FFN_CHUNK
    nf = FFN_DIM // tf
    return pl.pallas_call(
        functools.partial(_ffn_kernel, tm=tm, seq_p=seq_p, nbatch=nbatch),
        grid=(T // tm, nf),
        in_specs=[pl.BlockSpec((tm, D), lambda m, f: (m, 0)),
                  pl.BlockSpec((1, D), lambda m, f: (0, 0)),
                  pl.BlockSpec((D, tf), lambda m, f: (0, f)),
                  pl.BlockSpec((D, tf), lambda m, f: (0, nf + f)),
                  pl.BlockSpec((FFN_CONV, tf), lambda m, f: (0, f)),
                  pl.BlockSpec((1, tf), lambda m, f: (0, f)),
                  pl.BlockSpec((tf, D), lambda m, f: (f, 0))],
        out_specs=pl.BlockSpec((tm, D), lambda m, f: (m, 0)),
        out_shape=jax.ShapeDtypeStruct((T, D), F32),
        scratch_shapes=[pltpu.VMEM((tm, D), BF16), pltpu.VMEM((nf, SUBLANES, tf), F32)],
        compiler_params=_params("arbitrary", "arbitrary"),
        name="ffn",
    )(res, norm_w.reshape(1, D), w_up_bf16, w_up_bf16, conv_w, conv_b.reshape(1, FFN_DIM), w_down_bf16)


def _attn_kernel(sinks_ref, q_ref, kvc_ref, kvp_ref, kvm_ref, qw_ref, kw_ref, o_ref):
    n = pl.program_id(1)
    scale = HEAD_DIM ** -0.5
    qw = qw_ref[...]
    kw = kw_ref[...]

    def head_norm(xh, w):
        ms = jnp.mean(xh * xh, axis=-1, keepdims=True)
        return xh * lax.rsqrt(ms + QK_EPS) * w

    qi = lax.broadcasted_iota(jnp.int32, (BLOCK, 3 * BLOCK), 0)
    kj = lax.broadcasted_iota(jnp.int32, (BLOCK, 3 * BLOCK), 1)
    j_prev, j_cur, j_meta = kj, kj - BLOCK, kj - 2 * BLOCK
    first_prev = jnp.where(n == 0, BLOCK, jnp.where(n == 1, N_PAD, 0))
    first_cur = jnp.where(n == 0, N_PAD, 0)
    vis_prev = (kj < BLOCK) & (j_prev > qi) & (j_prev >= first_prev)
    vis_cur = (kj >= BLOCK) & (j_cur <= qi) & (j_cur >= first_cur)
    vis_meta = (j_meta >= N_PAD) & (n * BLOCK + qi - j_meta >= WINDOW)
    mask = vis_prev | vis_cur | vis_meta

    kvs = (kvp_ref[...], kvc_ref[...], kvm_ref[...])
    q = q_ref[...]
    outs = []
    for h in range(ATTN_KV_HEADS):
        ksl = slice(h * HEAD_DIM, (h + 1) * HEAD_DIM)
        vsl = slice(KV_COLS + h * HEAD_DIM, KV_COLS + (h + 1) * HEAD_DIM)
        k_all = jnp.concatenate([head_norm(kv[:, ksl], kw) for kv in kvs], axis=0).astype(BF16)
        v_all = jnp.concatenate([kv[:, vsl] for kv in kvs], axis=0).astype(BF16)
        for g in range(ATTN_GROUP):
            hq = h * ATTN_GROUP + g
            qh = head_norm(q[:, hq * HEAD_DIM:(hq + 1) * HEAD_DIM], qw) * scale
            s = _dot_nt(qh, k_all)
            s = jnp.where(mask, s, -jnp.inf)
            sink = sinks_ref[hq]
            mx = jnp.maximum(jnp.max(s, axis=-1, keepdims=True), sink)
            p = jnp.exp(s - mx)
            den = jnp.sum(p, axis=-1, keepdims=True) + jnp.exp(sink - mx)
            outs.append(jnp.dot(p.astype(BF16), v_all, preferred_element_type=F32) / den)
    o_ref[...] = jnp.concatenate(outs, axis=-1).astype(o_ref.dtype)


def _attention(zin, sinks, q_norm_w, k_norm_w, *, nbatch, nblk):
    T = zin.shape[0]
    q_blk = 3 * RWKV_DIM // Q_COLS
    kv_blk = (3 * RWKV_DIM + Q_COLS) // (2 * KV_COLS)
    row = lambda b, n: b * nblk + n
    return pl.pallas_call(
        _attn_kernel,
        grid=(nbatch, nblk),
        in_specs=[pl.BlockSpec(memory_space=pltpu.SMEM),
                  pl.BlockSpec((BLOCK, Q_COLS), lambda b, n: (row(b, n), q_blk)),
                  pl.BlockSpec((BLOCK, 2 * KV_COLS), lambda b, n: (row(b, n), kv_blk)),
                  pl.BlockSpec((BLOCK, 2 * KV_COLS), lambda b, n: (row(b, jnp.maximum(n - 1, 0)), kv_blk)),
                  pl.BlockSpec((BLOCK, 2 * KV_COLS), lambda b, n: (row(b, 0), kv_blk)),
                  pl.BlockSpec((1, HEAD_DIM), lambda b, n: (0, 0)),
                  pl.BlockSpec((1, HEAD_DIM), lambda b, n: (0, 0))],
        out_specs=pl.BlockSpec((BLOCK, Q_COLS), lambda b, n: (row(b, n), 0)),
        out_shape=jax.ShapeDtypeStruct((T, Q_COLS), BF16),
        compiler_params=_params("arbitrary", "arbitrary"),
        name="swa_attention",
    )(sinks, zin, zin, zin, zin, q_norm_w.reshape(1, HEAD_DIM), k_norm_w.reshape(1, HEAD_DIM))


def _pair_blockdiag(x):
    lane = lax.broadcasted_iota(jnp.int32, x.shape, 1)
    return jnp.concatenate([jnp.where(lane < RWKV_HEAD, x, 0.0), jnp.where(lane >= RWKV_HEAD, x, 0.0)], axis=0)


def _pair_nn(y, x):
    return _dot(y, _pair_blockdiag(x))


def _pair_nt(y, x):
    return _dot_nt(y, _pair_blockdiag(x))


def _pair_tn(y, x):
    full = _dot_tn(y, x)
    lane = lax.broadcasted_iota(jnp.int32, (RWKV_HEAD, LANES), 1)
    return jnp.where(lane < RWKV_HEAD, full[:RWKV_HEAD], full[RWKV_HEAD:])


def _pair_segsum(x, ones_bd):
    cols = []
    for p in range(x.shape[1] // LANES):
        cols.append(_dot_exact_lhs(x[:, p * LANES:(p + 1) * LANES], ones_bd))
    return jnp.concatenate(cols, axis=-1)


def _rwkv_kernel(rkv_ref, lora_ref, mu_rkv_ref, mu_lora_ref, wup_ref, vec_ref, o_ref,
                 s_scr, carry_rkv, carry_lora, *, tb):
    n = pl.program_id(1)
    L = RWKV_CHUNK
    npair = RWKV_DIM // LANES

    @pl.when(n == 0)
    def _():
        s_scr[...] = jnp.zeros(s_scr.shape, F32)
        carry_rkv[...] = jnp.zeros(carry_rkv.shape, F32)
        carry_lora[...] = jnp.zeros(carry_lora.shape, F32)

    rows = lax.broadcasted_iota(jnp.int32, (tb, 1), 0)
    valid = (n > 0) | (rows >= N_PAD)

    def token_shift(z_ref, carry, mu_ref):
        z = z_ref[...]
        prev = jnp.where(rows == 0, carry[SUBLANES - 1:SUBLANES], pltpu.roll(z, 1, axis=0))
        carry[...] = z[tb - SUBLANES:]
        return jnp.where(valid, z + (prev - z) * mu_ref[...], 0.0)

    zr = token_shift(rkv_ref, carry_rkv, mu_rkv_ref)
    zl = token_shift(lora_ref, carry_lora, mu_lora_ref)
    r, k, v = zr[:, :RWKV_DIM], zr[:, RWKV_DIM:2 * RWKV_DIM], zr[:, 2 * RWKV_DIM:]

    col = lax.broadcasted_iota(jnp.int32, zl.shape, 1)
    lora_act = jnp.where(col < DECAY_LORA, jnp.tanh(zl), jnp.where(col < DECAY_LORA + AAA_LORA, zl, _sigmoid(zl)))
    up = jnp.dot(lora_act.astype(BF16), wup_ref[...], preferred_element_type=F32)
    vec = vec_ref[...]
    w0, a0, k_k, k_a, r_k, ln_w, ln_b = (vec[i:i + 1] for i in range(7))
    w_log = -jnp.exp(-_softplus(-(w0 + up[:, :RWKV_DIM])) - 0.5)
    w_log = jnp.where(valid, w_log, 0.0)
    a = _sigmoid(a0 + up[:, RWKV_DIM:2 * RWKV_DIM])
    g = up[:, 2 * RWKV_DIM:]

    li = lax.broadcasted_iota(jnp.int32, (LANES, LANES), 0)
    lj = lax.broadcasted_iota(jnp.int32, (LANES, LANES), 1)
    ones_bd = jnp.where((li < RWKV_HEAD) == (lj < RWKV_HEAD), 1.0, 0.0).astype(BF16)

    kk = k * k_k
    kk = kk / jnp.maximum(jnp.sqrt(_pair_segsum(kk * kk, ones_bd)), 1e-12)
    k2 = k * (1.0 + (a - 1.0) * k_a)
    bonus = _pair_segsum(r * k2 * r_k, ones_bd)
    kka = kk * a

    t_i = lax.broadcasted_iota(jnp.int32, (L, LANES), 0)
    s_i = lax.broadcasted_iota(jnp.int32, (L, LANES), 1) & (RWKV_HEAD - 1)
    strict = s_i < t_i
    incl = s_i <= t_i
    same16 = (t_i >> 4) == (s_i >> 4)
    same32 = (t_i >> 5) == (s_i >> 5)
    eye = jnp.where(s_i == t_i, 1.0, 0.0)
    ti2 = lax.broadcasted_iota(jnp.int32, (L, L), 0)
    si2 = lax.broadcasted_iota(jnp.int32, (L, L), 1)
    tri_incl = jnp.where(si2 <= ti2, 1.0, 0.0).astype(BF16)

    y_chunks = []
    for c in range(tb // L):
        sl = slice(c * L, (c + 1) * L)
        wl = w_log[sl]
        cum = _dot_exact_rhs(tri_incl, wl)
        cum_end = cum[L - 1:L]
        e_w = jnp.exp(cum)
        e_iw = jnp.exp(-cum)
        e_prev = jnp.exp(cum - wl)
        e_end = jnp.exp(cum_end - cum)
        w_end = jnp.exp(cum_end)
        ah_all = -kk[sl] * e_prev
        bh_all = kka[sl] * e_iw
        kh_all = k2[sl] * e_iw
        rh_all = r[sl] * e_w
        bt_all = kka[sl] * e_end
        kt_all = k2[sl] * e_end
        v_all = v[sl]
        y_pairs = []
        for p in range(npair):
            ps = slice(p * LANES, (p + 1) * LANES)
            ah, bh, kh, rh, bt, kt, vp = (x[:, ps] for x in (ah_all, bh_all, kh_all, rh_all, bt_all, kt_all, v_all))
            ar = jnp.concatenate([ah, rh], axis=0)
            mb = _pair_nt(ar, bh)
            mk = _pair_nt(ar, kh)
            a_ab = jnp.where(strict, mb[:L], 0.0)
            a_rb = jnp.where(incl, mb[L:], 0.0)
            a_ak = jnp.where(strict, mk[:L], 0.0)
            a_rk = jnp.where(incl, mk[L:], 0.0)
            pw = jnp.where(same16, a_ab, 0.0)
            tinv = eye + pw
            for _ in range(3):
                pw = _pair_nn(pw, pw)
                tinv = tinv + _pair_nn(tinv, pw)
            off32 = jnp.where(same32 & jnp.logical_not(same16), a_ab, 0.0)
            tinv = tinv + _pair_nn(tinv, _pair_nn(off32, tinv))
            off64 = jnp.where(same32, 0.0, a_ab)
            tinv = tinv + _pair_nn(tinv, _pair_nn(off64, tinv))
            ta = _pair_nn(tinv, ah)
            gm = _pair_nn(tinv, _pair_nn(a_ak, vp))
            rt = rh + _pair_nn(a_rb, ta)
            y_intra = _pair_nn(a_rb, gm) + _pair_nn(a_rk, vp)
            phi = _pair_tn(ta, bt)
            psi = _pair_tn(gm, bt) + _pair_tn(vp, kt)
            s0 = s_scr[p]
            y_pairs.append(_pair_nt(rt, s0) + y_intra)
            s_scr[p] = s0 * w_end[:, ps] + _pair_nn(s0, phi) + psi
        y_chunks.append(jnp.concatenate(y_pairs, axis=-1))
    y = jnp.concatenate(y_chunks, axis=0)

    inv_n = 1.0 / RWKV_HEAD
    mu = _pair_segsum(y, ones_bd) * inv_n
    d = y - mu
    var = _pair_segsum(d * d, ones_bd) * inv_n
    yn = d * lax.rsqrt(var + RWKV_GN_EPS) * ln_w + ln_b
    o_ref[...] = ((yn + bonus * v) * g).astype(o_ref.dtype)


def _rwkv(zin, mu_rkv, mu_lora, wup_bf16, vecs, *, nbatch, nblk):
    T = zin.shape[0]
    tb = BLOCK
    lora_blk = (AR_COLS - LORA_PAD) // LORA_PAD
    row = lambda b, n: b * nblk + n
    return pl.pallas_call(
        functools.partial(_rwkv_kernel, tb=tb),
        grid=(nbatch, nblk),
        in_specs=[pl.BlockSpec((tb, 3 * RWKV_DIM), lambda b, n: (row(b, n), 0)),
                  pl.BlockSpec((tb, LORA_PAD), lambda b, n: (row(b, n), lora_blk)),
                  pl.BlockSpec((1, 3 * RWKV_DIM), lambda b, n: (0, 0)),
                  pl.BlockSpec((1, LORA_PAD), lambda b, n: (0, 0)),
                  pl.BlockSpec((LORA_PAD, 3 * RWKV_DIM), lambda b, n: (0, 0)),
                  pl.BlockSpec((SUBLANES, RWKV_DIM), lambda b, n: (0, 0))],
        out_specs=pl.BlockSpec((tb, RWKV_DIM), lambda b, n: (row(b, n), 0)),
        out_shape=jax.ShapeDtypeStruct((T, RWKV_DIM), BF16),
        scratch_shapes=[pltpu.VMEM((RWKV_DIM // LANES, RWKV_HEAD, LANES), F32),
                        pltpu.VMEM((SUBLANES, 3 * RWKV_DIM), F32),
                        pltpu.VMEM((SUBLANES, LORA_PAD), F32)],
        compiler_params=_params("arbitrary", "arbitrary"),
        name="rwkv7",
    )(zin, zin, mu_rkv, mu_lora, wup_bf16, vecs)


def _conv4_silu(xc, carry, w, b, valid):
    full = jnp.concatenate([carry[...], xc], axis=0)
    acc = b
    for j in range(SSD_CONV):
        k = SSD_CONV - 1 - j
        term = xc if k == 0 else pltpu.roll(full, k, axis=0)[SUBLANES:]
        acc = acc + w[j:j + 1] * term
    carry[...] = xc[BLOCK - SUBLANES:]
    return jnp.where(valid, acc * _sigmoid(acc), 0.0)


def _ssd_kernel(z_ref, x_ref, b_ref, c_ref, dt_ref, cwx_ref, cwb_ref, cwc_ref, cbx_ref, cbb_ref, cbc_ref,
                dtb_ref, alog_ref, dskip_ref, nw_ref, exp_ref, o_ref,
                st_scr, carry_x, carry_b, carry_c):
    n = pl.program_id(1)

    @pl.when(n == 0)
    def _():
        st_scr[...] = jnp.zeros(st_scr.shape, F32)
        carry_x[...] = jnp.zeros(carry_x.shape, F32)
        carry_b[...] = jnp.zeros(carry_b.shape, F32)
        carry_c[...] = jnp.zeros(carry_c.shape, F32)

    rows = lax.broadcasted_iota(jnp.int32, (BLOCK, 1), 0)
    valid = (n > 0) | (rows >= N_PAD)
    x = _conv4_silu(x_ref[...], carry_x, cwx_ref[...], cbx_ref[...], valid)
    bm = _conv4_silu(b_ref[...], carry_b, cwb_ref[...], cbb_ref[...], valid)
    cm = _conv4_silu(c_ref[...], carry_c, cwc_ref[...], cbc_ref[...], valid)

    dt = jnp.where(valid, _softplus(dt_ref[...] + dtb_ref[...]), 0.0)
    adt = dt * (-jnp.exp(alog_ref[...]))
    ti = lax.broadcasted_iota(jnp.int32, (BLOCK, BLOCK), 0)
    si = lax.broadcasted_iota(jnp.int32, (BLOCK, BLOCK), 1)
    causal = si <= ti
    cum = _dot_exact_rhs(jnp.where(causal, 1.0, 0.0).astype(BF16), adt)
    cum_t = cum.T
    ecum = jnp.exp(cum)
    toend = jnp.exp(cum[BLOCK - 1:BLOCK] - cum)
    ex = _dot_exact_lhs(jnp.concatenate([dt, ecum, toend], axis=0), exp_ref[...])
    dt_x, ecum_x, toend_x = ex[:BLOCK], ex[BLOCK:2 * BLOCK], ex[2 * BLOCK:]
    xdt = x * dt_x
    xend = xdt * toend_x
    lane = lax.broadcasted_iota(jnp.int32, (BLOCK, LANES), 1)
    lo_half = lane < SSD_HEAD_DIM

    y_groups = []
    for g in range(SSD_GROUPS):
        gs = slice(g * SSD_GROUP_COLS, (g + 1) * SSD_GROUP_COLS)
        bg = bm[:, g * SSD_STATE:(g + 1) * SSD_STATE]
        cg = cm[:, g * SSD_STATE:(g + 1) * SSD_STATE]
        cb = _dot_nt(cg, bg)
        st = st_scr[g]
        y_inter = _dot(cg, st) * ecum_x[:, gs]
        y_pairs = []
        for q in range(SSD_HPG // 2):
            mats = []
            for j in (g * SSD_HPG + 2 * q, g * SSD_HPG + 2 * q + 1):
                seg = cum[:, j:j + 1] - cum_t[j:j + 1, :]
                mats.append(cb * jnp.exp(jnp.where(causal, seg, -jnp.inf)))
            ll = jnp.concatenate(mats, axis=1).astype(BF16)
            c0 = g * SSD_GROUP_COLS + q * LANES
            xp = xdt[:, c0:c0 + LANES]
            bdx = jnp.concatenate([jnp.where(lo_half, xp, 0.0), jnp.where(lo_half, 0.0, xp)], axis=0)
            y_pairs.append(jnp.dot(ll, bdx.astype(BF16), preferred_element_type=F32))
        y_intra = jnp.concatenate(y_pairs, axis=-1)
        st_scr[g] = st * ecum_x[BLOCK - 1:BLOCK, gs] + _dot_tn(bg, xend[:, gs])
        y_groups.append(y_intra + y_inter + dskip_ref[:, gs] * x[:, gs])
    z = z_ref[...]
    nw = nw_ref[...]
    outs = []
    for g in range(SSD_GROUPS):
        gs = slice(g * SSD_GROUP_COLS, (g + 1) * SSD_GROUP_COLS)
        zg = z[:, gs]
        yg = y_groups[g] * (zg * _sigmoid(zg))
        yg = yg * lax.rsqrt(jnp.mean(yg * yg, axis=-1, keepdims=True) + SSD_NORM_EPS)
        outs.append((yg * nw[:, gs]).astype(o_ref.dtype))
    o_ref[...] = jnp.concatenate(outs, axis=-1)


def _ssd(zx, conv_w, conv_b, dt_bias, a_log, d_skip, norm_w, *, nbatch, nblk):
    T = zx.shape[0]
    row = lambda b, n: b * nblk + n
    pad_h = lambda t: jnp.pad(t.astype(F32), (0, SSD_DT_PAD - SSD_HEADS)).reshape(1, SSD_DT_PAD)
    head = lax.broadcasted_iota(jnp.int32, (SSD_DT_PAD, SSD_INNER), 0)
    colh = lax.broadcasted_iota(jnp.int32, (SSD_DT_PAD, SSD_INNER), 1) // SSD_HEAD_DIM
    expand = (head == colh).astype(BF16)
    cb2 = conv_b.reshape(1, -1)
    xs, bs, cs = slice(0, SSD_INNER), slice(SSD_INNER, SSD_INNER + SSD_BC_COLS), slice(SSD_INNER + SSD_BC_COLS, None)
    full = lambda shape: pl.BlockSpec(shape, lambda b, n: (0, 0))
    x_blk = 1
    b_blk = 2 * SSD_INNER // SSD_BC_COLS
    dt_blk = (2 * SSD_INNER + 2 * SSD_BC_COLS) // SSD_DT_PAD
    return pl.pallas_call(
        _ssd_kernel,
        grid=(nbatch, nblk),
        in_specs=[pl.BlockSpec((BLOCK, SSD_INNER), lambda b, n: (row(b, n), 0)),
                  pl.BlockSpec((BLOCK, SSD_INNER), lambda b, n: (row(b, n), x_blk)),
                  pl.BlockSpec((BLOCK, SSD_BC_COLS), lambda b, n: (row(b, n), b_blk)),
                  pl.BlockSpec((BLOCK, SSD_BC_COLS), lambda b, n: (row(b, n), b_blk + 1)),
                  pl.BlockSpec((BLOCK, SSD_DT_PAD), lambda b, n: (row(b, n), dt_blk)),
                  full((SSD_CONV, SSD_INNER)), full((SSD_CONV, SSD_BC_COLS)), full((SSD_CONV, SSD_BC_COLS)),
                  full((1, SSD_INNER)), full((1, SSD_BC_COLS)), full((1, SSD_BC_COLS)),
                  full((1, SSD_DT_PAD)), full((1, SSD_DT_PAD)), full((1, SSD_INNER)), full((1, SSD_INNER)),
                  full((SSD_DT_PAD, SSD_INNER))],
        out_specs=pl.BlockSpec((BLOCK, SSD_INNER), lambda b, n: (row(b, n), 0)),
        out_shape=jax.ShapeDtypeStruct((T, SSD_INNER), BF16),
        scratch_shapes=[pltpu.VMEM((SSD_GROUPS, SSD_STATE, SSD_GROUP_COLS), F32),
                        pltpu.VMEM((SUBLANES, SSD_INNER), F32),
                        pltpu.VMEM((SUBLANES, SSD_BC_COLS), F32),
                        pltpu.VMEM((SUBLANES, SSD_BC_COLS), F32)],
        compiler_params=_params("arbitrary", "arbitrary"),
        name="ssd",
    )(zx, zx, zx, zx, zx, conv_w[:, xs], conv_w[:, bs], conv_w[:, cs], cb2[:, xs], cb2[:, bs], cb2[:, cs],
      pad_h(dt_bias), pad_h(a_log), jnp.repeat(d_skip.astype(F32), SSD_HEAD_DIM).reshape(1, SSD_INNER),
      norm_w.reshape(1, SSD_INNER), expand)


def _pack_ar_w_in(w):
    pad = jnp.zeros((w.shape[0], LORA_PAD - LORA_COLS), w.dtype)
    lora0 = ATTN_COLS + 3 * RWKV_DIM
    return jnp.concatenate([w[:, ATTN_COLS:lora0], w[:, :ATTN_COLS], w[:, lora0:], pad], axis=1).astype(BF16)


def _pack_lora_up(w_up, a_up, g_up):
    wup = jnp.zeros((LORA_PAD, 3 * RWKV_DIM), F32)
    wup = wup.at[:DECAY_LORA, :RWKV_DIM].set(w_up)
    wup = wup.at[DECAY_LORA:DECAY_LORA + AAA_LORA, RWKV_DIM:2 * RWKV_DIM].set(a_up)
    wup = wup.at[DECAY_LORA + AAA_LORA:LORA_COLS, 2 * RWKV_DIM:].set(g_up)
    return wup.astype(BF16)


def kernel(x, meta_tokens, mix_norm_w, ffn_norm_w, ar_w_in, ar_shift_mu, attn_q_norm_w, attn_k_norm_w, attn_sinks, rwkv_w0, rwkv_w_up, rwkv_a0, rwkv_a_up, rwkv_g_up, rwkv_k_k, rwkv_k_a, rwkv_r_k, rwkv_ln_w, rwkv_ln_b, ar_w_out, ssd_w_in, ssd_conv_w, ssd_conv_b, ssd_dt_bias, ssd_a_log, ssd_d, ssd_norm_w, ssd_w_out, ffn_w_up, ffn_conv_w, ffn_conv_b, ffn_w_down):
    nbatch, seq, d = x.shape
    assert d == D_MODEL and seq % BLOCK == 0
    seq_p = N_PAD + N_META + seq
    nblk = seq_p // BLOCK
    depth = mix_norm_w.shape[0]
    res = jnp.concatenate([
        jnp.zeros((nbatch, N_PAD, d), x.dtype),
        jnp.broadcast_to(meta_tokens.astype(x.dtype)[None], (nbatch, N_META, d)),
        x], axis=1).reshape(nbatch * seq_p, d)
    kw = dict(seq_p=seq_p, nbatch=nbatch)
    for layer in range(depth):
        i = layer // 2
        if layer % 2 == 0:
            zin = _norm_matmul(res, mix_norm_w[layer], _pack_ar_w_in(ar_w_in[i]), tn=1024, **kw)
            attn = _attention(zin, attn_sinks[i].astype(F32), attn_q_norm_w[i], attn_k_norm_w[i],
                              nbatch=nbatch, nblk=nblk)
            mu = ar_shift_mu[i]
            mu_rkv = mu[:3 * RWKV_DIM].reshape(1, -1)
            mu_lora = jnp.pad(mu[3 * RWKV_DIM:], (0, LORA_PAD - LORA_COLS)).reshape(1, -1)
            vecs = jnp.stack([rwkv_w0[i], rwkv_a0[i], rwkv_k_k[i], rwkv_k_a[i], rwkv_r_k[i].reshape(-1),
                              rwkv_ln_w[i], rwkv_ln_b[i], jnp.zeros((RWKV_DIM,), F32)]).astype(F32)
            tm = _rwkv(zin, mu_rkv, mu_lora, _pack_lora_up(rwkv_w_up[i], rwkv_a_up[i], rwkv_g_up[i]), vecs,
                       nbatch=nbatch, nblk=nblk)
            res = _proj_residual([attn, tm], ar_w_out[i].astype(BF16), res)
        else:
            w_in = jnp.pad(ssd_w_in[i], ((0, 0), (0, SSD_DT_PAD - SSD_HEADS))).astype(BF16)
            zx = _norm_matmul(res, mix_norm_w[layer], w_in, tn=1152, **kw)
            y = _ssd(zx, ssd_conv_w[i], ssd_conv_b[i], ssd_dt_bias[i], ssd_a_log[i], ssd_d[i], ssd_norm_w[i],
                     nbatch=nbatch, nblk=nblk)
            res = _proj_residual([y], ssd_w_out[i].astype(BF16), res)
        res = _ffn(res, ffn_norm_w[layer], ffn_w_up[layer].astype(BF16), ffn_conv_w[layer], ffn_conv_b[layer],
                   ffn_w_down[layer].astype(BF16), **kw)
    return res.reshape(nbatch, seq_p, d)[:, N_PAD + N_META:]
```

```python
import functools

import jax
import jax.numpy as jnp
from jax import lax
from jax.experimental import pallas as pl
from jax.experimental.pallas import tpu as pltpu

F32 = jnp.float32
BF16 = jnp.bfloat16

D_MODEL = 2048
N_META = 16
BLOCK = 128
N_PAD = BLOCK - N_META
NORM_EPS = 1e-6

HEAD_DIM = 64
ATTN_HEADS = 16
ATTN_KV_HEADS = 4
ATTN_GROUP = 4
WINDOW = 128
QK_EPS = 1e-6
Q_COLS = ATTN_HEADS * HEAD_DIM
KV_COLS = ATTN_KV_HEADS * HEAD_DIM
ATTN_COLS = Q_COLS + 2 * KV_COLS

RWKV_DIM = 1024
RWKV_HEAD = 64
DECAY_LORA = 64
AAA_LORA = 64
GATE_LORA = 160
LORA_COLS = DECAY_LORA + AAA_LORA + GATE_LORA
LORA_PAD = 512
RWKV_GN_EPS = 64e-5
RWKV_CHUNK = 64
AR_COLS = 3 * RWKV_DIM + Q_COLS + 2 * KV_COLS + LORA_PAD

SSD_INNER = 4096
SSD_HEAD_DIM = 64
SSD_HEADS = 64
SSD_GROUPS = 8
SSD_HPG = 8
SSD_STATE = 128
SSD_CONV = 4
SSD_GROUP_COLS = SSD_INNER // SSD_GROUPS
SSD_BC_COLS = SSD_GROUPS * SSD_STATE
SSD_DT_PAD = 128
SSD_COLS = 2 * SSD_INNER + 2 * SSD_BC_COLS + SSD_DT_PAD
SSD_NORM_EPS = 1e-5

FFN_DIM = 5632
FFN_CONV = 3
FFN_CHUNK = 512

LANES = 128
SUBLANES = 8
VMEM_LIMIT = 56 * 1024 * 1024


def _params(*sem):
    return pltpu.CompilerParams(dimension_semantics=sem, vmem_limit_bytes=VMEM_LIMIT)


def _row_tile(total, target):
    best = None
    for t in range(16, min(total, target) + 1, 16):
        if total % t == 0:
            best = t
    assert best is not None, (total, target)
    return best


def _dot(a, b):
    return jnp.dot(a.astype(BF16), b.astype(BF16), preferred_element_type=F32)


def _dot_nt(a, b):
    return lax.dot_general(a.astype(BF16), b.astype(BF16), (((1,), (1,)), ((), ())),
                           preferred_element_type=F32)


def _dot_tn(a, b):
    return lax.dot_general(a.astype(BF16), b.astype(BF16), (((0,), (0,)), ((), ())),
                           preferred_element_type=F32)


def _split3(x):
    hi = x.astype(BF16)
    r1 = x - hi.astype(F32)
    mid = r1.astype(BF16)
    lo = (r1 - mid.astype(F32)).astype(BF16)
    return hi, mid, lo


def _dot_exact_rhs(sel_bf16, x):
    hi, mid, lo = _split3(x)
    d = lambda p: jnp.dot(sel_bf16, p, preferred_element_type=F32)
    return d(hi) + d(mid) + d(lo)


def _dot_exact_lhs(x, sel_bf16):
    hi, mid, lo = _split3(x)
    d = lambda p: jnp.dot(p, sel_bf16, preferred_element_type=F32)
    return d(hi) + d(mid) + d(lo)


def _sigmoid(x):
    return 1.0 / (1.0 + jnp.exp(-x))


def _softplus(x):
    return jnp.maximum(x, 0.0) + jnp.log(1.0 + jnp.exp(-jnp.abs(x)))


def _valid_rows(row0, nrows, seq_p, nbatch):
    rows = row0 + lax.broadcasted_iota(jnp.int32, (nrows, 1), 0)
    pos = rows
    for b in range(1, nbatch):
        pos = jnp.where(rows >= b * seq_p, rows - b * seq_p, pos)
    return pos >= N_PAD


def _masked_rmsnorm(x, w, row0, seq_p, nbatch):
    ms = jnp.mean(x * x, axis=-1, keepdims=True)
    y = x * lax.rsqrt(ms + NORM_EPS) * w
    return jnp.where(_valid_rows(row0, x.shape[0], seq_p, nbatch), y, 0.0)


def _norm_mm_kernel(x_ref, nw_ref, w_ref, o_ref, h_scr, *, tm, seq_p, nbatch):
    @pl.when(pl.program_id(1) == 0)
    def _():
        h = _masked_rmsnorm(x_ref[...], nw_ref[...], pl.program_id(0) * tm, seq_p, nbatch)
        h_scr[...] = h.astype(BF16)

    o_ref[...] = jnp.dot(h_scr[...], w_ref[...], preferred_element_type=F32)


def _norm_matmul(res, norm_w, w_bf16, *, tn, seq_p, nbatch):
    T, D = res.shape
    N = w_bf16.shape[1]
    tm = _row_tile(T, 768)
    assert N % tn == 0
    return pl.pallas_call(
        functools.partial(_norm_mm_kernel, tm=tm, seq_p=seq_p, nbatch=nbatch),
        grid=(T // tm, N // tn),
        in_specs=[pl.BlockSpec((tm, D), lambda m, n: (m, 0)),
                  pl.BlockSpec((1, D), lambda m, n: (0, 0)),
                  pl.BlockSpec((D, tn), lambda m, n: (0, n))],
        out_specs=pl.BlockSpec((tm, tn), lambda m, n: (m, n)),
        out_shape=jax.ShapeDtypeStruct((T, N), F32),
        scratch_shapes=[pltpu.VMEM((tm, D), BF16)],
        compiler_params=_params("arbitrary", "arbitrary"),
        name="norm_in_proj",
    )(res, norm_w.reshape(1, D), w_bf16)


def _proj_res_kernel(*refs, n_lhs):
    lhs = refs[:n_lhs]
    ws = refs[n_lhs:2 * n_lhs]
    res_ref, o_ref = refs[2 * n_lhs], refs[2 * n_lhs + 1]
    acc = res_ref[...]
    for l, w in zip(lhs, ws):
        acc = acc + jnp.dot(l[...], w[...], preferred_element_type=F32)
    o_ref[...] = acc


def _proj_residual(lhs_list, w_bf16, res, *, tn=512):
    T, D = res.shape
    n_lhs = len(lhs_list)
    kw = lhs_list[0].shape[1]
    assert all(l.shape == (T, kw) for l in lhs_list) and w_bf16.shape == (n_lhs * kw, D)
    tm = _row_tile(T, 768)
    in_specs = [pl.BlockSpec((tm, kw), lambda m, n: (m, 0)) for _ in range(n_lhs)]
    in_specs += [pl.BlockSpec((kw, tn), functools.partial(lambda i, m, n: (i, n), i)) for i in range(n_lhs)]
    in_specs += [pl.BlockSpec((tm, tn), lambda m, n: (m, n))]
    return pl.pallas_call(
        functools.partial(_proj_res_kernel, n_lhs=n_lhs),
        grid=(T // tm, D // tn),
        in_specs=in_specs,
        out_specs=pl.BlockSpec((tm, tn), lambda m, n: (m, n)),
        out_shape=jax.ShapeDtypeStruct((T, D), F32),
        compiler_params=_params("arbitrary", "arbitrary"),
        name="out_proj_residual",
    )(*lhs_list, *([w_bf16] * n_lhs), res)


def _ffn_kernel(res_ref, nw_ref, wg_ref, wv_ref, cw_ref, cb_ref, wd_ref, o_ref, h_scr, carry_scr,
                *, tm, seq_p, nbatch):
    m = pl.program_id(0)
    f = pl.program_id(1)

    @pl.when(f == 0)
    def _():
        h = _masked_rmsnorm(res_ref[...], nw_ref[...], m * tm, seq_p, nbatch)
        h_scr[...] = h.astype(BF16)

    @pl.when(m == 0)
    def _():
        carry_scr[f] = jnp.zeros(carry_scr.shape[1:], F32)

    h = h_scr[...]
    gate = jnp.dot(h, wg_ref[...], preferred_element_type=F32)
    val = jnp.dot(h, wv_ref[...], preferred_element_type=F32)
    prev = carry_scr[f]
    rows = lax.broadcasted_iota(jnp.int32, (tm, 1), 0)
    g1 = jnp.where(rows == 0, prev[7:8], pltpu.roll(gate, 1, axis=0))
    g2 = jnp.where(rows == 0, prev[6:7], jnp.where(rows == 1, prev[7:8], pltpu.roll(gate, 2, axis=0)))
    carry_scr[f] = gate[tm - SUBLANES:]
    cw = cw_ref[...]
    pre = cb_ref[...] + cw[0:1] * g2 + cw[1:2] * g1 + cw[2:3] * gate
    act = pre * _sigmoid(pre) * val
    contrib = jnp.dot(act.astype(BF16), wd_ref[...], preferred_element_type=F32)

    @pl.when(f == 0)
    def _():
        o_ref[...] = res_ref[...] + contrib

    @pl.when(f > 0)
    def _():
        o_ref[...] += contrib


def _ffn(res, norm_w, w_up_bf16, conv_w, conv_b, w_down_bf16, *, seq_p, nbatch):
    T, D = res.shape
    tm = _row_tile(T, 768)
    tf = FFN_CHUNK
    nf = FFN_DIM // tf
    return pl.pallas_call(
        functools.partial(_ffn_kernel, tm=tm, seq_p=seq_p, nbatch=nbatch),
        grid=(T // tm, nf),
        in_specs=[pl.BlockSpec((tm, D), lambda m, f: (m, 0)),
                  pl.BlockSpec((1, D), lambda m, f: (0, 0)),
                  pl.BlockSpec((D, tf), lambda m, f: (0, f)),
                  pl.BlockSpec((D, tf), lambda m, f: (0, nf + f)),
                  pl.BlockSpec((FFN_CONV, tf), lambda m, f: (0, f)),
                  pl.BlockSpec((1, tf), lambda m, f: (0, f)),
                  pl.BlockSpec((tf, D), lambda m, f: (f, 0))],
        out_specs=pl.BlockSpec((tm, D), lambda m, f: (m, 0)),
        out_shape=jax.ShapeDtypeStruct((T, D), F32),
        scratch_shapes=[pltpu.VMEM((tm, D), BF16), pltpu.VMEM((nf, SUBLANES, tf), F32)],
        compiler_params=_params("arbitrary", "arbitrary"),
        name="ffn",
    )(res, norm_w.reshape(1, D), w_up_bf16, w_up_bf16, conv_w, conv_b.reshape(1, FFN_DIM), w_down_bf16)


def _attn_kernel(sinks_ref, q_ref, kvc_ref, kvp_ref, kvm_ref, qw_ref, kw_ref, o_ref):
    n = pl.program_id(1)
    scale = HEAD_DIM ** -0.5
    qw = qw_ref[...]
    kw = kw_ref[...]

    def head_norm(xh, w):
        ms = jnp.mean(xh * xh, axis=-1, keepdims=True)
        return xh * lax.rsqrt(ms + QK_EPS) * w

    qi = lax.broadcasted_iota(jnp.int32, (BLOCK, 3 * BLOCK), 0)
    kj = lax.broadcasted_iota(jnp.int32, (BLOCK, 3 * BLOCK), 1)
    j_prev, j_cur, j_meta = kj, kj - BLOCK, kj - 2 * BLOCK
    first_prev = jnp.where(n == 0, BLOCK, jnp.where(n == 1, N_PAD, 0))
    first_cur = jnp.where(n == 0, N_PAD, 0)
    vis_prev = (kj < BLOCK) & (j_prev > qi) & (j_prev >= first_prev)
    vis_cur = (kj >= BLOCK) & (j_cur <= qi) & (j_cur >= first_cur)
    vis_meta = (j_meta >= N_PAD) & (n * BLOCK + qi - j_meta >= WINDOW)
    mask = vis_prev | vis_cur | vis_meta

    kvs = (kvp_ref[...], kvc_ref[...], kvm_ref[...])
    q = q_ref[...]
    outs = []
    for h in range(ATTN_KV_HEADS):
        ksl = slice(h * HEAD_DIM, (h + 1) * HEAD_DIM)
        vsl = slice(KV_COLS + h * HEAD_DIM, KV_COLS + (h + 1) * HEAD_DIM)
        k_all = jnp.concatenate([head_norm(kv[:, ksl], kw) for kv in kvs], axis=0).astype(BF16)
        v_all = jnp.concatenate([kv[:, vsl] for kv in kvs], axis=0).astype(BF16)
        for g in range(ATTN_GROUP):
            hq = h * ATTN_GROUP + g
            qh = head_norm(q[:, hq * HEAD_DIM:(hq + 1) * HEAD_DIM], qw) * scale
            s = _dot_nt(qh, k_all)
            s = jnp.where(mask, s, -jnp.inf)
            sink = sinks_ref[hq]
            mx = jnp.maximum(jnp.max(s, axis=-1, keepdims=True), sink)
            p = jnp.exp(s - mx)
            den = jnp.sum(p, axis=-1, keepdims=True) + jnp.exp(sink - mx)
            outs.append(jnp.dot(p.astype(BF16), v_all, preferred_element_type=F32) / den)
    o_ref[...] = jnp.concatenate(outs, axis=-1).astype(o_ref.dtype)


def _attention(zin, sinks, q_norm_w, k_norm_w, *, nbatch, nblk):
    T = zin.shape[0]
    q_blk = 3 * RWKV_DIM // Q_COLS
    kv_blk = (3 * RWKV_DIM + Q_COLS) // (2 * KV_COLS)
    row = lambda b, n: b * nblk + n
    return pl.pallas_call(
        _attn_kernel,
        grid=(nbatch, nblk),
        in_specs=[pl.BlockSpec(memory_space=pltpu.SMEM),
                  pl.BlockSpec((BLOCK, Q_COLS), lambda b, n: (row(b, n), q_blk)),
                  pl.BlockSpec((BLOCK, 2 * KV_COLS), lambda b, n: (row(b, n), kv_blk)),
                  pl.BlockSpec((BLOCK, 2 * KV_COLS), lambda b, n: (row(b, jnp.maximum(n - 1, 0)), kv_blk)),
                  pl.BlockSpec((BLOCK, 2 * KV_COLS), lambda b, n: (row(b, 0), kv_blk)),
                  pl.BlockSpec((1, HEAD_DIM), lambda b, n: (0, 0)),
                  pl.BlockSpec((1, HEAD_DIM), lambda b, n: (0, 0))],
        out_specs=pl.BlockSpec((BLOCK, Q_COLS), lambda b, n: (row(b, n), 0)),
        out_shape=jax.ShapeDtypeStruct((T, Q_COLS), BF16),
        compiler_params=_params("arbitrary", "arbitrary"),
        name="swa_attention",
    )(sinks, zin, zin, zin, zin, q_norm_w.reshape(1, HEAD_DIM), k_norm_w.reshape(1, HEAD_DIM))


def _pair_blockdiag(x):
    xb = x.astype(BF16)
    lane = lax.broadcasted_iota(jnp.int32, xb.shape, 2)
    zero = jnp.zeros_like(xb)
    return jnp.concatenate([jnp.where(lane < RWKV_HEAD, xb, zero), jnp.where(lane >= RWKV_HEAD, xb, zero)], axis=1)


def _bmm(a, b_bf16):
    return lax.dot_general(a.astype(BF16), b_bf16, (((2,), (1,)), ((0,), (0,))), preferred_element_type=F32)


def _bmm_nt(a, b_bf16):
    return lax.dot_general(a.astype(BF16), b_bf16, (((2,), (2,)), ((0,), (0,))), preferred_element_type=F32)


def _pair_pick(full):
    lane = lax.broadcasted_iota(jnp.int32, (full.shape[0], RWKV_HEAD, LANES), 2)
    return jnp.where(lane < RWKV_HEAD, full[:, :RWKV_HEAD], full[:, RWKV_HEAD:])


def _to_pairs(x, nchunk):
    L = RWKV_CHUNK
    return jnp.stack([x[c * L:(c + 1) * L, p * LANES:(p + 1) * LANES]
                      for c in range(nchunk) for p in range(x.shape[1] // LANES)], axis=0)


def _pair_segsum(x, ones_bd):
    npair = x.shape[1] // LANES
    rows = x.shape[0]
    stacked = jnp.concatenate([x[:, p * LANES:(p + 1) * LANES] for p in range(npair)], axis=0)
    s = _dot_exact_lhs(stacked, ones_bd)
    return jnp.concatenate([s[p * rows:(p + 1) * rows] for p in range(npair)], axis=-1)


def _rwkv_kernel(rkv_ref, lora_ref, mu_rkv_ref, mu_lora_ref, wup_ref, vec_ref, o_ref,
                 s_scr, carry_rkv, carry_lora, *, tb):
    n = pl.program_id(1)
    L = RWKV_CHUNK
    npair = RWKV_DIM // LANES

    @pl.when(n == 0)
    def _():
        s_scr[...] = jnp.zeros(s_scr.shape, F32)
        carry_rkv[...] = jnp.zeros(carry_rkv.shape, F32)
        carry_lora[...] = jnp.zeros(carry_lora.shape, F32)

    rows = lax.broadcasted_iota(jnp.int32, (tb, 1), 0)
    valid = (n > 0) | (rows >= N_PAD)

    def token_shift(z_ref, carry, mu_ref):
        z = z_ref[...]
        prev = jnp.where(rows == 0, carry[SUBLANES - 1:SUBLANES], pltpu.roll(z, 1, axis=0))
        carry[...] = z[tb - SUBLANES:]
        return jnp.where(valid, z + (prev - z) * mu_ref[...], 0.0)

    zr = token_shift(rkv_ref, carry_rkv, mu_rkv_ref)
    zl = token_shift(lora_ref, carry_lora, mu_lora_ref)
    r, k, v = zr[:, :RWKV_DIM], zr[:, RWKV_DIM:2 * RWKV_DIM], zr[:, 2 * RWKV_DIM:]

    col = lax.broadcasted_iota(jnp.int32, zl.shape, 1)
    lora_act = jnp.where(col < DECAY_LORA, jnp.tanh(zl), jnp.where(col < DECAY_LORA + AAA_LORA, zl, _sigmoid(zl)))
    up = jnp.dot(lora_act.astype(BF16), wup_ref[...], preferred_element_type=F32)
    vec = vec_ref[...]
    w0, a0, k_k, k_a, r_k, ln_w, ln_b = (vec[i:i + 1] for i in range(7))
    w_log = -jnp.exp(-_softplus(-(w0 + up[:, :RWKV_DIM])) - 0.5)
    w_log = jnp.where(valid, w_log, 0.0)
    a = _sigmoid(a0 + up[:, RWKV_DIM:2 * RWKV_DIM])
    g = up[:, 2 * RWKV_DIM:]

    li = lax.broadcasted_iota(jnp.int32, (LANES, LANES), 0)
    lj = lax.broadcasted_iota(jnp.int32, (LANES, LANES), 1)
    ones_bd = jnp.where((li < RWKV_HEAD) == (lj < RWKV_HEAD), 1.0, 0.0).astype(BF16)

    kk = k * k_k
    kk = kk / jnp.maximum(jnp.sqrt(_pair_segsum(kk * kk, ones_bd)), 1e-12)
    k2 = k * (1.0 + (a - 1.0) * k_a)
    bonus = _pair_segsum(r * k2 * r_k, ones_bd)
    kka = kk * a

    nchunk = tb // L
    nb = nchunk * npair
    t_i = lax.broadcasted_iota(jnp.int32, (nb, L, LANES), 1)
    s_i = lax.broadcasted_iota(jnp.int32, (nb, L, LANES), 2) & (RWKV_HEAD - 1)
    strict = s_i < t_i
    incl = s_i <= t_i
    same16 = (t_i >> 4) == (s_i >> 4)
    same32 = (t_i >> 5) == (s_i >> 5)
    eye = jnp.where(s_i == t_i, 1.0, 0.0)
    ti2 = lax.broadcasted_iota(jnp.int32, (tb, tb), 0)
    si2 = lax.broadcasted_iota(jnp.int32, (tb, tb), 1)
    lshift = L.bit_length() - 1
    tri_incl = jnp.where((si2 <= ti2) & ((si2 >> lshift) == (ti2 >> lshift)), 1.0, 0.0).astype(BF16)
    cum = _dot_exact_rhs(tri_incl, w_log)
    cum_end = jnp.concatenate(
        [jnp.broadcast_to(cum[(c + 1) * L - 1:(c + 1) * L], (L, RWKV_DIM)) for c in range(nchunk)], axis=0)
    e_w = jnp.exp(cum)
    e_iw = jnp.exp(-cum)
    e_prev = jnp.exp(cum - w_log)
    e_end = jnp.exp(cum_end - cum)
    w_end = jnp.exp(cum_end)
    ah = _to_pairs(-kk * e_prev, nchunk)
    bh = _to_pairs(kka * e_iw, nchunk)
    kh = _to_pairs(k2 * e_iw, nchunk)
    rh = _to_pairs(r * e_w, nchunk)
    bt = _to_pairs(kka * e_end, nchunk)
    kt = _to_pairs(k2 * e_end, nchunk)
    vp = _to_pairs(v, nchunk)
    vbd = _pair_blockdiag(vp)

    ar = jnp.concatenate([ah, rh], axis=1)
    m = _bmm_nt(ar, jnp.concatenate([_pair_blockdiag(bh), _pair_blockdiag(kh)], axis=1))
    a_ab = jnp.where(strict, m[:, :L, :LANES], 0.0)
    a_rb = jnp.where(incl, m[:, L:, :LANES], 0.0)
    a_ak = jnp.where(strict, m[:, :L, LANES:], 0.0)
    a_rk = jnp.where(incl, m[:, L:, LANES:], 0.0)
    kv = _bmm(jnp.concatenate([a_ak, a_rk], axis=1), vbd)
    akv, y_rk = kv[:, :L], kv[:, L:]
    a1 = jnp.where(same16, a_ab, 0.0)
    t1 = eye + a1
    a2 = _bmm(a1, _pair_blockdiag(a1))
    x = _bmm(jnp.concatenate([a2, t1], axis=1), _pair_blockdiag(a2))
    a4, t2 = x[:, :L], t1 + x[:, L:]
    x = _bmm(jnp.concatenate([a4, t2], axis=1), _pair_blockdiag(a4))
    a8, t3 = x[:, :L], t2 + x[:, L:]
    t16 = t3 + _bmm(t3, _pair_blockdiag(a8))
    off32 = jnp.where(same32 & jnp.logical_not(same16), a_ab, 0.0)
    t32 = t16 + _bmm(t16, _pair_blockdiag(_bmm(off32, _pair_blockdiag(t16))))
    off64 = jnp.where(same32, 0.0, a_ab)
    tinv = t32 + _bmm(t32, _pair_blockdiag(_bmm(off64, _pair_blockdiag(t32))))
    tg = _bmm(tinv, jnp.concatenate([_pair_blockdiag(ah), _pair_blockdiag(akv)], axis=2))
    ta, gm = tg[:, :, :LANES], tg[:, :, LANES:]
    rg = _bmm(a_rb, jnp.concatenate([_pair_blockdiag(ta), _pair_blockdiag(gm)], axis=2))
    rt = rh + rg[:, :, :LANES]
    y_intra = rg[:, :, LANES:] + y_rk
    tgv_t = jnp.swapaxes(jnp.concatenate([ta, gm, vp], axis=2), 1, 2)
    pp = _bmm(tgv_t[:, :2 * LANES], bt.astype(BF16))
    phi = _pair_pick(pp[:, :LANES])
    psi = _pair_pick(pp[:, LANES:]) + _pair_pick(_bmm(tgv_t[:, 2 * LANES:], kt.astype(BF16)))
    w_end_p = _to_pairs(w_end, nchunk)[:, :1]

    y_chunks = []
    s = s_scr[...]
    for c in range(nchunk):
        cs = slice(c * npair, (c + 1) * npair)
        y_c = _bmm_nt(rt[cs], _pair_blockdiag(s)) + y_intra[cs]
        s = s * w_end_p[cs] + _bmm(s, _pair_blockdiag(phi[cs])) + psi[cs]
        y_chunks.append(jnp.concatenate([y_c[p] for p in range(npair)], axis=-1))
    s_scr[...] = s
    y = jnp.concatenate(y_chunks, axis=0)

    inv_n = 1.0 / RWKV_HEAD
    mu = _pair_segsum(y, ones_bd) * inv_n
    d = y - mu
    var = _pair_segsum(d * d, ones_bd) * inv_n
    yn = d * lax.rsqrt(var + RWKV_GN_EPS) * ln_w + ln_b
    o_ref[...] = ((yn + bonus * v) * g).astype(o_ref.dtype)


def _rwkv(zin, mu_rkv, mu_lora, wup_bf16, vecs, *, nbatch, nblk):
    T = zin.shape[0]
    tb = BLOCK
    lora_blk = (AR_COLS - LORA_PAD) // LORA_PAD
    row = lambda b, n: b * nblk + n
    return pl.pallas_call(
        functools.partial(_rwkv_kernel, tb=tb),
        grid=(nbatch, nblk),
        in_specs=[pl.BlockSpec((tb, 3 * RWKV_DIM), lambda b, n: (row(b, n), 0)),
                  pl.BlockSpec((tb, LORA_PAD), lambda b, n: (row(b, n), lora_blk)),
                  pl.BlockSpec((1, 3 * RWKV_DIM), lambda b, n: (0, 0)),
                  pl.BlockSpec((1, LORA_PAD), lambda b, n: (0, 0)),
                  pl.BlockSpec((LORA_PAD, 3 * RWKV_DIM), lambda b, n: (0, 0)),
                  pl.BlockSpec((SUBLANES, RWKV_DIM), lambda b, n: (0, 0))],
        out_specs=pl.BlockSpec((tb, RWKV_DIM), lambda b, n: (row(b, n), 0)),
        out_shape=jax.ShapeDtypeStruct((T, RWKV_DIM), BF16),
        scratch_shapes=[pltpu.VMEM((RWKV_DIM // LANES, RWKV_HEAD, LANES), F32),
                        pltpu.VMEM((SUBLANES, 3 * RWKV_DIM), F32),
                        pltpu.VMEM((SUBLANES, LORA_PAD), F32)],
        compiler_params=_params("arbitrary", "arbitrary"),
        name="rwkv7",
    )(zin, zin, mu_rkv, mu_lora, wup_bf16, vecs)


def _conv4_silu(xc, carry, w, b, valid):
    full = jnp.concatenate([carry[...], xc], axis=0)
    acc = b
    for j in range(SSD_CONV):
        k = SSD_CONV - 1 - j
        term = xc if k == 0 else pltpu.roll(full, k, axis=0)[SUBLANES:]
        acc = acc + w[j:j + 1] * term
    carry[...] = xc[BLOCK - SUBLANES:]
    return jnp.where(valid, acc * _sigmoid(acc), 0.0)


def _ssd_kernel(z_ref, x_ref, b_ref, c_ref, dt_ref, cwx_ref, cwb_ref, cwc_ref, cbx_ref, cbb_ref, cbc_ref,
                dtb_ref, alog_ref, dskip_ref, nw_ref, exp_ref, o_ref,
                st_scr, carry_x, carry_b, carry_c):
    n = pl.program_id(1)

    @pl.when(n == 0)
    def _():
        st_scr[...] = jnp.zeros(st_scr.shape, F32)
        carry_x[...] = jnp.zeros(carry_x.shape, F32)
        carry_b[...] = jnp.zeros(carry_b.shape, F32)
        carry_c[...] = jnp.zeros(carry_c.shape, F32)

    rows = lax.broadcasted_iota(jnp.int32, (BLOCK, 1), 0)
    valid = (n > 0) | (rows >= N_PAD)
    x = _conv4_silu(x_ref[...], carry_x, cwx_ref[...], cbx_ref[...], valid)
    bm = _conv4_silu(b_ref[...], carry_b, cwb_ref[...], cbb_ref[...], valid)
    cm = _conv4_silu(c_ref[...], carry_c, cwc_ref[...], cbc_ref[...], valid)

    dt = jnp.where(valid, _softplus(dt_ref[...] + dtb_ref[...]), 0.0)
    adt = dt * (-jnp.exp(alog_ref[...]))
    ti = lax.broadcasted_iota(jnp.int32, (BLOCK, BLOCK), 0)
    si = lax.broadcasted_iota(jnp.int32, (BLOCK, BLOCK), 1)
    causal = si <= ti
    cum = _dot_exact_rhs(jnp.where(causal, 1.0, 0.0).astype(BF16), adt)
    cum_t = cum.T
    ecum = jnp.exp(cum)
    toend = jnp.exp(cum[BLOCK - 1:BLOCK] - cum)
    ex = _dot_exact_lhs(jnp.concatenate([dt, ecum, toend], axis=0), exp_ref[...])
    dt_x, ecum_x, toend_x = ex[:BLOCK], ex[BLOCK:2 * BLOCK], ex[2 * BLOCK:]
    xdt = x * dt_x
    xend = xdt * toend_x
    lane = lax.broadcasted_iota(jnp.int32, (BLOCK, LANES), 1)
    lo_half = lane < SSD_HEAD_DIM

    y_groups = []
    for g in range(SSD_GROUPS):
        gs = slice(g * SSD_GROUP_COLS, (g + 1) * SSD_GROUP_COLS)
        bg = bm[:, g * SSD_STATE:(g + 1) * SSD_STATE]
        cg = cm[:, g * SSD_STATE:(g + 1) * SSD_STATE]
        cb = _dot_nt(cg, bg)
        st = st_scr[g]
        y_inter = _dot(cg, st) * ecum_x[:, gs]
        y_pairs = []
        for q in range(SSD_HPG // 2):
            mats = []
            for j in (g * SSD_HPG + 2 * q, g * SSD_HPG + 2 * q + 1):
                seg = cum[:, j:j + 1] - cum_t[j:j + 1, :]
                mats.append(cb * jnp.exp(jnp.where(causal, seg, -jnp.inf)))
            ll = jnp.concatenate(mats, axis=1).astype(BF16)
            c0 = g * SSD_GROUP_COLS + q * LANES
            xp = xdt[:, c0:c0 + LANES]
            bdx = jnp.concatenate([jnp.where(lo_half, xp, 0.0), jnp.where(lo_half, 0.0, xp)], axis=0)
            y_pairs.append(jnp.dot(ll, bdx.astype(BF16), preferred_element_type=F32))
        y_intra = jnp.concatenate(y_pairs, axis=-1)
        st_scr[g] = st * ecum_x[BLOCK - 1:BLOCK, gs] + _dot_tn(bg, xend[:, gs])
        y_groups.append(y_intra + y_inter + dskip_ref[:, gs] * x[:, gs])
    z = z_ref[...]
    nw = nw_ref[...]
    outs = []
    for g in range(SSD_GROUPS):
        gs = slice(g * SSD_GROUP_COLS, (g + 1) * SSD_GROUP_COLS)
        zg = z[:, gs]
        yg = y_groups[g] * (zg * _sigmoid(zg))
        yg = yg * lax.rsqrt(jnp.mean(yg * yg, axis=-1, keepdims=True) + SSD_NORM_EPS)
        outs.append((yg * nw[:, gs]).astype(o_ref.dtype))
    o_ref[...] = jnp.concatenate(outs, axis=-1)


def _ssd(zx, conv_w, conv_b, dt_bias, a_log, d_skip, norm_w, *, nbatch, nblk):
    T = zx.shape[0]
    row = lambda b, n: b * nblk + n
    pad_h = lambda t: jnp.pad(t.astype(F32), (0, SSD_DT_PAD - SSD_HEADS)).reshape(1, SSD_DT_PAD)
    head = lax.broadcasted_iota(jnp.int32, (SSD_DT_PAD, SSD_INNER), 0)
    colh = lax.broadcasted_iota(jnp.int32, (SSD_DT_PAD, SSD_INNER), 1) // SSD_HEAD_DIM
    expand = (head == colh).astype(BF16)
    cb2 = conv_b.reshape(1, -1)
    xs, bs, cs = slice(0, SSD_INNER), slice(SSD_INNER, SSD_INNER + SSD_BC_COLS), slice(SSD_INNER + SSD_BC_COLS, None)
    full = lambda shape: pl.BlockSpec(shape, lambda b, n: (0, 0))
    x_blk = 1
    b_blk = 2 * SSD_INNER // SSD_BC_COLS
    dt_blk = (2 * SSD_INNER + 2 * SSD_BC_COLS) // SSD_DT_PAD
    return pl.pallas_call(
        _ssd_kernel,
        grid=(nbatch, nblk),
        in_specs=[pl.BlockSpec((BLOCK, SSD_INNER), lambda b, n: (row(b, n), 0)),
                  pl.BlockSpec((BLOCK, SSD_INNER), lambda b, n: (row(b, n), x_blk)),
                  pl.BlockSpec((BLOCK, SSD_BC_COLS), lambda b, n: (row(b, n), b_blk)),
                  pl.BlockSpec((BLOCK, SSD_BC_COLS), lambda b, n: (row(b, n), b_blk + 1)),
                  pl.BlockSpec((BLOCK, SSD_DT_PAD), lambda b, n: (row(b, n), dt_blk)),
                  full((SSD_CONV, SSD_INNER)), full((SSD_CONV, SSD_BC_COLS)), full((SSD_CONV, SSD_BC_COLS)),
                  full((1, SSD_INNER)), full((1, SSD_BC_COLS)), full((1, SSD_BC_COLS)),
                  full((1, SSD_DT_PAD)), full((1, SSD_DT_PAD)), full((1, SSD_INNER)), full((1, SSD_INNER)),
                  full((SSD_DT_PAD, SSD_INNER))],
        out_specs=pl.BlockSpec((BLOCK, SSD_INNER), lambda b, n: (row(b, n), 0)),
        out_shape=jax.ShapeDtypeStruct((T, SSD_INNER), BF16),
        scratch_shapes=[pltpu.VMEM((SSD_GROUPS, SSD_STATE, SSD_GROUP_COLS), F32),
                        pltpu.VMEM((SUBLANES, SSD_INNER), F32),
                        pltpu.VMEM((SUBLANES, SSD_BC_COLS), F32),
                        pltpu.VMEM((SUBLANES, SSD_BC_COLS), F32)],
        compiler_params=_params("arbitrary", "arbitrary"),
        name="ssd",
    )(zx, zx, zx, zx, zx, conv_w[:, xs], conv_w[:, bs], conv_w[:, cs], cb2[:, xs], cb2[:, bs], cb2[:, cs],
      pad_h(dt_bias), pad_h(a_log), jnp.repeat(d_skip.astype(F32), SSD_HEAD_DIM).reshape(1, SSD_INNER),
      norm_w.reshape(1, SSD_INNER), expand)


def _pack_ar_w_in(w):
    pad = jnp.zeros((w.shape[0], LORA_PAD - LORA_COLS), w.dtype)
    lora0 = ATTN_COLS + 3 * RWKV_DIM
    return jnp.concatenate([w[:, ATTN_COLS:lora0], w[:, :ATTN_COLS], w[:, lora0:], pad], axis=1).astype(BF16)


def _pack_lora_up(w_up, a_up, g_up):
    wup = jnp.zeros((LORA_PAD, 3 * RWKV_DIM), F32)
    wup = wup.at[:DECAY_LORA, :RWKV_DIM].set(w_up)
    wup = wup.at[DECAY_LORA:DECAY_LORA + AAA_LORA, RWKV_DIM:2 * RWKV_DIM].set(a_up)
    wup = wup.at[DECAY_LORA + AAA_LORA:LORA_COLS, 2 * RWKV_DIM:].set(g_up)
    return wup.astype(BF16)


def kernel(x, meta_tokens, mix_norm_w, ffn_norm_w, ar_w_in, ar_shift_mu, attn_q_norm_w, attn_k_norm_w, attn_sinks, rwkv_w0, rwkv_w_up, rwkv_a0, rwkv_a_up, rwkv_g_up, rwkv_k_k, rwkv_k_a, rwkv_r_k, rwkv_ln_w, rwkv_ln_b, ar_w_out, ssd_w_in, ssd_conv_w, ssd_conv_b, ssd_dt_bias, ssd_a_log, ssd_d, ssd_norm_w, ssd_w_out, ffn_w_up, ffn_conv_w, ffn_conv_b, ffn_w_down):
    nbatch, seq, d = x.shape
    assert d == D_MODEL and seq % BLOCK == 0
    seq_p = N_PAD + N_META + seq
    nblk = seq_p // BLOCK
    depth = mix_norm_w.shape[0]
    res = jnp.concatenate([
        jnp.zeros((nbatch, N_PAD, d), x.dtype),
        jnp.broadcast_to(meta_tokens.astype(x.dtype)[None], (nbatch, N_META, d)),
        x], axis=1).reshape(nbatch * seq_p, d)
    kw = dict(seq_p=seq_p, nbatch=nbatch)
    for layer in range(depth):
        i = layer // 2
        if layer % 2 == 0:
            zin = _norm_matmul(res, mix_norm_w[layer], _pack_ar_w_in(ar_w_in[i]), tn=1024, **kw)
            attn = _attention(zin, attn_sinks[i].astype(F32), attn_q_norm_w[i], attn_k_norm_w[i],
                              nbatch=nbatch, nblk=nblk)
            mu = ar_shift_mu[i]
            mu_rkv = mu[:3 * RWKV_DIM].reshape(1, -1)
            mu_lora = jnp.pad(mu[3 * RWKV_DIM:], (0, LORA_PAD - LORA_COLS)).reshape(1, -1)
            vecs = jnp.stack([rwkv_w0[i], rwkv_a0[i], rwkv_k_k[i], rwkv_k_a[i], rwkv_r_k[i].reshape(-1),
                              rwkv_ln_w[i], rwkv_ln_b[i], jnp.zeros((RWKV_DIM,), F32)]).astype(F32)
            tm = _rwkv(zin, mu_rkv, mu_lora, _pack_lora_up(rwkv_w_up[i], rwkv_a_up[i], rwkv_g_up[i]), vecs,
                       nbatch=nbatch, nblk=nblk)
            res = _proj_residual([attn, tm], ar_w_out[i].astype(BF16), res)
        else:
            w_in = jnp.pad(ssd_w_in[i], ((0, 0), (0, SSD_DT_PAD - SSD_HEADS))).astype(BF16)
            zx = _norm_matmul(res, mix_norm_w[layer], w_in, tn=1152, **kw)
            y = _ssd(zx, ssd_conv_w[i], ssd_conv_b[i], ssd_dt_bias[i], ssd_a_log[i], ssd_d[i], ssd_norm_w[i],
                     nbatch=nbatch, nblk=nblk)
            res = _proj_residual([y], ssd_w_out[i].astype(BF16), res)
        res = _ffn(res, ffn_norm_w[layer], ffn_w_up[layer].astype(BF16), ffn_conv_w[layer], ffn_conv_b[layer],
                   ffn_w_down[layer].astype(BF16), **kw)
    return res.reshape(nbatch, seq_p, d)[:, N_PAD + N_META:]
```

```python
import functools

import jax
import jax.numpy as jnp
from jax import lax
from jax.experimental import pallas as pl
from jax.experimental.pallas import tpu as pltpu

F32 = jnp.float32
BF16 = jnp.bfloat16

D_MODEL = 2048
N_META = 16
BLOCK = 128
N_PAD = BLOCK - N_META
NORM_EPS = 1e-6

HEAD_DIM = 64
ATTN_HEADS = 16
ATTN_KV_HEADS = 4
ATTN_GROUP = 4
WINDOW = 128
QK_EPS = 1e-6
Q_COLS = ATTN_HEADS * HEAD_DIM
KV_COLS = ATTN_KV_HEADS * HEAD_DIM
ATTN_COLS = Q_COLS + 2 * KV_COLS

RWKV_DIM = 1024
RWKV_HEAD = 64
DECAY_LORA = 64
AAA_LORA = 64
GATE_LORA = 160
LORA_COLS = DECAY_LORA + AAA_LORA + GATE_LORA
LORA_PAD = 512
RWKV_GN_EPS = 64e-5
RWKV_CHUNK = 64
AR_COLS = 3 * RWKV_DIM + Q_COLS + 2 * KV_COLS + LORA_PAD

SSD_INNER = 4096
SSD_HEAD_DIM = 64
SSD_HEADS = 64
SSD_GROUPS = 8
SSD_HPG = 8
SSD_STATE = 128
SSD_CONV = 4
SSD_GROUP_COLS = SSD_INNER // SSD_GROUPS
SSD_BC_COLS = SSD_GROUPS * SSD_STATE
SSD_DT_PAD = 128
SSD_COLS = 2 * SSD_INNER + 2 * SSD_BC_COLS + SSD_DT_PAD
SSD_NORM_EPS = 1e-5

FFN_DIM = 5632
FFN_CONV = 3
FFN_CHUNK = 512

LANES = 128
SUBLANES = 8
VMEM_LIMIT = 56 * 1024 * 1024


def _params(*sem):
    return pltpu.CompilerParams(dimension_semantics=sem, vmem_limit_bytes=VMEM_LIMIT)


def _row_tile(total, target):
    best = None
    for t in range(16, min(total, target) + 1, 16):
        if total % t == 0:
            best = t
    assert best is not None, (total, target)
    return best


def _dot(a, b):
    return jnp.dot(a.astype(BF16), b.astype(BF16), preferred_element_type=F32)


def _dot_nt(a, b):
    return lax.dot_general(a.astype(BF16), b.astype(BF16), (((1,), (1,)), ((), ())),
                           preferred_element_type=F32)


def _dot_tn(a, b):
    return lax.dot_general(a.astype(BF16), b.astype(BF16), (((0,), (0,)), ((), ())),
                           preferred_element_type=F32)


def _split3(x):
    hi = x.astype(BF16)
    r1 = x - hi.astype(F32)
    mid = r1.astype(BF16)
    lo = (r1 - mid.astype(F32)).astype(BF16)
    return hi, mid, lo


def _dot_exact_rhs(sel_bf16, x):
    hi, mid, lo = _split3(x)
    d = lambda p: jnp.dot(sel_bf16, p, preferred_element_type=F32)
    return d(hi) + d(mid) + d(lo)


def _dot_exact_lhs(x, sel_bf16):
    hi, mid, lo = _split3(x)
    d = lambda p: jnp.dot(p, sel_bf16, preferred_element_type=F32)
    return d(hi) + d(mid) + d(lo)


def _half_sums(x):
    hi = x.astype(BF16)
    lo = (x - hi.astype(F32)).astype(BF16)
    i = lax.broadcasted_iota(jnp.int32, (2 * LANES, LANES), 0) & (LANES - 1)
    j = lax.broadcasted_iota(jnp.int32, (2 * LANES, LANES), 1)
    ones2 = jnp.where((i < LANES // 2) == (j < LANES // 2), 1.0, 0.0).astype(BF16)
    return jnp.dot(jnp.concatenate([hi, lo], axis=1), ones2, preferred_element_type=F32)


def _sigmoid(x):
    return 1.0 / (1.0 + jnp.exp(-x))


def _softplus(x):
    return jnp.maximum(x, 0.0) + jnp.log(1.0 + jnp.exp(-jnp.abs(x)))


def _valid_rows(row0, nrows, seq_p, nbatch):
    rows = row0 + lax.broadcasted_iota(jnp.int32, (nrows, 1), 0)
    pos = rows
    for b in range(1, nbatch):
        pos = jnp.where(rows >= b * seq_p, rows - b * seq_p, pos)
    return pos >= N_PAD


def _masked_rmsnorm(x, w, row0, seq_p, nbatch):
    ms = jnp.mean(x * x, axis=-1, keepdims=True)
    y = x * lax.rsqrt(ms + NORM_EPS) * w
    return jnp.where(_valid_rows(row0, x.shape[0], seq_p, nbatch), y, 0.0)


def _norm_mm_kernel(x_ref, nw_ref, w_ref, o_ref, h_scr, *, tm, seq_p, nbatch):
    @pl.when(pl.program_id(1) == 0)
    def _():
        h = _masked_rmsnorm(x_ref[...], nw_ref[...], pl.program_id(0) * tm, seq_p, nbatch)
        h_scr[...] = h.astype(BF16)

    o_ref[...] = jnp.dot(h_scr[...], w_ref[...], preferred_element_type=F32)


def _norm_matmul(res, norm_w, w_bf16, *, tn, seq_p, nbatch):
    T, D = res.shape
    N = w_bf16.shape[1]
    tm = _row_tile(T, 768)
    assert N % tn == 0
    return pl.pallas_call(
        functools.partial(_norm_mm_kernel, tm=tm, seq_p=seq_p, nbatch=nbatch),
        grid=(T // tm, N // tn),
        in_specs=[pl.BlockSpec((tm, D), lambda m, n: (m, 0)),
                  pl.BlockSpec((1, D), lambda m, n: (0, 0)),
                  pl.BlockSpec((D, tn), lambda m, n: (0, n))],
        out_specs=pl.BlockSpec((tm, tn), lambda m, n: (m, n)),
        out_shape=jax.ShapeDtypeStruct((T, N), F32),
        scratch_shapes=[pltpu.VMEM((tm, D), BF16)],
        compiler_params=_params("arbitrary", "arbitrary"),
        name="norm_in_proj",
    )(res, norm_w.reshape(1, D), w_bf16)


def _proj_res_kernel(*refs, n_lhs):
    lhs = refs[:n_lhs]
    ws = refs[n_lhs:2 * n_lhs]
    res_ref, o_ref = refs[2 * n_lhs], refs[2 * n_lhs + 1]
    acc = res_ref[...]
    for l, w in zip(lhs, ws):
        acc = acc + jnp.dot(l[...], w[...], preferred_element_type=F32)
    o_ref[...] = acc


def _proj_residual(lhs_list, w_bf16, res, *, tn=512):
    T, D = res.shape
    n_lhs = len(lhs_list)
    kw = lhs_list[0].shape[1]
    assert all(l.shape == (T, kw) for l in lhs_list) and w_bf16.shape == (n_lhs * kw, D)
    tm = _row_tile(T, 768)
    in_specs = [pl.BlockSpec((tm, kw), lambda m, n: (m, 0)) for _ in range(n_lhs)]
    in_specs += [pl.BlockSpec((kw, tn), functools.partial(lambda i, m, n: (i, n), i)) for i in range(n_lhs)]
    in_specs += [pl.BlockSpec((tm, tn), lambda m, n: (m, n))]
    return pl.pallas_call(
        functools.partial(_proj_res_kernel, n_lhs=n_lhs),
        grid=(T // tm, D // tn),
        in_specs=in_specs,
        out_specs=pl.BlockSpec((tm, tn), lambda m, n: (m, n)),
        out_shape=jax.ShapeDtypeStruct((T, D), F32),
        compiler_params=_params("arbitrary", "arbitrary"),
        name="out_proj_residual",
    )(*lhs_list, *([w_bf16] * n_lhs), res)


def _ffn_kernel(res_ref, nw_ref, wg_ref, wv_ref, cw_ref, cb_ref, wd_ref, o_ref, h_scr, carry_scr,
                *, tm, seq_p, nbatch):
    m = pl.program_id(0)
    f = pl.program_id(1)

    @pl.when(f == 0)
    def _():
        res = res_ref[...]
        h_scr[...] = _masked_rmsnorm(res, nw_ref[...], m * tm, seq_p, nbatch).astype(BF16)
        o_ref[...] = res

    @pl.when(m == 0)
    def _():
        carry_scr[f] = jnp.zeros(carry_scr.shape[1:], F32)

    h = h_scr[...]
    gate = jnp.dot(h, wg_ref[...], preferred_element_type=F32)
    val = jnp.dot(h, wv_ref[...], preferred_element_type=F32)
    prev = carry_scr[f]
    rows = lax.broadcasted_iota(jnp.int32, (tm, 1), 0)
    g1 = jnp.where(rows == 0, prev[7:8], pltpu.roll(gate, 1, axis=0))
    g2 = jnp.where(rows == 0, prev[6:7], jnp.where(rows == 1, prev[7:8], pltpu.roll(gate, 2, axis=0)))
    carry_scr[f] = gate[tm - SUBLANES:]
    cw = cw_ref[...]
    pre = cb_ref[...] + cw[0:1] * g2 + cw[1:2] * g1 + cw[2:3] * gate
    act = pre * _sigmoid(pre) * val
    o_ref[...] += jnp.dot(act.astype(BF16), wd_ref[...], preferred_element_type=F32)


def _ffn(res, norm_w, w_up_bf16, conv_w, conv_b, w_down_bf16, *, seq_p, nbatch):
    T, D = res.shape
    tm = _row_tile(T, 768)
    tf = FFN_CHUNK
    nf = FFN_DIM // tf
    return pl.pallas_call(
        functools.partial(_ffn_kernel, tm=tm, seq_p=seq_p, nbatch=nbatch),
        grid=(T // tm, nf),
        in_specs=[pl.BlockSpec((tm, D), lambda m, f: (m, 0)),
                  pl.BlockSpec((1, D), lambda m, f: (0, 0)),
                  pl.BlockSpec((D, tf), lambda m, f: (0, f)),
                  pl.BlockSpec((D, tf), lambda m, f: (0, nf + f)),
                  pl.BlockSpec((FFN_CONV, tf), lambda m, f: (0, f)),
                  pl.BlockSpec((1, tf), lambda m, f: (0, f)),
                  pl.BlockSpec((tf, D), lambda m, f: (f, 0))],
        out_specs=pl.BlockSpec((tm, D), lambda m, f: (m, 0)),
        out_shape=jax.ShapeDtypeStruct((T, D), F32),
        scratch_shapes=[pltpu.VMEM((tm, D), BF16), pltpu.VMEM((nf, SUBLANES, tf), F32)],
        compiler_params=_params("arbitrary", "arbitrary"),
        name="ffn",
    )(res, norm_w.reshape(1, D), w_up_bf16, w_up_bf16, conv_w, conv_b.reshape(1, FFN_DIM), w_down_bf16)


def _attn_kernel(sinks_ref, q_ref, kvc_ref, kvp_ref, kvm_ref, qw_ref, kw_ref, o_ref):
    n = pl.program_id(1)
    scale = HEAD_DIM ** -0.5
    lane = lax.broadcasted_iota(jnp.int32, (BLOCK, LANES), 1)
    lo_half = lane < HEAD_DIM

    kvs = (kvp_ref[...], kvc_ref[...], kvm_ref[...])
    q = q_ref[...]
    nq, nk = Q_COLS // LANES, KV_COLS // LANES
    tiles = [q[:, c * LANES:(c + 1) * LANES] for c in range(nq)]
    tiles += [kv[:, c * LANES:(c + 1) * LANES] for kv in kvs for c in range(nk)]
    sq = jnp.concatenate([t * t for t in tiles], axis=0)
    ms = _half_sums(sq) * (1.0 / HEAD_DIM)
    inv = lax.rsqrt(ms + QK_EPS)
    qw = qw_ref[...] * scale
    kw = kw_ref[...]
    qn = [tiles[c] * inv[c * BLOCK:(c + 1) * BLOCK] * qw for c in range(nq)]
    kn = [[tiles[nq + i * nk + c] * inv[(nq + i * nk + c) * BLOCK:(nq + i * nk + c + 1) * BLOCK] * kw
           for i in range(3)] for c in range(nk)]

    qi = lax.broadcasted_iota(jnp.int32, (BLOCK, 3 * BLOCK), 0)
    kj = lax.broadcasted_iota(jnp.int32, (BLOCK, 3 * BLOCK), 1)
    j_prev, j_cur, j_meta = kj, kj - BLOCK, kj - 2 * BLOCK
    first_prev = jnp.where(n == 0, BLOCK, jnp.where(n == 1, N_PAD, 0))
    first_cur = jnp.where(n == 0, N_PAD, 0)
    vis_prev = (kj < BLOCK) & (j_prev > qi) & (j_prev >= first_prev)
    vis_cur = (kj >= BLOCK) & (j_cur <= qi) & (j_cur >= first_cur)
    vis_meta = (j_meta >= N_PAD) & (n * BLOCK + qi - j_meta >= WINDOW)
    bias = jnp.where(vis_prev | vis_cur | vis_meta, 0.0, -jnp.inf)
    bias4 = jnp.concatenate([bias] * ATTN_GROUP, axis=0)
    grow = lax.broadcasted_iota(jnp.int32, (ATTN_GROUP * BLOCK, 1), 0)

    def dup_half(x, upper):
        swapped = pltpu.roll(x, HEAD_DIM, axis=1)
        lane_r = lax.broadcasted_iota(jnp.int32, x.shape, 1) < HEAD_DIM
        return jnp.where(lane_r, swapped, x) if upper else jnp.where(lane_r, x, swapped)

    outs = []
    for c in range(nk):
        k_col = jnp.concatenate(kn[c], axis=0)
        v_col = jnp.concatenate([kv[:, KV_COLS + c * LANES:KV_COLS + (c + 1) * LANES] for kv in kvs], axis=0)
        for half in range(2):
            h = 2 * c + half
            k_dup = dup_half(k_col, half == 1).astype(BF16)
            v_dup = dup_half(v_col, half == 1).astype(BF16)
            lhs = jnp.concatenate(
                [jnp.where(lo_half if e == 0 else jnp.logical_not(lo_half), qn[2 * h + pr], 0.0)
                 for pr in range(2) for e in range(2)], axis=0)
            s = _dot_nt(lhs, k_dup) + bias4
            sink = sinks_ref[ATTN_GROUP * h]
            for g in range(1, ATTN_GROUP):
                sink = jnp.where(grow >= g * BLOCK, sinks_ref[ATTN_GROUP * h + g], sink)
            mx = jnp.maximum(jnp.max(s, axis=-1, keepdims=True), sink)
            p = jnp.exp(s - mx)
            den = jnp.sum(p, axis=-1, keepdims=True) + jnp.exp(sink - mx)
            o = jnp.dot(p.astype(BF16), v_dup, preferred_element_type=F32) / den
            for pr in range(2):
                r0 = 2 * pr * BLOCK
                outs.append(jnp.where(lo_half, o[r0:r0 + BLOCK], o[r0 + BLOCK:r0 + 2 * BLOCK]))
    o_ref[...] = jnp.concatenate(outs, axis=-1).astype(o_ref.dtype)


def _attention(zin, sinks, q_norm_w, k_norm_w, *, nbatch, nblk):
    T = zin.shape[0]
    q_blk = 3 * RWKV_DIM // Q_COLS
    kv_blk = (3 * RWKV_DIM + Q_COLS) // (2 * KV_COLS)
    row = lambda b, n: b * nblk + n
    return pl.pallas_call(
        _attn_kernel,
        grid=(nbatch, nblk),
        in_specs=[pl.BlockSpec(memory_space=pltpu.SMEM),
                  pl.BlockSpec((BLOCK, Q_COLS), lambda b, n: (row(b, n), q_blk)),
                  pl.BlockSpec((BLOCK, 2 * KV_COLS), lambda b, n: (row(b, n), kv_blk)),
                  pl.BlockSpec((BLOCK, 2 * KV_COLS), lambda b, n: (row(b, jnp.maximum(n - 1, 0)), kv_blk)),
                  pl.BlockSpec((BLOCK, 2 * KV_COLS), lambda b, n: (row(b, 0), kv_blk)),
                  pl.BlockSpec((1, LANES), lambda b, n: (0, 0)),
                  pl.BlockSpec((1, LANES), lambda b, n: (0, 0))],
        out_specs=pl.BlockSpec((BLOCK, Q_COLS), lambda b, n: (row(b, n), 0)),
        out_shape=jax.ShapeDtypeStruct((T, Q_COLS), BF16),
        compiler_params=_params("arbitrary", "arbitrary"),
        name="swa_attention",
    )(sinks, zin, zin, zin, zin, jnp.tile(q_norm_w, LANES // HEAD_DIM).reshape(1, LANES),
      jnp.tile(k_norm_w, LANES // HEAD_DIM).reshape(1, LANES))


def _pair_blockdiag(x):
    xb = x.astype(BF16)
    lane = lax.broadcasted_iota(jnp.int32, xb.shape, 2)
    zero = jnp.zeros_like(xb)
    return jnp.concatenate([jnp.where(lane < RWKV_HEAD, xb, zero), jnp.where(lane >= RWKV_HEAD, xb, zero)], axis=1)


def _bmm(a, b_bf16):
    return lax.dot_general(a.astype(BF16), b_bf16, (((2,), (1,)), ((0,), (0,))), preferred_element_type=F32)


def _bmm_nt(a, b_bf16):
    return lax.dot_general(a.astype(BF16), b_bf16, (((2,), (2,)), ((0,), (0,))), preferred_element_type=F32)


def _pair_pick(full):
    lane = lax.broadcasted_iota(jnp.int32, (full.shape[0], RWKV_HEAD, LANES), 2)
    return jnp.where(lane < RWKV_HEAD, full[:, :RWKV_HEAD], full[:, RWKV_HEAD:])


def _to_pairs(x, nchunk):
    L = RWKV_CHUNK
    return jnp.stack([x[c * L:(c + 1) * L, p * LANES:(p + 1) * LANES]
                      for c in range(nchunk) for p in range(x.shape[1] // LANES)], axis=0)


def _pair_segsum(x, ones_bd):
    npair = x.shape[1] // LANES
    rows = x.shape[0]
    stacked = jnp.concatenate([x[:, p * LANES:(p + 1) * LANES] for p in range(npair)], axis=0)
    s = _dot_exact_lhs(stacked, ones_bd)
    return jnp.concatenate([s[p * rows:(p + 1) * rows] for p in range(npair)], axis=-1)


def _rwkv_kernel(rkv_ref, lora_ref, mu_rkv_ref, mu_lora_ref, wup_ref, vec_ref, o_ref,
                 s_scr, carry_rkv, carry_lora, *, tb):
    n = pl.program_id(1)
    L = RWKV_CHUNK
    npair = RWKV_DIM // LANES

    @pl.when(n == 0)
    def _():
        s_scr[...] = jnp.zeros(s_scr.shape, F32)
        carry_rkv[...] = jnp.zeros(carry_rkv.shape, F32)
        carry_lora[...] = jnp.zeros(carry_lora.shape, F32)

    rows = lax.broadcasted_iota(jnp.int32, (tb, 1), 0)
    valid = (n > 0) | (rows >= N_PAD)

    def token_shift(z_ref, carry, mu_ref):
        z = z_ref[...]
        prev = jnp.where(rows == 0, carry[SUBLANES - 1:SUBLANES], pltpu.roll(z, 1, axis=0))
        carry[...] = z[tb - SUBLANES:]
        return jnp.where(valid, z + (prev - z) * mu_ref[...], 0.0)

    zr = token_shift(rkv_ref, carry_rkv, mu_rkv_ref)
    zl = token_shift(lora_ref, carry_lora, mu_lora_ref)
    r, k, v = zr[:, :RWKV_DIM], zr[:, RWKV_DIM:2 * RWKV_DIM], zr[:, 2 * RWKV_DIM:]

    col = lax.broadcasted_iota(jnp.int32, zl.shape, 1)
    lora_act = jnp.where(col < DECAY_LORA, jnp.tanh(zl), jnp.where(col < DECAY_LORA + AAA_LORA, zl, _sigmoid(zl)))
    up = jnp.dot(lora_act.astype(BF16), wup_ref[...], preferred_element_type=F32)
    vec = vec_ref[...]
    w0, a0, k_k, k_a, r_k, ln_w, ln_b = (vec[i:i + 1] for i in range(7))
    w_log = -jnp.exp(-_softplus(-(w0 + up[:, :RWKV_DIM])) - 0.5)
    w_log = jnp.where(valid, w_log, 0.0)
    a = _sigmoid(a0 + up[:, RWKV_DIM:2 * RWKV_DIM])
    g = up[:, 2 * RWKV_DIM:]

    li = lax.broadcasted_iota(jnp.int32, (LANES, LANES), 0)
    lj = lax.broadcasted_iota(jnp.int32, (LANES, LANES), 1)
    ones_bd = jnp.where((li < RWKV_HEAD) == (lj < RWKV_HEAD), 1.0, 0.0).astype(BF16)

    kk = k * k_k
    kk = kk / jnp.maximum(jnp.sqrt(_pair_segsum(kk * kk, ones_bd)), 1e-12)
    k2 = k * (1.0 + (a - 1.0) * k_a)
    bonus = _pair_segsum(r * k2 * r_k, ones_bd)
    kka = kk * a

    nchunk = tb // L
    nb = nchunk * npair
    t_i = lax.broadcasted_iota(jnp.int32, (nb, L, LANES), 1)
    s_i = lax.broadcasted_iota(jnp.int32, (nb, L, LANES), 2) & (RWKV_HEAD - 1)
    strict = s_i < t_i
    incl = s_i <= t_i
    same16 = (t_i >> 4) == (s_i >> 4)
    same32 = (t_i >> 5) == (s_i >> 5)
    eye = jnp.where(s_i == t_i, 1.0, 0.0)
    ti2 = lax.broadcasted_iota(jnp.int32, (tb, tb), 0)
    si2 = lax.broadcasted_iota(jnp.int32, (tb, tb), 1)
    lshift = L.bit_length() - 1
    tri_incl = jnp.where((si2 <= ti2) & ((si2 >> lshift) == (ti2 >> lshift)), 1.0, 0.0).astype(BF16)
    cum = _dot_exact_rhs(tri_incl, w_log)
    cum_end = jnp.concatenate(
        [jnp.broadcast_to(cum[(c + 1) * L - 1:(c + 1) * L], (L, RWKV_DIM)) for c in range(nchunk)], axis=0)
    e_w = jnp.exp(cum)
    e_iw = jnp.exp(-cum)
    e_prev = jnp.exp(cum - w_log)
    e_end = jnp.exp(cum_end - cum)
    w_end = jnp.exp(cum_end)
    ah = _to_pairs(-kk * e_prev, nchunk)
    bh = _to_pairs(kka * e_iw, nchunk)
    kh = _to_pairs(k2 * e_iw, nchunk)
    rh = _to_pairs(r * e_w, nchunk)
    bt = _to_pairs(kka * e_end, nchunk)
    kt = _to_pairs(k2 * e_end, nchunk)
    vp = _to_pairs(v, nchunk)
    vbd = _pair_blockdiag(vp)

    ar = jnp.concatenate([ah, rh], axis=1)
    m = _bmm_nt(ar, jnp.concatenate([_pair_blockdiag(bh), _pair_blockdiag(kh)], axis=1))
    a_ab = jnp.where(strict, m[:, :L, :LANES], 0.0)
    a_rb = jnp.where(incl, m[:, L:, :LANES], 0.0)
    a_ak = jnp.where(strict, m[:, :L, LANES:], 0.0)
    a_rk = jnp.where(incl, m[:, L:, LANES:], 0.0)
    kv = _bmm(jnp.concatenate([a_ak, a_rk], axis=1), vbd)
    akv, y_rk = kv[:, :L], kv[:, L:]
    a1 = jnp.where(same16, a_ab, 0.0)
    t1 = eye + a1
    a2 = _bmm(a1, _pair_blockdiag(a1))
    x = _bmm(jnp.concatenate([a2, t1], axis=1), _pair_blockdiag(a2))
    a4, t2 = x[:, :L], t1 + x[:, L:]
    x = _bmm(jnp.concatenate([a4, t2], axis=1), _pair_blockdiag(a4))
    a8, t3 = x[:, :L], t2 + x[:, L:]
    t16 = t3 + _bmm(t3, _pair_blockdiag(a8))
    off32 = jnp.where(same32 & jnp.logical_not(same16), a_ab, 0.0)
    t32 = t16 + _bmm(t16, _pair_blockdiag(_bmm(off32, _pair_blockdiag(t16))))
    off64 = jnp.where(same32, 0.0, a_ab)
    tinv = t32 + _bmm(t32, _pair_blockdiag(_bmm(off64, _pair_blockdiag(t32))))
    tg = _bmm(tinv, jnp.concatenate([_pair_blockdiag(ah), _pair_blockdiag(akv)], axis=2))
    ta, gm = tg[:, :, :LANES], tg[:, :, LANES:]
    rg = _bmm(a_rb, jnp.concatenate([_pair_blockdiag(ta), _pair_blockdiag(gm)], axis=2))
    rt = rh + rg[:, :, :LANES]
    y_intra = rg[:, :, LANES:] + y_rk
    tgv_t = jnp.swapaxes(jnp.concatenate([ta, gm, vp], axis=2), 1, 2)
    pp = _bmm(tgv_t[:, :2 * LANES], bt.astype(BF16))
    phi = _pair_pick(pp[:, :LANES])
    psi = _pair_pick(pp[:, LANES:]) + _pair_pick(_bmm(tgv_t[:, 2 * LANES:], kt.astype(BF16)))
    w_end_p = _to_pairs(w_end, nchunk)[:, :1]

    y_chunks = []
    s = s_scr[...]
    for c in range(nchunk):
        cs = slice(c * npair, (c + 1) * npair)
        y_c = _bmm_nt(rt[cs], _pair_blockdiag(s)) + y_intra[cs]
        s = s * w_end_p[cs] + _bmm(s, _pair_blockdiag(phi[cs])) + psi[cs]
        y_chunks.append(jnp.concatenate([y_c[p] for p in range(npair)], axis=-1))
    s_scr[...] = s
    y = jnp.concatenate(y_chunks, axis=0)

    inv_n = 1.0 / RWKV_HEAD
    mu = _pair_segsum(y, ones_bd) * inv_n
    d = y - mu
    var = _pair_segsum(d * d, ones_bd) * inv_n
    yn = d * lax.rsqrt(var + RWKV_GN_EPS) * ln_w + ln_b
    o_ref[...] = ((yn + bonus * v) * g).astype(o_ref.dtype)


def _rwkv(zin, mu_rkv, mu_lora, wup_bf16, vecs, *, nbatch, nblk):
    T = zin.shape[0]
    tb = BLOCK
    lora_blk = (AR_COLS - LORA_PAD) // LORA_PAD
    row = lambda b, n: b * nblk + n
    return pl.pallas_call(
        functools.partial(_rwkv_kernel, tb=tb),
        grid=(nbatch, nblk),
        in_specs=[pl.BlockSpec((tb, 3 * RWKV_DIM), lambda b, n: (row(b, n), 0)),
                  pl.BlockSpec((tb, LORA_PAD), lambda b, n: (row(b, n), lora_blk)),
                  pl.BlockSpec((1, 3 * RWKV_DIM), lambda b, n: (0, 0)),
                  pl.BlockSpec((1, LORA_PAD), lambda b, n: (0, 0)),
                  pl.BlockSpec((LORA_PAD, 3 * RWKV_DIM), lambda b, n: (0, 0)),
                  pl.BlockSpec((SUBLANES, RWKV_DIM), lambda b, n: (0, 0))],
        out_specs=pl.BlockSpec((tb, RWKV_DIM), lambda b, n: (row(b, n), 0)),
        out_shape=jax.ShapeDtypeStruct((T, RWKV_DIM), BF16),
        scratch_shapes=[pltpu.VMEM((RWKV_DIM // LANES, RWKV_HEAD, LANES), F32),
                        pltpu.VMEM((SUBLANES, 3 * RWKV_DIM), F32),
                        pltpu.VMEM((SUBLANES, LORA_PAD), F32)],
        compiler_params=_params("arbitrary", "arbitrary"),
        name="rwkv7",
    )(zin, zin, mu_rkv, mu_lora, wup_bf16, vecs)


def _conv4_silu(xc, carry, w, b, valid):
    full = jnp.concatenate([carry[...], xc], axis=0)
    acc = b
    for j in range(SSD_CONV):
        k = SSD_CONV - 1 - j
        term = xc if k == 0 else pltpu.roll(full, k, axis=0)[SUBLANES:]
        acc = acc + w[j:j + 1] * term
    carry[...] = xc[BLOCK - SUBLANES:]
    return jnp.where(valid, acc * _sigmoid(acc), 0.0)


def _ssd_kernel(z_ref, x_ref, b_ref, c_ref, dt_ref, cwx_ref, cwb_ref, cwc_ref, cbx_ref, cbb_ref, cbc_ref,
                dtb_ref, alog_ref, dskip_ref, nw_ref, exp_ref, o_ref,
                st_scr, carry_x, carry_b, carry_c):
    n = pl.program_id(1)

    @pl.when(n == 0)
    def _():
        st_scr[...] = jnp.zeros(st_scr.shape, F32)
        carry_x[...] = jnp.zeros(carry_x.shape, F32)
        carry_b[...] = jnp.zeros(carry_b.shape, F32)
        carry_c[...] = jnp.zeros(carry_c.shape, F32)

    rows = lax.broadcasted_iota(jnp.int32, (BLOCK, 1), 0)
    valid = (n > 0) | (rows >= N_PAD)
    x = _conv4_silu(x_ref[...], carry_x, cwx_ref[...], cbx_ref[...], valid)
    bm = _conv4_silu(b_ref[...], carry_b, cwb_ref[...], cbb_ref[...], valid)
    cm = _conv4_silu(c_ref[...], carry_c, cwc_ref[...], cbc_ref[...], valid)

    dt = jnp.where(valid, _softplus(dt_ref[...] + dtb_ref[...]), 0.0)
    adt = dt * (-jnp.exp(alog_ref[...]))
    ti = lax.broadcasted_iota(jnp.int32, (BLOCK, BLOCK), 0)
    si = lax.broadcasted_iota(jnp.int32, (BLOCK, BLOCK), 1)
    causal = si <= ti
    cum = _dot_exact_rhs(jnp.where(causal, 1.0, 0.0).astype(BF16), adt)
    cum_t = cum.T
    ecum = jnp.exp(cum)
    toend = jnp.exp(cum[BLOCK - 1:BLOCK] - cum)
    ex = _dot_exact_lhs(jnp.concatenate([dt, ecum, toend], axis=0), exp_ref[...])
    dt_x, ecum_x, toend_x = ex[:BLOCK], ex[BLOCK:2 * BLOCK], ex[2 * BLOCK:]
    xdt = x * dt_x
    xend = xdt * toend_x
    lane = lax.broadcasted_iota(jnp.int32, (BLOCK, LANES), 1)
    lo_half = lane < SSD_HEAD_DIM

    y_groups = []
    for g in range(SSD_GROUPS):
        gs = slice(g * SSD_GROUP_COLS, (g + 1) * SSD_GROUP_COLS)
        bg = bm[:, g * SSD_STATE:(g + 1) * SSD_STATE]
        cg = cm[:, g * SSD_STATE:(g + 1) * SSD_STATE]
        cb = _dot_nt(cg, bg)
        st = st_scr[g]
        y_inter = _dot(cg, st) * ecum_x[:, gs]
        y_pairs = []
        for q in range(SSD_HPG // 2):
            mats = []
            for j in (g * SSD_HPG + 2 * q, g * SSD_HPG + 2 * q + 1):
                seg = cum[:, j:j + 1] - cum_t[j:j + 1, :]
                mats.append(cb * jnp.exp(jnp.where(causal, seg, -jnp.inf)))
            ll = jnp.concatenate(mats, axis=1).astype(BF16)
            c0 = g * SSD_GROUP_COLS + q * LANES
            xp = xdt[:, c0:c0 + LANES]
            bdx = jnp.concatenate([jnp.where(lo_half, xp, 0.0), jnp.where(lo_half, 0.0, xp)], axis=0)
            y_pairs.append(jnp.dot(ll, bdx.astype(BF16), preferred_element_type=F32))
        y_intra = jnp.concatenate(y_pairs, axis=-1)
        st_scr[g] = st * ecum_x[BLOCK - 1:BLOCK, gs] + _dot_tn(bg, xend[:, gs])
        y_groups.append(y_intra + y_inter + dskip_ref[:, gs] * x[:, gs])
    z = z_ref[...]
    nw = nw_ref[...]
    outs = []
    for g in range(SSD_GROUPS):
        gs = slice(g * SSD_GROUP_COLS, (g + 1) * SSD_GROUP_COLS)
        zg = z[:, gs]
        yg = y_groups[g] * (zg * _sigmoid(zg))
        yg = yg * lax.rsqrt(jnp.mean(yg * yg, axis=-1, keepdims=True) + SSD_NORM_EPS)
        outs.append((yg * nw[:, gs]).astype(o_ref.dtype))
    o_ref[...] = jnp.concatenate(outs, axis=-1)


def _ssd(zx, conv_w, conv_b, dt_bias, a_log, d_skip, norm_w, *, nbatch, nblk):
    T = zx.shape[0]
    row = lambda b, n: b * nblk + n
    pad_h = lambda t: jnp.pad(t.astype(F32), (0, SSD_DT_PAD - SSD_HEADS)).reshape(1, SSD_DT_PAD)
    head = lax.broadcasted_iota(jnp.int32, (SSD_DT_PAD, SSD_INNER), 0)
    colh = lax.broadcasted_iota(jnp.int32, (SSD_DT_PAD, SSD_INNER), 1) // SSD_HEAD_DIM
    expand = (head == colh).astype(BF16)
    cb2 = conv_b.reshape(1, -1)
    xs, bs, cs = slice(0, SSD_INNER), slice(SSD_INNER, SSD_INNER + SSD_BC_COLS), slice(SSD_INNER + SSD_BC_COLS, None)
    full = lambda shape: pl.BlockSpec(shape, lambda b, n: (0, 0))
    x_blk = 1
    b_blk = 2 * SSD_INNER // SSD_BC_COLS
    dt_blk = (2 * SSD_INNER + 2 * SSD_BC_COLS) // SSD_DT_PAD
    return pl.pallas_call(
        _ssd_kernel,
        grid=(nbatch, nblk),
        in_specs=[pl.BlockSpec((BLOCK, SSD_INNER), lambda b, n: (row(b, n), 0)),
                  pl.BlockSpec((BLOCK, SSD_INNER), lambda b, n: (row(b, n), x_blk)),
                  pl.BlockSpec((BLOCK, SSD_BC_COLS), lambda b, n: (row(b, n), b_blk)),
                  pl.BlockSpec((BLOCK, SSD_BC_COLS), lambda b, n: (row(b, n), b_blk + 1)),
                  pl.BlockSpec((BLOCK, SSD_DT_PAD), lambda b, n: (row(b, n), dt_blk)),
                  full((SSD_CONV, SSD_INNER)), full((SSD_CONV, SSD_BC_COLS)), full((SSD_CONV, SSD_BC_COLS)),
                  full((1, SSD_INNER)), full((1, SSD_BC_COLS)), full((1, SSD_BC_COLS)),
                  full((1, SSD_DT_PAD)), full((1, SSD_DT_PAD)), full((1, SSD_INNER)), full((1, SSD_INNER)),
                  full((SSD_DT_PAD, SSD_INNER))],
        out_specs=pl.BlockSpec((BLOCK, SSD_INNER), lambda b, n: (row(b, n), 0)),
        out_shape=jax.ShapeDtypeStruct((T, SSD_INNER), BF16),
        scratch_shapes=[pltpu.VMEM((SSD_GROUPS, SSD_STATE, SSD_GROUP_COLS), F32),
                        pltpu.VMEM((SUBLANES, SSD_INNER), F32),
                        pltpu.VMEM((SUBLANES, SSD_BC_COLS), F32),
                        pltpu.VMEM((SUBLANES, SSD_BC_COLS), F32)],
        compiler_params=_params("arbitrary", "arbitrary"),
        name="ssd",
    )(zx, zx, zx, zx, zx, conv_w[:, xs], conv_w[:, bs], conv_w[:, cs], cb2[:, xs], cb2[:, bs], cb2[:, cs],
      pad_h(dt_bias), pad_h(a_log), jnp.repeat(d_skip.astype(F32), SSD_HEAD_DIM).reshape(1, SSD_INNER),
      norm_w.reshape(1, SSD_INNER), expand)


def _pack_ar_w_in(w):
    pad = jnp.zeros((w.shape[0], LORA_PAD - LORA_COLS), w.dtype)
    lora0 = ATTN_COLS + 3 * RWKV_DIM
    return jnp.concatenate([w[:, ATTN_COLS:lora0], w[:, :ATTN_COLS], w[:, lora0:], pad], axis=1).astype(BF16)


def _pack_lora_up(w_up, a_up, g_up):
    wup = jnp.zeros((LORA_PAD, 3 * RWKV_DIM), F32)
    wup = wup.at[:DECAY_LORA, :RWKV_DIM].set(w_up)
    wup = wup.at[DECAY_LORA:DECAY_LORA + AAA_LORA, RWKV_DIM:2 * RWKV_DIM].set(a_up)
    wup = wup.at[DECAY_LORA + AAA_LORA:LORA_COLS, 2 * RWKV_DIM:].set(g_up)
    return wup.astype(BF16)


def kernel(x, meta_tokens, mix_norm_w, ffn_norm_w, ar_w_in, ar_shift_mu, attn_q_norm_w, attn_k_norm_w, attn_sinks, rwkv_w0, rwkv_w_up, rwkv_a0, rwkv_a_up, rwkv_g_up, rwkv_k_k, rwkv_k_a, rwkv_r_k, rwkv_ln_w, rwkv_ln_b, ar_w_out, ssd_w_in, ssd_conv_w, ssd_conv_b, ssd_dt_bias, ssd_a_log, ssd_d, ssd_norm_w, ssd_w_out, ffn_w_up, ffn_conv_w, ffn_conv_b, ffn_w_down):
    nbatch, seq, d = x.shape
    assert d == D_MODEL and seq % BLOCK == 0
    seq_p = N_PAD + N_META + seq
    nblk = seq_p // BLOCK
    depth = mix_norm_w.shape[0]
    res = jnp.concatenate([
        jnp.zeros((nbatch, N_PAD, d), x.dtype),
        jnp.broadcast_to(meta_tokens.astype(x.dtype)[None], (nbatch, N_META, d)),
        x], axis=1).reshape(nbatch * seq_p, d)
    kw = dict(seq_p=seq_p, nbatch=nbatch)
    for layer in range(depth):
        i = layer // 2
        if layer % 2 == 0:
            zin = _norm_matmul(res, mix_norm_w[layer], _pack_ar_w_in(ar_w_in[i]), tn=1024, **kw)
            attn = _attention(zin, attn_sinks[i].astype(F32), attn_q_norm_w[i], attn_k_norm_w[i],
                              nbatch=nbatch, nblk=nblk)
            mu = ar_shift_mu[i]
            mu_rkv = mu[:3 * RWKV_DIM].reshape(1, -1)
            mu_lora = jnp.pad(mu[3 * RWKV_DIM:], (0, LORA_PAD - LORA_COLS)).reshape(1, -1)
            vecs = jnp.stack([rwkv_w0[i], rwkv_a0[i], rwkv_k_k[i], rwkv_k_a[i], rwkv_r_k[i].reshape(-1),
                              rwkv_ln_w[i], rwkv_ln_b[i], jnp.zeros((RWKV_DIM,), F32)]).astype(F32)
            tm = _rwkv(zin, mu_rkv, mu_lora, _pack_lora_up(rwkv_w_up[i], rwkv_a_up[i], rwkv_g_up[i]), vecs,
                       nbatch=nbatch, nblk=nblk)
            res = _proj_residual([attn, tm], ar_w_out[i].astype(BF16), res)
        else:
            w_in = jnp.pad(ssd_w_in[i], ((0, 0), (0, SSD_DT_PAD - SSD_HEADS))).astype(BF16)
            zx = _norm_matmul(res, mix_norm_w[layer], w_in, tn=1152, **kw)
            y = _ssd(zx, ssd_conv_w[i], ssd_conv_b[i], ssd_dt_bias[i], ssd_a_log[i], ssd_d[i], ssd_norm_w[i],
                     nbatch=nbatch, nblk=nblk)
            res = _proj_residual([y], ssd_w_out[i].astype(BF16), res)
        res = _ffn(res, ffn_norm_w[layer], ffn_w_up[layer].astype(BF16), ffn_conv_w[layer], ffn_conv_b[layer],
                   ffn_w_down[layer].astype(BF16), **kw)
    return res.reshape(nbatch, seq_p, d)[:, N_PAD + N_META:]
```

```python
import functools

import jax
import jax.numpy as jnp
from jax import lax
from jax.experimental import pallas as pl
from jax.experimental.pallas import tpu as pltpu

F32 = jnp.float32
BF16 = jnp.bfloat16

D_MODEL = 2048
N_META = 16
BLOCK = 128
N_PAD = BLOCK - N_META
NORM_EPS = 1e-6

HEAD_DIM = 64
ATTN_HEADS = 16
ATTN_KV_HEADS = 4
ATTN_GROUP = 4
WINDOW = 128
QK_EPS = 1e-6
Q_COLS = ATTN_HEADS * HEAD_DIM
KV_COLS = ATTN_KV_HEADS * HEAD_DIM
ATTN_COLS = Q_COLS + 2 * KV_COLS

RWKV_DIM = 1024
RWKV_HEAD = 64
DECAY_LORA = 64
AAA_LORA = 64
GATE_LORA = 160
LORA_COLS = DECAY_LORA + AAA_LORA + GATE_LORA
LORA_PAD = 384
RWKV_GN_EPS = 64e-5
RWKV_CHUNK = 64
AR_COLS = ATTN_COLS + 3 * RWKV_DIM + LORA_PAD

SSD_INNER = 4096
SSD_HEAD_DIM = 64
SSD_HEADS = 64
SSD_GROUPS = 8
SSD_HPG = 8
SSD_STATE = 128
SSD_CONV = 4
SSD_GROUP_COLS = SSD_INNER // SSD_GROUPS
SSD_BC_COLS = SSD_GROUPS * SSD_STATE
SSD_DT_PAD = 128
SSD_COLS = 2 * SSD_INNER + 2 * SSD_BC_COLS + SSD_DT_PAD
SSD_NORM_EPS = 1e-5

FFN_DIM = 5632
FFN_CONV = 3
FFN_CHUNK = 512

LANES = 128
SUBLANES = 8
VMEM_LIMIT = 56 * 1024 * 1024


def _params(*sem):
    return pltpu.CompilerParams(dimension_semantics=sem, vmem_limit_bytes=VMEM_LIMIT)


def _row_tile(total, target):
    best = None
    for t in range(16, min(total, target) + 1, 16):
        if total % t == 0:
            best = t
    assert best is not None, (total, target)
    return best


def _dot(a, b):
    return jnp.dot(a.astype(BF16), b.astype(BF16), preferred_element_type=F32)


def _dot_nt(a, b):
    return lax.dot_general(a.astype(BF16), b.astype(BF16), (((1,), (1,)), ((), ())),
                           preferred_element_type=F32)


def _dot_tn(a, b):
    return lax.dot_general(a.astype(BF16), b.astype(BF16), (((0,), (0,)), ((), ())),
                           preferred_element_type=F32)


def _split3(x):
    hi = x.astype(BF16)
    r1 = x - hi.astype(F32)
    mid = r1.astype(BF16)
    lo = (r1 - mid.astype(F32)).astype(BF16)
    return hi, mid, lo


def _dot_exact_rhs(sel_bf16, x):
    return jnp.dot(jnp.concatenate([sel_bf16] * 3, axis=1), jnp.concatenate(_split3(x), axis=0),
                   preferred_element_type=F32)


def _half_sums(x):
    hi = x.astype(BF16)
    lo = (x - hi.astype(F32)).astype(BF16)
    i = lax.broadcasted_iota(jnp.int32, (2 * LANES, LANES), 0) & (LANES - 1)
    j = lax.broadcasted_iota(jnp.int32, (2 * LANES, LANES), 1)
    ones2 = jnp.where((i < LANES // 2) == (j < LANES // 2), 1.0, 0.0).astype(BF16)
    return jnp.dot(jnp.concatenate([hi, lo], axis=1), ones2, preferred_element_type=F32)


def _sigmoid(x):
    return 1.0 / (1.0 + jnp.exp(-x))


def _softplus(x):
    return jnp.maximum(x, 0.0) + jnp.log(1.0 + jnp.exp(-jnp.abs(x)))


def _valid_rows(row0, nrows, seq_p, nbatch):
    rows = row0 + lax.broadcasted_iota(jnp.int32, (nrows, 1), 0)
    pos = rows
    for b in range(1, nbatch):
        pos = jnp.where(rows >= b * seq_p, rows - b * seq_p, pos)
    return pos >= N_PAD


def _masked_rmsnorm(x, w, row0, seq_p, nbatch):
    ms = jnp.mean(x * x, axis=-1, keepdims=True)
    y = x * lax.rsqrt(ms + NORM_EPS) * w
    return jnp.where(_valid_rows(row0, x.shape[0], seq_p, nbatch), y, 0.0)


def _norm_mm_kernel(x_ref, nw_ref, w_ref, o_ref, h_scr, *, tm, seq_p, nbatch):
    @pl.when(pl.program_id(1) == 0)
    def _():
        h = _masked_rmsnorm(x_ref[...], nw_ref[...], pl.program_id(0) * tm, seq_p, nbatch)
        h_scr[...] = h.astype(BF16)

    o_ref[...] = jnp.dot(h_scr[...], w_ref[...], preferred_element_type=F32)


def _norm_matmul(res, norm_w, w_bf16, *, tn, seq_p, nbatch):
    T, D = res.shape
    N = w_bf16.shape[1]
    tm = _row_tile(T, 768)
    assert N % tn == 0
    return pl.pallas_call(
        functools.partial(_norm_mm_kernel, tm=tm, seq_p=seq_p, nbatch=nbatch),
        grid=(T // tm, N // tn),
        in_specs=[pl.BlockSpec((tm, D), lambda m, n: (m, 0)),
                  pl.BlockSpec((1, D), lambda m, n: (0, 0)),
                  pl.BlockSpec((D, tn), lambda m, n: (0, n))],
        out_specs=pl.BlockSpec((tm, tn), lambda m, n: (m, n)),
        out_shape=jax.ShapeDtypeStruct((T, N), F32),
        scratch_shapes=[pltpu.VMEM((tm, D), BF16)],
        compiler_params=_params("arbitrary", "arbitrary"),
        name="norm_in_proj",
    )(res, norm_w.reshape(1, D), w_bf16)


def _proj_res_kernel(*refs, n_lhs):
    lhs = refs[:n_lhs]
    ws = refs[n_lhs:2 * n_lhs]
    res_ref, o_ref = refs[2 * n_lhs], refs[2 * n_lhs + 1]
    acc = res_ref[...]
    for l, w in zip(lhs, ws):
        acc = acc + jnp.dot(l[...], w[...], preferred_element_type=F32)
    o_ref[...] = acc


def _proj_residual(lhs_list, w_bf16, res, *, tn=1024):
    T, D = res.shape
    n_lhs = len(lhs_list)
    kw = lhs_list[0].shape[1]
    assert all(l.shape == (T, kw) for l in lhs_list) and w_bf16.shape == (n_lhs * kw, D)
    tm = _row_tile(T, 768)
    in_specs = [pl.BlockSpec((tm, kw), lambda m, n: (m, 0)) for _ in range(n_lhs)]
    in_specs += [pl.BlockSpec((kw, tn), functools.partial(lambda i, m, n: (i, n), i)) for i in range(n_lhs)]
    in_specs += [pl.BlockSpec((tm, tn), lambda m, n: (m, n))]
    return pl.pallas_call(
        functools.partial(_proj_res_kernel, n_lhs=n_lhs),
        grid=(T // tm, D // tn),
        in_specs=in_specs,
        out_specs=pl.BlockSpec((tm, tn), lambda m, n: (m, n)),
        out_shape=jax.ShapeDtypeStruct((T, D), F32),
        compiler_params=_params("arbitrary", "arbitrary"),
        name="out_proj_residual",
    )(*lhs_list, *([w_bf16] * n_lhs), res)


def _ffn_kernel(res_ref, nw_ref, wg_ref, wv_ref, cw_ref, cb_ref, wd_ref, o_ref, h_scr, carry_scr,
                *, tm, seq_p, nbatch):
    m = pl.program_id(0)
    f = pl.program_id(1)

    @pl.when(f == 0)
    def _():
        res = res_ref[...]
        h_scr[...] = _masked_rmsnorm(res, nw_ref[...], m * tm, seq_p, nbatch).astype(BF16)
        o_ref[...] = res

    @pl.when(m == 0)
    def _():
        carry_scr[f] = jnp.zeros(carry_scr.shape[1:], F32)

    h = h_scr[...]
    gate = jnp.dot(h, wg_ref[...], preferred_element_type=F32)
    val = jnp.dot(h, wv_ref[...], preferred_element_type=F32)
    prev = carry_scr[f]
    rows = lax.broadcasted_iota(jnp.int32, (tm, 1), 0)
    g1 = jnp.where(rows == 0, prev[7:8], pltpu.roll(gate, 1, axis=0))
    g2 = jnp.where(rows == 0, prev[6:7], jnp.where(rows == 1, prev[7:8], pltpu.roll(gate, 2, axis=0)))
    carry_scr[f] = gate[tm - SUBLANES:]
    cw = cw_ref[...]
    pre = cb_ref[...] + cw[0:1] * g2 + cw[1:2] * g1 + cw[2:3] * gate
    act = pre * _sigmoid(pre) * val
    o_ref[...] += jnp.dot(act.astype(BF16), wd_ref[...], preferred_element_type=F32)


def _ffn(res, norm_w, w_up_bf16, conv_w, conv_b, w_down_bf16, *, seq_p, nbatch):
    T, D = res.shape
    tm = _row_tile(T, 768)
    tf = FFN_CHUNK
    nf = FFN_DIM // tf
    return pl.pallas_call(
        functools.partial(_ffn_kernel, tm=tm, seq_p=seq_p, nbatch=nbatch),
        grid=(T // tm, nf),
        in_specs=[pl.BlockSpec((tm, D), lambda m, f: (m, 0)),
                  pl.BlockSpec((1, D), lambda m, f: (0, 0)),
                  pl.BlockSpec((D, tf), lambda m, f: (0, f)),
                  pl.BlockSpec((D, tf), lambda m, f: (0, nf + f)),
                  pl.BlockSpec((FFN_CONV, tf), lambda m, f: (0, f)),
                  pl.BlockSpec((1, tf), lambda m, f: (0, f)),
                  pl.BlockSpec((tf, D), lambda m, f: (f, 0))],
        out_specs=pl.BlockSpec((tm, D), lambda m, f: (m, 0)),
        out_shape=jax.ShapeDtypeStruct((T, D), F32),
        scratch_shapes=[pltpu.VMEM((tm, D), BF16), pltpu.VMEM((nf, SUBLANES, tf), F32)],
        compiler_params=_params("arbitrary", "arbitrary"),
        name="ffn",
    )(res, norm_w.reshape(1, D), w_up_bf16, w_up_bf16, conv_w, conv_b.reshape(1, FFN_DIM), w_down_bf16)


def _attn_kernel(sinks_ref, q_ref, kvc_ref, kvp_ref, kvm_ref, qw_ref, kw_ref, o_ref):
    n = pl.program_id(1)
    scale = HEAD_DIM ** -0.5
    lane = lax.broadcasted_iota(jnp.int32, (BLOCK, LANES), 1)
    lo_half = lane < HEAD_DIM

    kvs = (kvp_ref[...], kvc_ref[...], kvm_ref[...])
    q = q_ref[...]
    nq, nk = Q_COLS // LANES, KV_COLS // LANES
    tiles = [q[:, c * LANES:(c + 1) * LANES] for c in range(nq)]
    tiles += [kv[:, c * LANES:(c + 1) * LANES] for kv in kvs for c in range(nk)]
    sq = jnp.concatenate([t * t for t in tiles], axis=0)
    ms = _half_sums(sq) * (1.0 / HEAD_DIM)
    inv = lax.rsqrt(ms + QK_EPS)
    qw = qw_ref[...] * scale
    kw = kw_ref[...]
    qn = [tiles[c] * inv[c * BLOCK:(c + 1) * BLOCK] * qw for c in range(nq)]
    kn = [[tiles[nq + i * nk + c] * inv[(nq + i * nk + c) * BLOCK:(nq + i * nk + c + 1) * BLOCK] * kw
           for i in range(3)] for c in range(nk)]

    qi = lax.broadcasted_iota(jnp.int32, (BLOCK, 3 * BLOCK), 0)
    kj = lax.broadcasted_iota(jnp.int32, (BLOCK, 3 * BLOCK), 1)
    j_prev, j_cur, j_meta = kj, kj - BLOCK, kj - 2 * BLOCK
    first_prev = jnp.where(n == 0, BLOCK, jnp.where(n == 1, N_PAD, 0))
    first_cur = jnp.where(n == 0, N_PAD, 0)
    vis_prev = (kj < BLOCK) & (j_prev > qi) & (j_prev >= first_prev)
    vis_cur = (kj >= BLOCK) & (j_cur <= qi) & (j_cur >= first_cur)
    vis_meta = (j_meta >= N_PAD) & (n * BLOCK + qi - j_meta >= WINDOW)
    bias = jnp.where(vis_prev | vis_cur | vis_meta, 0.0, -jnp.inf)
    bias4 = jnp.concatenate([bias] * ATTN_GROUP, axis=0)
    grow = lax.broadcasted_iota(jnp.int32, (ATTN_GROUP * BLOCK, 1), 0)

    def dup_half(x, upper):
        swapped = pltpu.roll(x, HEAD_DIM, axis=1)
        lane_r = lax.broadcasted_iota(jnp.int32, x.shape, 1) < HEAD_DIM
        return jnp.where(lane_r, swapped, x) if upper else jnp.where(lane_r, x, swapped)

    outs = []
    for c in range(nk):
        k_col = jnp.concatenate(kn[c], axis=0)
        v_col = jnp.concatenate([kv[:, KV_COLS + c * LANES:KV_COLS + (c + 1) * LANES] for kv in kvs], axis=0)
        for half in range(2):
            h = 2 * c + half
            k_dup = dup_half(k_col, half == 1).astype(BF16)
            v_dup = dup_half(v_col, half == 1).astype(BF16)
            lhs = jnp.concatenate(
                [jnp.where(lo_half if e == 0 else jnp.logical_not(lo_half), qn[2 * h + pr], 0.0)
                 for pr in range(2) for e in range(2)], axis=0)
            s = _dot_nt(lhs, k_dup) + bias4
            sink = sinks_ref[ATTN_GROUP * h]
            for g in range(1, ATTN_GROUP):
                sink = jnp.where(grow >= g * BLOCK, sinks_ref[ATTN_GROUP * h + g], sink)
            mx = jnp.maximum(jnp.max(s, axis=-1, keepdims=True), sink)
            p = jnp.exp(s - mx)
            den = jnp.sum(p, axis=-1, keepdims=True) + jnp.exp(sink - mx)
            o = jnp.dot(p.astype(BF16), v_dup, preferred_element_type=F32) / den
            for pr in range(2):
                r0 = 2 * pr * BLOCK
                outs.append(jnp.where(lo_half, o[r0:r0 + BLOCK], o[r0 + BLOCK:r0 + 2 * BLOCK]))
    o_ref[...] = jnp.concatenate(outs, axis=-1).astype(o_ref.dtype)


def _attention(zin, sinks, q_norm_w, k_norm_w, *, nbatch, nblk):
    T = zin.shape[0]
    q_blk = 0
    kv_blk = Q_COLS // (2 * KV_COLS)
    row = lambda b, n: b * nblk + n
    return pl.pallas_call(
        _attn_kernel,
        grid=(nbatch, nblk),
        in_specs=[pl.BlockSpec(memory_space=pltpu.SMEM),
                  pl.BlockSpec((BLOCK, Q_COLS), lambda b, n: (row(b, n), q_blk)),
                  pl.BlockSpec((BLOCK, 2 * KV_COLS), lambda b, n: (row(b, n), kv_blk)),
                  pl.BlockSpec((BLOCK, 2 * KV_COLS), lambda b, n: (row(b, jnp.maximum(n - 1, 0)), kv_blk)),
                  pl.BlockSpec((BLOCK, 2 * KV_COLS), lambda b, n: (row(b, 0), kv_blk)),
                  pl.BlockSpec((1, LANES), lambda b, n: (0, 0)),
                  pl.BlockSpec((1, LANES), lambda b, n: (0, 0))],
        out_specs=pl.BlockSpec((BLOCK, Q_COLS), lambda b, n: (row(b, n), 0)),
        out_shape=jax.ShapeDtypeStruct((T, Q_COLS), BF16),
        compiler_params=_params("arbitrary", "arbitrary"),
        name="swa_attention",
    )(sinks, zin, zin, zin, zin, jnp.tile(q_norm_w, LANES // HEAD_DIM).reshape(1, LANES),
      jnp.tile(k_norm_w, LANES // HEAD_DIM).reshape(1, LANES))


def _pair_blockdiag(x):
    xb = x.astype(BF16)
    lane = lax.broadcasted_iota(jnp.int32, xb.shape, 2)
    zero = jnp.zeros_like(xb)
    return jnp.concatenate([jnp.where(lane < RWKV_HEAD, xb, zero), jnp.where(lane >= RWKV_HEAD, xb, zero)], axis=1)


def _bmm(a, b_bf16):
    return lax.dot_general(a.astype(BF16), b_bf16, (((2,), (1,)), ((0,), (0,))), preferred_element_type=F32)


def _bmm_nt(a, b_bf16):
    return lax.dot_general(a.astype(BF16), b_bf16, (((2,), (2,)), ((0,), (0,))), preferred_element_type=F32)


def _pair_pick(full):
    lane = lax.broadcasted_iota(jnp.int32, (full.shape[0], RWKV_HEAD, LANES), 2)
    return jnp.where(lane < RWKV_HEAD, full[:, :RWKV_HEAD], full[:, RWKV_HEAD:])


def _to_pairs(x, nchunk):
    L = RWKV_CHUNK
    return jnp.stack([x[c * L:(c + 1) * L, p * LANES:(p + 1) * LANES]
                      for c in range(nchunk) for p in range(x.shape[1] // LANES)], axis=0)


def _pair_segsum(*xs):
    npair = xs[0].shape[1] // LANES
    rows = xs[0].shape[0]
    stacked = jnp.concatenate([x[:, p * LANES:(p + 1) * LANES] for x in xs for p in range(npair)], axis=0)
    s = _half_sums(stacked)
    outs = [jnp.concatenate([s[(i * npair + p) * rows:(i * npair + p + 1) * rows] for p in range(npair)], axis=-1)
            for i in range(len(xs))]
    return outs[0] if len(xs) == 1 else outs


def _rwkv_kernel(rkv_a_ref, rkv_b_ref, lora_ref, mu_rkv_ref, mu_lora_ref, wup_wa_ref, wup_g_ref, vec_ref, o_ref,
                 s_scr, carry_rkv, carry_lora, *, tb):
    n = pl.program_id(1)
    L = RWKV_CHUNK
    npair = RWKV_DIM // LANES

    @pl.when(n == 0)
    def _():
        s_scr[...] = jnp.zeros(s_scr.shape, F32)
        carry_rkv[...] = jnp.zeros(carry_rkv.shape, F32)
        carry_lora[...] = jnp.zeros(carry_lora.shape, F32)

    rows = lax.broadcasted_iota(jnp.int32, (tb, 1), 0)
    valid = (n > 0) | (rows >= N_PAD)

    def token_shift(z, carry, mu_ref):
        prev = jnp.where(rows == 0, carry[SUBLANES - 1:SUBLANES], pltpu.roll(z, 1, axis=0))
        carry[...] = z[tb - SUBLANES:]
        return jnp.where(valid, z + (prev - z) * mu_ref[...], 0.0)

    zr = token_shift(jnp.concatenate([rkv_a_ref[...], rkv_b_ref[...]], axis=1), carry_rkv, mu_rkv_ref)
    zl = token_shift(lora_ref[...], carry_lora, mu_lora_ref)
    r, k, v = zr[:, :RWKV_DIM], zr[:, RWKV_DIM:2 * RWKV_DIM], zr[:, 2 * RWKV_DIM:]

    z_wa = zl[:, :LANES]
    col = lax.broadcasted_iota(jnp.int32, z_wa.shape, 1)
    act_wa = jnp.where(col < DECAY_LORA, jnp.tanh(z_wa), z_wa)
    up_wa = jnp.dot(act_wa.astype(BF16), wup_wa_ref[...], preferred_element_type=F32)
    g = jnp.dot(_sigmoid(zl[:, LANES:]).astype(BF16), wup_g_ref[...], preferred_element_type=F32)
    vec = vec_ref[...]
    w0, a0, k_k, k_a, r_k, ln_w, ln_b = (vec[i:i + 1] for i in range(7))
    w_log = -jnp.exp(-_softplus(-(w0 + up_wa[:, :RWKV_DIM])) - 0.5)
    w_log = jnp.where(valid, w_log, 0.0)
    a = _sigmoid(a0 + up_wa[:, RWKV_DIM:])

    kk = k * k_k
    k2 = k * (1.0 + (a - 1.0) * k_a)
    kk_sq, bonus = _pair_segsum(kk * kk, r * k2 * r_k)
    kk = kk / jnp.maximum(jnp.sqrt(kk_sq), 1e-12)
    kka = kk * a

    nchunk = tb // L
    nb = nchunk * npair
    t_i = lax.broadcasted_iota(jnp.int32, (nb, L, LANES), 1)
    s_i = lax.broadcasted_iota(jnp.int32, (nb, L, LANES), 2) & (RWKV_HEAD - 1)
    strict = s_i < t_i
    incl = s_i <= t_i
    same16 = (t_i >> 4) == (s_i >> 4)
    same32 = (t_i >> 5) == (s_i >> 5)
    eye = jnp.where(s_i == t_i, 1.0, 0.0)
    ti2 = lax.broadcasted_iota(jnp.int32, (tb, tb), 0)
    si2 = lax.broadcasted_iota(jnp.int32, (tb, tb), 1)
    lshift = L.bit_length() - 1
    tri_incl = jnp.where((si2 <= ti2) & ((si2 >> lshift) == (ti2 >> lshift)), 1.0, 0.0).astype(BF16)
    cum = _dot_exact_rhs(tri_incl, w_log)
    cum_end = jnp.concatenate(
        [jnp.broadcast_to(cum[(c + 1) * L - 1:(c + 1) * L], (L, RWKV_DIM)) for c in range(nchunk)], axis=0)
    e_w = jnp.exp(cum)
    e_iw = jnp.exp(-cum)
    e_prev = jnp.exp(cum - w_log)
    e_end = jnp.exp(cum_end - cum)
    w_end = jnp.exp(cum_end)
    ah = _to_pairs(-kk * e_prev, nchunk)
    bh = _to_pairs(kka * e_iw, nchunk)
    kh = _to_pairs(k2 * e_iw, nchunk)
    rh = _to_pairs(r * e_w, nchunk)
    bt = _to_pairs(kka * e_end, nchunk)
    kt = _to_pairs(k2 * e_end, nchunk)
    vp = _to_pairs(v, nchunk)
    vbd = _pair_blockdiag(vp)

    ar = jnp.concatenate([ah, rh], axis=1)
    m = _bmm_nt(ar, jnp.concatenate([_pair_blockdiag(bh), _pair_blockdiag(kh)], axis=1))
    a_ab = jnp.where(strict, m[:, :L, :LANES], 0.0)
    a_rb = jnp.where(incl, m[:, L:, :LANES], 0.0)
    a_ak = jnp.where(strict, m[:, :L, LANES:], 0.0)
    a_rk = jnp.where(incl, m[:, L:, LANES:], 0.0)
    kv = _bmm(jnp.concatenate([a_ak, a_rk], axis=1), vbd)
    akv, y_rk = kv[:, :L], kv[:, L:]
    a1 = jnp.where(same16, a_ab, 0.0)
    t1 = eye + a1
    a2 = _bmm(a1, _pair_blockdiag(a1))
    x = _bmm(jnp.concatenate([a2, t1], axis=1), _pair_blockdiag(a2))
    a4, t2 = x[:, :L], t1 + x[:, L:]
    x = _bmm(jnp.concatenate([a4, t2], axis=1), _pair_blockdiag(a4))
    a8, t3 = x[:, :L], t2 + x[:, L:]
    t16 = t3 + _bmm(t3, _pair_blockdiag(a8))
    off32 = jnp.where(same32 & jnp.logical_not(same16), a_ab, 0.0)
    t32 = t16 + _bmm(t16, _pair_blockdiag(_bmm(off32, _pair_blockdiag(t16))))
    off64 = jnp.where(same32, 0.0, a_ab)
    tinv = t32 + _bmm(t32, _pair_blockdiag(_bmm(off64, _pair_blockdiag(t32))))
    tg = _bmm(tinv, jnp.concatenate([_pair_blockdiag(ah), _pair_blockdiag(akv)], axis=2))
    ta, gm = tg[:, :, :LANES], tg[:, :, LANES:]
    rg = _bmm(a_rb, jnp.concatenate([_pair_blockdiag(ta), _pair_blockdiag(gm)], axis=2))
    rt = rh + rg[:, :, :LANES]
    y_intra = rg[:, :, LANES:] + y_rk
    tgv_t = jnp.swapaxes(jnp.concatenate([ta, gm, vp], axis=2), 1, 2)
    pp = _bmm(tgv_t[:, :2 * LANES], bt.astype(BF16))
    phi = _pair_pick(pp[:, :LANES])
    psi = _pair_pick(pp[:, LANES:]) + _pair_pick(_bmm(tgv_t[:, 2 * LANES:], kt.astype(BF16)))
    w_end_p = _to_pairs(w_end, nchunk)[:, :1]

    y_chunks = []
    s = s_scr[...]
    for c in range(nchunk):
        cs = slice(c * npair, (c + 1) * npair)
        y_c = _bmm_nt(rt[cs], _pair_blockdiag(s)) + y_intra[cs]
        s = s * w_end_p[cs] + _bmm(s, _pair_blockdiag(phi[cs])) + psi[cs]
        y_chunks.append(jnp.concatenate([y_c[p] for p in range(npair)], axis=-1))
    s_scr[...] = s
    y = jnp.concatenate(y_chunks, axis=0)

    inv_n = 1.0 / RWKV_HEAD
    mu = _pair_segsum(y) * inv_n
    d = y - mu
    var = _pair_segsum(d * d) * inv_n
    yn = d * lax.rsqrt(var + RWKV_GN_EPS) * ln_w + ln_b
    o_ref[...] = ((yn + bonus * v) * g).astype(o_ref.dtype)


def _rwkv(zin, mu_rkv, mu_lora, wup_wa, wup_g, vecs, *, nbatch, nblk):
    T = zin.shape[0]
    tb = BLOCK
    half = 3 * RWKV_DIM // 2
    assert ATTN_COLS == half and (ATTN_COLS + 3 * RWKV_DIM) % LORA_PAD == 0
    lora_blk = (ATTN_COLS + 3 * RWKV_DIM) // LORA_PAD
    row = lambda b, n: b * nblk + n
    return pl.pallas_call(
        functools.partial(_rwkv_kernel, tb=tb),
        grid=(nbatch, nblk),
        in_specs=[pl.BlockSpec((tb, half), lambda b, n: (row(b, n), 1)),
                  pl.BlockSpec((tb, half), lambda b, n: (row(b, n), 2)),
                  pl.BlockSpec((tb, LORA_PAD), lambda b, n: (row(b, n), lora_blk)),
                  pl.BlockSpec((1, 3 * RWKV_DIM), lambda b, n: (0, 0)),
                  pl.BlockSpec((1, LORA_PAD), lambda b, n: (0, 0)),
                  pl.BlockSpec((LANES, 2 * RWKV_DIM), lambda b, n: (0, 0)),
                  pl.BlockSpec((LORA_PAD - LANES, RWKV_DIM), lambda b, n: (0, 0)),
                  pl.BlockSpec((SUBLANES, RWKV_DIM), lambda b, n: (0, 0))],
        out_specs=pl.BlockSpec((tb, RWKV_DIM), lambda b, n: (row(b, n), 0)),
        out_shape=jax.ShapeDtypeStruct((T, RWKV_DIM), BF16),
        scratch_shapes=[pltpu.VMEM((RWKV_DIM // LANES, RWKV_HEAD, LANES), F32),
                        pltpu.VMEM((SUBLANES, 3 * RWKV_DIM), F32),
                        pltpu.VMEM((SUBLANES, LORA_PAD), F32)],
        compiler_params=_params("arbitrary", "arbitrary"),
        name="rwkv7",
    )(zin, zin, zin, mu_rkv, mu_lora, wup_wa, wup_g, vecs)


def _conv4_silu(xc, carry, w, b, valid):
    full = jnp.concatenate([carry[...], xc], axis=0)
    acc = b
    for j in range(SSD_CONV):
        k = SSD_CONV - 1 - j
        term = xc if k == 0 else pltpu.roll(full, k, axis=0)[SUBLANES:]
        acc = acc + w[j:j + 1] * term
    carry[...] = xc[BLOCK - SUBLANES:]
    return jnp.where(valid, acc * _sigmoid(acc), 0.0)


def _ssd_kernel(z_ref, x_ref, b_ref, c_ref, dt_ref, cwx_ref, cwb_ref, cwc_ref, cbx_ref, cbb_ref, cbc_ref,
                dtb_ref, alog_ref, dskip_ref, nw_ref, exp_ref, o_ref,
                st_scr, carry_x, carry_b, carry_c):
    n = pl.program_id(1)

    @pl.when(n == 0)
    def _():
        st_scr[...] = jnp.zeros(st_scr.shape, F32)
        carry_x[...] = jnp.zeros(carry_x.shape, F32)
        carry_b[...] = jnp.zeros(carry_b.shape, F32)
        carry_c[...] = jnp.zeros(carry_c.shape, F32)

    rows = lax.broadcasted_iota(jnp.int32, (BLOCK, 1), 0)
    valid = (n > 0) | (rows >= N_PAD)
    x = _conv4_silu(x_ref[...], carry_x, cwx_ref[...], cbx_ref[...], valid)
    bm = _conv4_silu(b_ref[...], carry_b, cwb_ref[...], cbb_ref[...], valid)
    cm = _conv4_silu(c_ref[...], carry_c, cwc_ref[...], cbc_ref[...], valid)

    dt = jnp.where(valid, _softplus(dt_ref[...] + dtb_ref[...]), 0.0)
    adt = dt * (-jnp.exp(alog_ref[...]))
    ti = lax.broadcasted_iota(jnp.int32, (BLOCK, BLOCK), 0)
    si = lax.broadcasted_iota(jnp.int32, (BLOCK, BLOCK), 1)
    causal = si <= ti
    cum = _dot_exact_rhs(jnp.where(causal, 1.0, 0.0).astype(BF16), adt)
    cum_t = cum.T
    ecum = jnp.exp(cum)
    dt_end = dt * jnp.exp(cum[BLOCK - 1:BLOCK] - cum)
    e_hi = ecum.astype(BF16)
    e_lo = (ecum - e_hi.astype(F32)).astype(BF16)
    ecum_x = jnp.dot(jnp.concatenate([e_hi, e_lo], axis=1), exp_ref[...], preferred_element_type=F32)
    ex = jnp.dot(jnp.concatenate([dt, dt_end], axis=0).astype(BF16), exp_ref[:SSD_DT_PAD],
                 preferred_element_type=F32)
    xdt = x * ex[:BLOCK]
    xend = x * ex[BLOCK:]
    lane = lax.broadcasted_iota(jnp.int32, (BLOCK, LANES), 1)
    lo_half = lane < SSD_HEAD_DIM

    y_groups = []
    for g in range(SSD_GROUPS):
        gs = slice(g * SSD_GROUP_COLS, (g + 1) * SSD_GROUP_COLS)
        bg = bm[:, g * SSD_STATE:(g + 1) * SSD_STATE]
        cg = cm[:, g * SSD_STATE:(g + 1) * SSD_STATE]
        cb = _dot_nt(cg, bg)
        st = st_scr[g]
        y_inter = _dot(cg, st) * ecum_x[:, gs]
        y_pairs = []
        for q in range(SSD_HPG // 2):
            mats = []
            for j in (g * SSD_HPG + 2 * q, g * SSD_HPG + 2 * q + 1):
                seg = cum[:, j:j + 1] - cum_t[j:j + 1, :]
                mats.append(cb * jnp.exp(jnp.where(causal, seg, -jnp.inf)))
            ll = jnp.concatenate(mats, axis=1).astype(BF16)
            c0 = g * SSD_GROUP_COLS + q * LANES
            xp = xdt[:, c0:c0 + LANES]
            bdx = jnp.concatenate([jnp.where(lo_half, xp, 0.0), jnp.where(lo_half, 0.0, xp)], axis=0)
            y_pairs.append(jnp.dot(ll, bdx.astype(BF16), preferred_element_type=F32))
        y_intra = jnp.concatenate(y_pairs, axis=-1)
        st_scr[g] = st * ecum_x[BLOCK - 1:BLOCK, gs] + _dot_tn(bg, xend[:, gs])
        y_groups.append(y_intra + y_inter + dskip_ref[:, gs] * x[:, gs])
    z = z_ref[...]
    nw = nw_ref[...]
    outs = []
    for g in range(SSD_GROUPS):
        gs = slice(g * SSD_GROUP_COLS, (g + 1) * SSD_GROUP_COLS)
        zg = z[:, gs]
        yg = y_groups[g] * (zg * _sigmoid(zg))
        yg = yg * lax.rsqrt(jnp.mean(yg * yg, axis=-1, keepdims=True) + SSD_NORM_EPS)
        outs.append((yg * nw[:, gs]).astype(o_ref.dtype))
    o_ref[...] = jnp.concatenate(outs, axis=-1)


def _ssd(zx, conv_w, conv_b, dt_bias, a_log, d_skip, norm_w, *, nbatch, nblk):
    T = zx.shape[0]
    row = lambda b, n: b * nblk + n
    pad_h = lambda t: jnp.pad(t.astype(F32), (0, SSD_DT_PAD - SSD_HEADS)).reshape(1, SSD_DT_PAD)
    head = lax.broadcasted_iota(jnp.int32, (SSD_DT_PAD, SSD_INNER), 0)
    colh = lax.broadcasted_iota(jnp.int32, (SSD_DT_PAD, SSD_INNER), 1) // SSD_HEAD_DIM
    expand = jnp.tile((head == colh).astype(BF16), (2, 1))
    cb2 = conv_b.reshape(1, -1)
    xs, bs, cs = slice(0, SSD_INNER), slice(SSD_INNER, SSD_INNER + SSD_BC_COLS), slice(SSD_INNER + SSD_BC_COLS, None)
    full = lambda shape: pl.BlockSpec(shape, lambda b, n: (0, 0))
    x_blk = 1
    b_blk = 2 * SSD_INNER // SSD_BC_COLS
    dt_blk = (2 * SSD_INNER + 2 * SSD_BC_COLS) // SSD_DT_PAD
    return pl.pallas_call(
        _ssd_kernel,
        grid=(nbatch, nblk),
        in_specs=[pl.BlockSpec((BLOCK, SSD_INNER), lambda b, n: (row(b, n), 0)),
                  pl.BlockSpec((BLOCK, SSD_INNER), lambda b, n: (row(b, n), x_blk)),
                  pl.BlockSpec((BLOCK, SSD_BC_COLS), lambda b, n: (row(b, n), b_blk)),
                  pl.BlockSpec((BLOCK, SSD_BC_COLS), lambda b, n: (row(b, n), b_blk + 1)),
                  pl.BlockSpec((BLOCK, SSD_DT_PAD), lambda b, n: (row(b, n), dt_blk)),
                  full((SSD_CONV, SSD_INNER)), full((SSD_CONV, SSD_BC_COLS)), full((SSD_CONV, SSD_BC_COLS)),
                  full((1, SSD_INNER)), full((1, SSD_BC_COLS)), full((1, SSD_BC_COLS)),
                  full((1, SSD_DT_PAD)), full((1, SSD_DT_PAD)), full((1, SSD_INNER)), full((1, SSD_INNER)),
                  full((2 * SSD_DT_PAD, SSD_INNER))],
        out_specs=pl.BlockSpec((BLOCK, SSD_INNER), lambda b, n: (row(b, n), 0)),
        out_shape=jax.ShapeDtypeStruct((T, SSD_INNER), BF16),
        scratch_shapes=[pltpu.VMEM((SSD_GROUPS, SSD_STATE, SSD_GROUP_COLS), F32),
                        pltpu.VMEM((SUBLANES, SSD_INNER), F32),
                        pltpu.VMEM((SUBLANES, SSD_BC_COLS), F32),
                        pltpu.VMEM((SUBLANES, SSD_BC_COLS), F32)],
        compiler_params=_params("arbitrary", "arbitrary"),
        name="ssd",
    )(zx, zx, zx, zx, zx, conv_w[:, xs], conv_w[:, bs], conv_w[:, cs], cb2[:, xs], cb2[:, bs], cb2[:, cs],
      pad_h(dt_bias), pad_h(a_log), jnp.repeat(d_skip.astype(F32), SSD_HEAD_DIM).reshape(1, SSD_INNER),
      norm_w.reshape(1, SSD_INNER), expand)


def _pack_ar_w_in(w):
    return jnp.pad(w, ((0, 0), (0, LORA_PAD - LORA_COLS))).astype(BF16)


def _pack_lora_up(w_up, a_up, g_up):
    assert DECAY_LORA + AAA_LORA == LANES
    zero = jnp.zeros((DECAY_LORA, RWKV_DIM), F32)
    wup_wa = jnp.concatenate([jnp.concatenate([w_up, zero], axis=1), jnp.concatenate([zero, a_up], axis=1)], axis=0)
    wup_g = jnp.pad(g_up, ((0, LORA_PAD - LANES - GATE_LORA), (0, 0)))
    return wup_wa.astype(BF16), wup_g.astype(BF16)


def kernel(x, meta_tokens, mix_norm_w, ffn_norm_w, ar_w_in, ar_shift_mu, attn_q_norm_w, attn_k_norm_w, attn_sinks, rwkv_w0, rwkv_w_up, rwkv_a0, rwkv_a_up, rwkv_g_up, rwkv_k_k, rwkv_k_a, rwkv_r_k, rwkv_ln_w, rwkv_ln_b, ar_w_out, ssd_w_in, ssd_conv_w, ssd_conv_b, ssd_dt_bias, ssd_a_log, ssd_d, ssd_norm_w, ssd_w_out, ffn_w_up, ffn_conv_w, ffn_conv_b, ffn_w_down):
    nbatch, seq, d = x.shape
    assert d == D_MODEL and seq % BLOCK == 0
    seq_p = N_PAD + N_META + seq
    nblk = seq_p // BLOCK
    depth = mix_norm_w.shape[0]
    res = jnp.concatenate([
        jnp.zeros((nbatch, N_PAD, d), x.dtype),
        jnp.broadcast_to(meta_tokens.astype(x.dtype)[None], (nbatch, N_META, d)),
        x], axis=1).reshape(nbatch * seq_p, d)
    kw = dict(seq_p=seq_p, nbatch=nbatch)
    for layer in range(depth):
        i = layer // 2
        if layer % 2 == 0:
            zin = _norm_matmul(res, mix_norm_w[layer], _pack_ar_w_in(ar_w_in[i]), tn=AR_COLS // 3, **kw)
            attn = _attention(zin, attn_sinks[i].astype(F32), attn_q_norm_w[i], attn_k_norm_w[i],
                              nbatch=nbatch, nblk=nblk)
            mu = ar_shift_mu[i]
            mu_rkv = mu[:3 * RWKV_DIM].reshape(1, -1)
            mu_lora = jnp.pad(mu[3 * RWKV_DIM:], (0, LORA_PAD - LORA_COLS)).reshape(1, -1)
            vecs = jnp.stack([rwkv_w0[i], rwkv_a0[i], rwkv_k_k[i], rwkv_k_a[i], rwkv_r_k[i].reshape(-1),
                              rwkv_ln_w[i], rwkv_ln_b[i], jnp.zeros((RWKV_DIM,), F32)]).astype(F32)
            tm = _rwkv(zin, mu_rkv, mu_lora, *_pack_lora_up(rwkv_w_up[i], rwkv_a_up[i], rwkv_g_up[i]), vecs,
                       nbatch=nbatch, nblk=nblk)
            res = _proj_residual([attn, tm], ar_w_out[i].astype(BF16), res)
        else:
            w_in = jnp.pad(ssd_w_in[i], ((0, 0), (0, SSD_DT_PAD - SSD_HEADS))).astype(BF16)
            zx = _norm_matmul(res, mix_norm_w[layer], w_in, tn=1152, **kw)
            y = _ssd(zx, ssd_conv_w[i], ssd_conv_b[i], ssd_dt_bias[i], ssd_a_log[i], ssd_d[i], ssd_norm_w[i],
                     nbatch=nbatch, nblk=nblk)
            res = _proj_residual([y], ssd_w_out[i].astype(BF16), res)
        res = _ffn(res, ffn_norm_w[layer], ffn_w_up[layer].astype(BF16), ffn_conv_w[layer], ffn_conv_b[layer],
                   ffn_w_down[layer].astype(BF16), **kw)
    return res.reshape(nbatch, seq_p, d)[:, N_PAD + N_META:]
```

```python
import functools

import jax
import jax.numpy as jnp
from jax import lax
from jax.experimental import pallas as pl
from jax.experimental.pallas import tpu as pltpu

F32 = jnp.float32
BF16 = jnp.bfloat16

D_MODEL = 2048
N_META = 16
BLOCK = 128
N_PAD = BLOCK - N_META
NORM_EPS = 1e-6

HEAD_DIM = 64
ATTN_HEADS = 16
ATTN_KV_HEADS = 4
ATTN_GROUP = 4
WINDOW = 128
QK_EPS = 1e-6
Q_COLS = ATTN_HEADS * HEAD_DIM
KV_COLS = ATTN_KV_HEADS * HEAD_DIM
ATTN_COLS = Q_COLS + 2 * KV_COLS

RWKV_DIM = 1024
RWKV_HEAD = 64
DECAY_LORA = 64
AAA_LORA = 64
GATE_LORA = 160
LORA_COLS = DECAY_LORA + AAA_LORA + GATE_LORA
LORA_PAD = 384
RWKV_GN_EPS = 64e-5
RWKV_CHUNK = 64
AR_COLS = ATTN_COLS + 3 * RWKV_DIM + LORA_PAD

SSD_INNER = 4096
SSD_HEAD_DIM = 64
SSD_HEADS = 64
SSD_GROUPS = 8
SSD_HPG = 8
SSD_STATE = 128
SSD_CONV = 4
SSD_GROUP_COLS = SSD_INNER // SSD_GROUPS
SSD_BC_COLS = SSD_GROUPS * SSD_STATE
SSD_DT_PAD = 128
SSD_NORM_EPS = 1e-5

FFN_DIM = 5632
FFN_CONV = 3
FFN_CHUNK = 512
CONV_SLAB = 256

LANES = 128
SUBLANES = 8
VMEM_LIMIT = 56 * 1024 * 1024


def _params(*sem):
    return pltpu.CompilerParams(dimension_semantics=sem, vmem_limit_bytes=VMEM_LIMIT)


def _row_tile(total, target):
    best = None
    for t in range(16, min(total, target) + 1, 16):
        if total % t == 0:
            best = t
    assert best is not None, (total, target)
    return best


def _dot(a, b):
    return jnp.dot(a.astype(BF16), b.astype(BF16), preferred_element_type=F32)


def _dot_nt(a, b):
    return lax.dot_general(a.astype(BF16), b.astype(BF16), (((1,), (1,)), ((), ())),
                           preferred_element_type=F32)


def _dot_tn(a, b):
    return lax.dot_general(a.astype(BF16), b.astype(BF16), (((0,), (0,)), ((), ())),
                           preferred_element_type=F32)


def _split3(x):
    hi = x.astype(BF16)
    r1 = x - hi.astype(F32)
    mid = r1.astype(BF16)
    lo = (r1 - mid.astype(F32)).astype(BF16)
    return hi, mid, lo


def _dot_exact_rhs(sel_bf16, x):
    return jnp.dot(jnp.concatenate([sel_bf16] * 3, axis=1), jnp.concatenate(_split3(x), axis=0),
                   preferred_element_type=F32)


def _half_sums(x):
    hi = x.astype(BF16)
    lo = (x - hi.astype(F32)).astype(BF16)
    i = lax.broadcasted_iota(jnp.int32, (2 * LANES, LANES), 0) & (LANES - 1)
    j = lax.broadcasted_iota(jnp.int32, (2 * LANES, LANES), 1)
    ones2 = jnp.where((i < LANES // 2) == (j < LANES // 2), 1.0, 0.0).astype(BF16)
    return jnp.dot(jnp.concatenate([hi, lo], axis=1), ones2, preferred_element_type=F32)


def _sigmoid(x):
    return 1.0 / (1.0 + jnp.exp(-x))


def _softplus(x):
    return jnp.maximum(x, 0.0) + jnp.log(1.0 + jnp.exp(-jnp.abs(x)))


def _valid_rows(row0, nrows, seq_p, nbatch):
    rows = row0 + lax.broadcasted_iota(jnp.int32, (nrows, 1), 0)
    pos = rows
    for b in range(1, nbatch):
        pos = jnp.where(rows >= b * seq_p, rows - b * seq_p, pos)
    return pos >= N_PAD


def _masked_rmsnorm(x, w, row0, seq_p, nbatch):
    ms = jnp.mean(x * x, axis=-1, keepdims=True)
    y = x * lax.rsqrt(ms + NORM_EPS) * w
    return jnp.where(_valid_rows(row0, x.shape[0], seq_p, nbatch), y, 0.0)


def _norm_mm_kernel(x_ref, nw_ref, w_ref, o_ref, h_scr, *, tm, seq_p, nbatch):
    @pl.when(pl.program_id(1) == 0)
    def _():
        h = _masked_rmsnorm(x_ref[...], nw_ref[...], pl.program_id(0) * tm, seq_p, nbatch)
        h_scr[...] = h.astype(BF16)

    o_ref[...] = jnp.dot(h_scr[...], w_ref[...], preferred_element_type=F32)


def _norm_matmul(res, norm_w, w_bf16, *, tn, seq_p, nbatch):
    T, D = res.shape
    N = w_bf16.shape[1]
    tm = _row_tile(T, 768)
    assert N % tn == 0
    return pl.pallas_call(
        functools.partial(_norm_mm_kernel, tm=tm, seq_p=seq_p, nbatch=nbatch),
        grid=(T // tm, N // tn),
        in_specs=[pl.BlockSpec((tm, D), lambda m, n: (m, 0)),
                  pl.BlockSpec((1, D), lambda m, n: (0, 0)),
                  pl.BlockSpec((D, tn), lambda m, n: (0, n))],
        out_specs=pl.BlockSpec((tm, tn), lambda m, n: (m, n)),
        out_shape=jax.ShapeDtypeStruct((T, N), F32),
        scratch_shapes=[pltpu.VMEM((tm, D), BF16)],
        compiler_params=_params("arbitrary", "arbitrary"),
        name="norm_in_proj",
    )(res, norm_w.reshape(1, D), w_bf16)


def _norm_mm_conv_kernel(x_ref, nw_ref, w_ref, cw_ref, cb_ref, o_ref, h_scr, carry_scr, *, tm, seq_p, nbatch):
    m = pl.program_id(0)
    n = pl.program_id(1)

    @pl.when(n == 0)
    def _():
        h = _masked_rmsnorm(x_ref[...], nw_ref[...], m * tm, seq_p, nbatch)
        h_scr[...] = h.astype(BF16)

    @pl.when(m == 0)
    def _():
        carry_scr[n] = jnp.zeros(carry_scr.shape[1:], F32)

    valid = _valid_rows(m * tm, tm, seq_p, nbatch)
    h = h_scr[...]
    for s0 in range(0, o_ref.shape[1], CONV_SLAB):
        sl = slice(s0, s0 + CONV_SLAB)
        xc = jnp.dot(h, w_ref[:, sl], preferred_element_type=F32)
        full = jnp.concatenate([carry_scr[n, :, sl], xc], axis=0)
        carry_scr[n, :, sl] = xc[tm - SUBLANES:]
        acc = cb_ref[:, sl]
        for j in range(SSD_CONV):
            k = SSD_CONV - 1 - j
            term = xc if k == 0 else pltpu.roll(full, k, axis=0)[SUBLANES:]
            acc = acc + cw_ref[j:j + 1, sl] * term
        o_ref[:, sl] = jnp.where(valid, acc * _sigmoid(acc), 0.0)


def _norm_matmul_conv(res, norm_w, w_bf16, conv_w, conv_b, *, tn, seq_p, nbatch):
    T, D = res.shape
    N = w_bf16.shape[1]
    tm = _row_tile(T, 768)
    assert N % tn == 0
    return pl.pallas_call(
        functools.partial(_norm_mm_conv_kernel, tm=tm, seq_p=seq_p, nbatch=nbatch),
        grid=(T // tm, N // tn),
        in_specs=[pl.BlockSpec((tm, D), lambda m, n: (m, 0)),
                  pl.BlockSpec((1, D), lambda m, n: (0, 0)),
                  pl.BlockSpec((D, tn), lambda m, n: (0, n)),
                  pl.BlockSpec((SSD_CONV, tn), lambda m, n: (0, n)),
                  pl.BlockSpec((1, tn), lambda m, n: (0, n))],
        out_specs=pl.BlockSpec((tm, tn), lambda m, n: (m, n)),
        out_shape=jax.ShapeDtypeStruct((T, N), F32),
        scratch_shapes=[pltpu.VMEM((tm, D), BF16), pltpu.VMEM((N // tn, SUBLANES, tn), F32)],
        compiler_params=_params("arbitrary", "arbitrary"),
        name="norm_in_proj_conv",
    )(res, norm_w.reshape(1, D), w_bf16, conv_w, conv_b.reshape(1, N))


def _proj_res_kernel(*refs, n_lhs):
    lhs = refs[:n_lhs]
    ws = refs[n_lhs:2 * n_lhs]
    res_ref, o_ref = refs[2 * n_lhs], refs[2 * n_lhs + 1]
    acc = res_ref[...]
    for l, w in zip(lhs, ws):
        acc = acc + jnp.dot(l[...], w[...], preferred_element_type=F32)
    o_ref[...] = acc


def _proj_residual(lhs_list, w_bf16, res, *, tn=1024):
    T, D = res.shape
    n_lhs = len(lhs_list)
    kw = lhs_list[0].shape[1]
    assert all(l.shape == (T, kw) for l in lhs_list) and w_bf16.shape == (n_lhs * kw, D)
    tm = _row_tile(T, 768)
    in_specs = [pl.BlockSpec((tm, kw), lambda m, n: (m, 0)) for _ in range(n_lhs)]
    in_specs += [pl.BlockSpec((kw, tn), functools.partial(lambda i, m, n: (i, n), i)) for i in range(n_lhs)]
    in_specs += [pl.BlockSpec((tm, tn), lambda m, n: (m, n))]
    return pl.pallas_call(
        functools.partial(_proj_res_kernel, n_lhs=n_lhs),
        grid=(T // tm, D // tn),
        in_specs=in_specs,
        out_specs=pl.BlockSpec((tm, tn), lambda m, n: (m, n)),
        out_shape=jax.ShapeDtypeStruct((T, D), F32),
        compiler_params=_params("arbitrary", "arbitrary"),
        name="out_proj_residual",
    )(*lhs_list, *([w_bf16] * n_lhs), res)


def _ffn_kernel(res_ref, nw_ref, wg_ref, wv_ref, cw_ref, cb_ref, wd_ref, o_ref, h_scr, carry_scr,
                *, tm, seq_p, nbatch):
    m = pl.program_id(0)
    f = pl.program_id(1)

    @pl.when(f == 0)
    def _():
        res = res_ref[...]
        h_scr[...] = _masked_rmsnorm(res, nw_ref[...], m * tm, seq_p, nbatch).astype(BF16)
        o_ref[...] = res

    @pl.when(m == 0)
    def _():
        carry_scr[f] = jnp.zeros(carry_scr.shape[1:], F32)

    h = h_scr[...]
    gate = jnp.dot(h, wg_ref[...], preferred_element_type=F32)
    val = jnp.dot(h, wv_ref[...], preferred_element_type=F32)
    prev = carry_scr[f]
    rows = lax.broadcasted_iota(jnp.int32, (tm, 1), 0)
    g1 = jnp.where(rows == 0, prev[7:8], pltpu.roll(gate, 1, axis=0))
    g2 = jnp.where(rows == 0, prev[6:7], jnp.where(rows == 1, prev[7:8], pltpu.roll(gate, 2, axis=0)))
    carry_scr[f] = gate[tm - SUBLANES:]
    cw = cw_ref[...]
    pre = cb_ref[...] + cw[0:1] * g2 + cw[1:2] * g1 + cw[2:3] * gate
    act = pre * _sigmoid(pre) * val
    o_ref[...] += jnp.dot(act.astype(BF16), wd_ref[...], preferred_element_type=F32)


def _ffn(res, norm_w, w_up_bf16, conv_w, conv_b, w_down_bf16, *, seq_p, nbatch):
    T, D = res.shape
    tm = _row_tile(T, 768)
    tf = FFN_CHUNK
    nf = FFN_DIM // tf
    return pl.pallas_call(
        functools.partial(_ffn_kernel, tm=tm, seq_p=seq_p, nbatch=nbatch),
        grid=(T // tm, nf),
        in_specs=[pl.BlockSpec((tm, D), lambda m, f: (m, 0)),
                  pl.BlockSpec((1, D), lambda m, f: (0, 0)),
                  pl.BlockSpec((D, tf), lambda m, f: (0, f)),
                  pl.BlockSpec((D, tf), lambda m, f: (0, nf + f)),
                  pl.BlockSpec((FFN_CONV, tf), lambda m, f: (0, f)),
                  pl.BlockSpec((1, tf), lambda m, f: (0, f)),
                  pl.BlockSpec((tf, D), lambda m, f: (f, 0))],
        out_specs=pl.BlockSpec((tm, D), lambda m, f: (m, 0)),
        out_shape=jax.ShapeDtypeStruct((T, D), F32),
        scratch_shapes=[pltpu.VMEM((tm, D), BF16), pltpu.VMEM((nf, SUBLANES, tf), F32)],
        compiler_params=_params("arbitrary", "arbitrary"),
        name="ffn",
    )(res, norm_w.reshape(1, D), w_up_bf16, w_up_bf16, conv_w, conv_b.reshape(1, FFN_DIM), w_down_bf16)


def _attn_kernel(sinks_ref, q_ref, kvc_ref, kvp_ref, kvm_ref, qw_ref, kw_ref, o_ref):
    n = pl.program_id(1)
    scale = HEAD_DIM ** -0.5
    lane = lax.broadcasted_iota(jnp.int32, (BLOCK, LANES), 1)
    lo_half = lane < HEAD_DIM

    kvs = (kvp_ref[...], kvc_ref[...], kvm_ref[...])
    q = q_ref[...]
    nq, nk = Q_COLS // LANES, KV_COLS // LANES
    tiles = [q[:, c * LANES:(c + 1) * LANES] for c in range(nq)]
    tiles += [kv[:, c * LANES:(c + 1) * LANES] for kv in kvs for c in range(nk)]
    sq = jnp.concatenate([t * t for t in tiles], axis=0)
    ms = _half_sums(sq) * (1.0 / HEAD_DIM)
    inv = lax.rsqrt(ms + QK_EPS)
    qw = qw_ref[...] * scale
    kw = kw_ref[...]
    qn = [tiles[c] * inv[c * BLOCK:(c + 1) * BLOCK] * qw for c in range(nq)]
    kn = [[tiles[nq + i * nk + c] * inv[(nq + i * nk + c) * BLOCK:(nq + i * nk + c + 1) * BLOCK] * kw
           for i in range(3)] for c in range(nk)]

    qi = lax.broadcasted_iota(jnp.int32, (BLOCK, 3 * BLOCK), 0)
    kj = lax.broadcasted_iota(jnp.int32, (BLOCK, 3 * BLOCK), 1)
    j_prev, j_cur, j_meta = kj, kj - BLOCK, kj - 2 * BLOCK
    first_prev = jnp.where(n == 0, BLOCK, jnp.where(n == 1, N_PAD, 0))
    first_cur = jnp.where(n == 0, N_PAD, 0)
    vis_prev = (kj < BLOCK) & (j_prev > qi) & (j_prev >= first_prev)
    vis_cur = (kj >= BLOCK) & (j_cur <= qi) & (j_cur >= first_cur)
    vis_meta = (j_meta >= N_PAD) & (n * BLOCK + qi - j_meta >= WINDOW)
    bias = jnp.where(vis_prev | vis_cur | vis_meta, 0.0, -jnp.inf)
    bias4 = jnp.concatenate([bias] * ATTN_GROUP, axis=0)
    grow = lax.broadcasted_iota(jnp.int32, (ATTN_GROUP * BLOCK, 1), 0)

    def dup_half(x, upper):
        swapped = pltpu.roll(x, HEAD_DIM, axis=1)
        lane_r = lax.broadcasted_iota(jnp.int32, x.shape, 1) < HEAD_DIM
        return jnp.where(lane_r, swapped, x) if upper else jnp.where(lane_r, x, swapped)

    outs = []
    for c in range(nk):
        k_col = jnp.concatenate(kn[c], axis=0)
        v_col = jnp.concatenate([kv[:, KV_COLS + c * LANES:KV_COLS + (c + 1) * LANES] for kv in kvs], axis=0)
        for half in range(2):
            h = 2 * c + half
            k_dup = dup_half(k_col, half == 1).astype(BF16)
            v_dup = dup_half(v_col, half == 1).astype(BF16)
            lhs = jnp.concatenate(
                [jnp.where(lo_half if e == 0 else jnp.logical_not(lo_half), qn[2 * h + pr], 0.0)
                 for pr in range(2) for e in range(2)], axis=0)
            s = _dot_nt(lhs, k_dup) + bias4
            sink = sinks_ref[ATTN_GROUP * h]
            for g in range(1, ATTN_GROUP):
                sink = jnp.where(grow >= g * BLOCK, sinks_ref[ATTN_GROUP * h + g], sink)
            mx = jnp.maximum(jnp.max(s, axis=-1, keepdims=True), sink)
            p = jnp.exp(s - mx)
            den = jnp.sum(p, axis=-1, keepdims=True) + jnp.exp(sink - mx)
            o = jnp.dot(p.astype(BF16), v_dup, preferred_element_type=F32) / den
            for pr in range(2):
                r0 = 2 * pr * BLOCK
                outs.append(jnp.where(lo_half, o[r0:r0 + BLOCK], o[r0 + BLOCK:r0 + 2 * BLOCK]))
    o_ref[...] = jnp.concatenate(outs, axis=-1).astype(o_ref.dtype)


def _attention(zin, sinks, q_norm_w, k_norm_w, *, nbatch, nblk):
    T = zin.shape[0]
    q_blk = 0
    kv_blk = Q_COLS // (2 * KV_COLS)
    row = lambda b, n: b * nblk + n
    return pl.pallas_call(
        _attn_kernel,
        grid=(nbatch, nblk),
        in_specs=[pl.BlockSpec(memory_space=pltpu.SMEM),
                  pl.BlockSpec((BLOCK, Q_COLS), lambda b, n: (row(b, n), q_blk)),
                  pl.BlockSpec((BLOCK, 2 * KV_COLS), lambda b, n: (row(b, n), kv_blk)),
                  pl.BlockSpec((BLOCK, 2 * KV_COLS), lambda b, n: (row(b, jnp.maximum(n - 1, 0)), kv_blk)),
                  pl.BlockSpec((BLOCK, 2 * KV_COLS), lambda b, n: (row(b, 0), kv_blk)),
                  pl.BlockSpec((1, LANES), lambda b, n: (0, 0)),
                  pl.BlockSpec((1, LANES), lambda b, n: (0, 0))],
        out_specs=pl.BlockSpec((BLOCK, Q_COLS), lambda b, n: (row(b, n), 0)),
        out_shape=jax.ShapeDtypeStruct((T, Q_COLS), BF16),
        compiler_params=_params("arbitrary", "arbitrary"),
        name="swa_attention",
    )(sinks, zin, zin, zin, zin, jnp.tile(q_norm_w, LANES // HEAD_DIM).reshape(1, LANES),
      jnp.tile(k_norm_w, LANES // HEAD_DIM).reshape(1, LANES))


def _pair_blockdiag(x):
    xb = x.astype(BF16)
    lane = lax.broadcasted_iota(jnp.int32, xb.shape, 2)
    zero = jnp.zeros_like(xb)
    return jnp.concatenate([jnp.where(lane < RWKV_HEAD, xb, zero), jnp.where(lane >= RWKV_HEAD, xb, zero)], axis=1)


def _bmm(a, b_bf16):
    return lax.dot_general(a.astype(BF16), b_bf16, (((2,), (1,)), ((0,), (0,))), preferred_element_type=F32)


def _bmm_nt(a, b_bf16):
    return lax.dot_general(a.astype(BF16), b_bf16, (((2,), (2,)), ((0,), (0,))), preferred_element_type=F32)


def _pair_pick(full):
    lane = lax.broadcasted_iota(jnp.int32, (full.shape[0], RWKV_HEAD, LANES), 2)
    return jnp.where(lane < RWKV_HEAD, full[:, :RWKV_HEAD], full[:, RWKV_HEAD:])


def _to_pairs(x, nchunk):
    L = RWKV_CHUNK
    return jnp.stack([x[c * L:(c + 1) * L, p * LANES:(p + 1) * LANES]
                      for c in range(nchunk) for p in range(x.shape[1] // LANES)], axis=0)


def _pair_segsum(*xs):
    npair = xs[0].shape[1] // LANES
    rows = xs[0].shape[0]
    stacked = jnp.concatenate([x[:, p * LANES:(p + 1) * LANES] for x in xs for p in range(npair)], axis=0)
    s = _half_sums(stacked)
    outs = [jnp.concatenate([s[(i * npair + p) * rows:(i * npair + p + 1) * rows] for p in range(npair)], axis=-1)
            for i in range(len(xs))]
    return outs[0] if len(xs) == 1 else outs


def _rwkv_kernel(rkv_a_ref, rkv_b_ref, lora_ref, mu_rkv_ref, mu_lora_ref, wup_wa_ref, wup_g_ref, vec_ref, o_ref,
                 s_scr, carry_rkv, carry_lora, *, tb):
    n = pl.program_id(1)
    L = RWKV_CHUNK
    npair = RWKV_DIM // LANES

    @pl.when(n == 0)
    def _():
        s_scr[...] = jnp.zeros(s_scr.shape, F32)
        carry_rkv[...] = jnp.zeros(carry_rkv.shape, F32)
        carry_lora[...] = jnp.zeros(carry_lora.shape, F32)

    rows = lax.broadcasted_iota(jnp.int32, (tb, 1), 0)
    valid = (n > 0) | (rows >= N_PAD)

    def token_shift(z, carry, mu_ref):
        prev = jnp.where(rows == 0, carry[SUBLANES - 1:SUBLANES], pltpu.roll(z, 1, axis=0))
        carry[...] = z[tb - SUBLANES:]
        return jnp.where(valid, z + (prev - z) * mu_ref[...], 0.0)

    zr = token_shift(jnp.concatenate([rkv_a_ref[...], rkv_b_ref[...]], axis=1), carry_rkv, mu_rkv_ref)
    zl = token_shift(lora_ref[...], carry_lora, mu_lora_ref)
    r, k, v = zr[:, :RWKV_DIM], zr[:, RWKV_DIM:2 * RWKV_DIM], zr[:, 2 * RWKV_DIM:]

    z_wa = zl[:, :LANES]
    col = lax.broadcasted_iota(jnp.int32, z_wa.shape, 1)
    act_wa = jnp.where(col < DECAY_LORA, jnp.tanh(z_wa), z_wa)
    up_wa = jnp.dot(act_wa.astype(BF16), wup_wa_ref[...], preferred_element_type=F32)
    g = jnp.dot(_sigmoid(zl[:, LANES:]).astype(BF16), wup_g_ref[...], preferred_element_type=F32)
    vec = vec_ref[...]
    w0, a0, k_k, k_a, r_k, ln_w, ln_b = (vec[i:i + 1] for i in range(7))
    w_log = -jnp.exp(-_softplus(-(w0 + up_wa[:, :RWKV_DIM])) - 0.5)
    w_log = jnp.where(valid, w_log, 0.0)
    a = _sigmoid(a0 + up_wa[:, RWKV_DIM:])

    kk = k * k_k
    k2 = k * (1.0 + (a - 1.0) * k_a)
    kk_sq, bonus = _pair_segsum(kk * kk, r * k2 * r_k)
    kk = kk / jnp.maximum(jnp.sqrt(kk_sq), 1e-12)
    kka = kk * a

    nchunk = tb // L
    nb = nchunk * npair
    t_i = lax.broadcasted_iota(jnp.int32, (nb, L, LANES), 1)
    s_i = lax.broadcasted_iota(jnp.int32, (nb, L, LANES), 2) & (RWKV_HEAD - 1)
    strict = s_i < t_i
    incl = s_i <= t_i
    same16 = (t_i >> 4) == (s_i >> 4)
    same32 = (t_i >> 5) == (s_i >> 5)
    eye = jnp.where(s_i == t_i, 1.0, 0.0)
    ti2 = lax.broadcasted_iota(jnp.int32, (tb, tb), 0)
    si2 = lax.broadcasted_iota(jnp.int32, (tb, tb), 1)
    lshift = L.bit_length() - 1
    tri_incl = jnp.where((si2 <= ti2) & ((si2 >> lshift) == (ti2 >> lshift)), 1.0, 0.0).astype(BF16)
    cum = _dot_exact_rhs(tri_incl, w_log)
    cum_end = jnp.concatenate(
        [jnp.broadcast_to(cum[(c + 1) * L - 1:(c + 1) * L], (L, RWKV_DIM)) for c in range(nchunk)], axis=0)
    e_w = jnp.exp(cum)
    e_iw = jnp.exp(-cum)
    e_prev = jnp.exp(cum - w_log)
    e_end = jnp.exp(cum_end - cum)
    w_end = jnp.exp(cum_end)
    ah = _to_pairs(-kk * e_prev, nchunk)
    bh = _to_pairs(kka * e_iw, nchunk)
    kh = _to_pairs(k2 * e_iw, nchunk)
    rh = _to_pairs(r * e_w, nchunk)
    bt = _to_pairs(kka * e_end, nchunk)
    kt = _to_pairs(k2 * e_end, nchunk)
    vp = _to_pairs(v, nchunk)
    vbd = _pair_blockdiag(vp)

    ar = jnp.concatenate([ah, rh], axis=1)
    m = _bmm_nt(ar, jnp.concatenate([_pair_blockdiag(bh), _pair_blockdiag(kh)], axis=1))
    a_ab = jnp.where(strict, m[:, :L, :LANES], 0.0)
    a_rb = jnp.where(incl, m[:, L:, :LANES], 0.0)
    a_ak = jnp.where(strict, m[:, :L, LANES:], 0.0)
    a_rk = jnp.where(incl, m[:, L:, LANES:], 0.0)
    kv = _bmm(jnp.concatenate([a_ak, a_rk], axis=1), vbd)
    akv, y_rk = kv[:, :L], kv[:, L:]
    a1 = jnp.where(same16, a_ab, 0.0)
    t1 = eye + a1
    a2 = _bmm(a1, _pair_blockdiag(a1))
    x = _bmm(jnp.concatenate([a2, t1], axis=1), _pair_blockdiag(a2))
    a4, t2 = x[:, :L], t1 + x[:, L:]
    x = _bmm(jnp.concatenate([a4, t2], axis=1), _pair_blockdiag(a4))
    a8, t3 = x[:, :L], t2 + x[:, L:]
    t16 = t3 + _bmm(t3, _pair_blockdiag(a8))
    off32 = jnp.where(same32 & jnp.logical_not(same16), a_ab, 0.0)
    t32 = t16 + _bmm(t16, _pair_blockdiag(_bmm(off32, _pair_blockdiag(t16))))
    off64 = jnp.where(same32, 0.0, a_ab)
    tinv = t32 + _bmm(t32, _pair_blockdiag(_bmm(off64, _pair_blockdiag(t32))))
    tg = _bmm(tinv, jnp.concatenate([_pair_blockdiag(ah), _pair_blockdiag(akv)], axis=2))
    ta, gm = tg[:, :, :LANES], tg[:, :, LANES:]
    rg = _bmm(a_rb, jnp.concatenate([_pair_blockdiag(ta), _pair_blockdiag(gm)], axis=2))
    rt = rh + rg[:, :, :LANES]
    y_intra = rg[:, :, LANES:] + y_rk
    tgv_t = jnp.swapaxes(jnp.concatenate([ta, gm, vp], axis=2), 1, 2)
    pp = _bmm(tgv_t[:, :2 * LANES], bt.astype(BF16))
    phi = _pair_pick(pp[:, :LANES])
    psi = _pair_pick(pp[:, LANES:]) + _pair_pick(_bmm(tgv_t[:, 2 * LANES:], kt.astype(BF16)))
    w_end_p = _to_pairs(w_end, nchunk)[:, :1]

    y_chunks = []
    s = s_scr[...]
    for c in range(nchunk):
        cs = slice(c * npair, (c + 1) * npair)
        y_c = _bmm_nt(rt[cs], _pair_blockdiag(s)) + y_intra[cs]
        s = s * w_end_p[cs] + _bmm(s, _pair_blockdiag(phi[cs])) + psi[cs]
        y_chunks.append(jnp.concatenate([y_c[p] for p in range(npair)], axis=-1))
    s_scr[...] = s
    y = jnp.concatenate(y_chunks, axis=0)

    inv_n = 1.0 / RWKV_HEAD
    mu = _pair_segsum(y) * inv_n
    d = y - mu
    var = _pair_segsum(d * d) * inv_n
    yn = d * lax.rsqrt(var + RWKV_GN_EPS) * ln_w + ln_b
    o_ref[...] = ((yn + bonus * v) * g).astype(o_ref.dtype)


def _rwkv(zin, mu_rkv, mu_lora, wup_wa, wup_g, vecs, *, nbatch, nblk):
    T = zin.shape[0]
    tb = BLOCK
    half = 3 * RWKV_DIM // 2
    assert ATTN_COLS == half and (ATTN_COLS + 3 * RWKV_DIM) % LORA_PAD == 0
    lora_blk = (ATTN_COLS + 3 * RWKV_DIM) // LORA_PAD
    row = lambda b, n: b * nblk + n
    return pl.pallas_call(
        functools.partial(_rwkv_kernel, tb=tb),
        grid=(nbatch, nblk),
        in_specs=[pl.BlockSpec((tb, half), lambda b, n: (row(b, n), 1)),
                  pl.BlockSpec((tb, half), lambda b, n: (row(b, n), 2)),
                  pl.BlockSpec((tb, LORA_PAD), lambda b, n: (row(b, n), lora_blk)),
                  pl.BlockSpec((1, 3 * RWKV_DIM), lambda b, n: (0, 0)),
                  pl.BlockSpec((1, LORA_PAD), lambda b, n: (0, 0)),
                  pl.BlockSpec((LANES, 2 * RWKV_DIM), lambda b, n: (0, 0)),
                  pl.BlockSpec((LORA_PAD - LANES, RWKV_DIM), lambda b, n: (0, 0)),
                  pl.BlockSpec((SUBLANES, RWKV_DIM), lambda b, n: (0, 0))],
        out_specs=pl.BlockSpec((tb, RWKV_DIM), lambda b, n: (row(b, n), 0)),
        out_shape=jax.ShapeDtypeStruct((T, RWKV_DIM), BF16),
        scratch_shapes=[pltpu.VMEM((RWKV_DIM // LANES, RWKV_HEAD, LANES), F32),
                        pltpu.VMEM((SUBLANES, 3 * RWKV_DIM), F32),
                        pltpu.VMEM((SUBLANES, LORA_PAD), F32)],
        compiler_params=_params("arbitrary", "arbitrary"),
        name="rwkv7",
    )(zin, zin, zin, mu_rkv, mu_lora, wup_wa, wup_g, vecs)


def _ssd_kernel(z_ref, dt_ref, x_ref, b_ref, c_ref, dtb_ref, alog_ref, dskip_ref, nw_ref, exp_ref, o_ref, st_scr):
    n = pl.program_id(1)

    @pl.when(n == 0)
    def _():
        st_scr[...] = jnp.zeros(st_scr.shape, F32)

    rows = lax.broadcasted_iota(jnp.int32, (BLOCK, 1), 0)
    valid = (n > 0) | (rows >= N_PAD)
    x = x_ref[...]
    bm = b_ref[...]
    cm = c_ref[...]

    dt = jnp.where(valid, _softplus(dt_ref[...] + dtb_ref[...]), 0.0)
    adt = dt * (-jnp.exp(alog_ref[...]))
    ti = lax.broadcasted_iota(jnp.int32, (BLOCK, BLOCK), 0)
    si = lax.broadcasted_iota(jnp.int32, (BLOCK, BLOCK), 1)
    causal = si <= ti
    cum = _dot_exact_rhs(jnp.where(causal, 1.0, 0.0).astype(BF16), adt)
    cum_t = cum.T
    ecum = jnp.exp(cum)
    dt_end = dt * jnp.exp(cum[BLOCK - 1:BLOCK] - cum)
    e_hi = ecum.astype(BF16)
    e_lo = (ecum - e_hi.astype(F32)).astype(BF16)
    ecum_x = jnp.dot(jnp.concatenate([e_hi, e_lo], axis=1), exp_ref[...], preferred_element_type=F32)
    ex = jnp.dot(jnp.concatenate([dt, dt_end], axis=0).astype(BF16), exp_ref[:SSD_DT_PAD],
                 preferred_element_type=F32)
    xdt = x * ex[:BLOCK]
    xend = x * ex[BLOCK:]
    lane = lax.broadcasted_iota(jnp.int32, (BLOCK, LANES), 1)
    lo_half = lane < SSD_HEAD_DIM

    y_groups = []
    for g in range(SSD_GROUPS):
        gs = slice(g * SSD_GROUP_COLS, (g + 1) * SSD_GROUP_COLS)
        bg = bm[:, g * SSD_STATE:(g + 1) * SSD_STATE]
        cg = cm[:, g * SSD_STATE:(g + 1) * SSD_STATE]
        cb = _dot_nt(cg, bg)
        st = st_scr[g]
        y_inter = _dot(cg, st) * ecum_x[:, gs]
        y_pairs = []
        for q in range(SSD_HPG // 2):
            mats = []
            for j in (g * SSD_HPG + 2 * q, g * SSD_HPG + 2 * q + 1):
                seg = cum[:, j:j + 1] - cum_t[j:j + 1, :]
                mats.append(cb * jnp.exp(jnp.where(causal, seg, -jnp.inf)))
            ll = jnp.concatenate(mats, axis=1).astype(BF16)
            c0 = g * SSD_GROUP_COLS + q * LANES
            xp = xdt[:, c0:c0 + LANES]
            bdx = jnp.concatenate([jnp.where(lo_half, xp, 0.0), jnp.where(lo_half, 0.0, xp)], axis=0)
            y_pairs.append(jnp.dot(ll, bdx.astype(BF16), preferred_element_type=F32))
        y_intra = jnp.concatenate(y_pairs, axis=-1)
        st_scr[g] = st * ecum_x[BLOCK - 1:BLOCK, gs] + _dot_tn(bg, xend[:, gs])
        y_groups.append(y_intra + y_inter + dskip_ref[:, gs] * x[:, gs])
    z = z_ref[...]
    nw = nw_ref[...]
    outs = []
    for g in range(SSD_GROUPS):
        gs = slice(g * SSD_GROUP_COLS, (g + 1) * SSD_GROUP_COLS)
        zg = z[:, gs]
        yg = y_groups[g] * (zg * _sigmoid(zg))
        yg = yg * lax.rsqrt(jnp.mean(yg * yg, axis=-1, keepdims=True) + SSD_NORM_EPS)
        outs.append((yg * nw[:, gs]).astype(o_ref.dtype))
    o_ref[...] = jnp.concatenate(outs, axis=-1)


def _ssd(zdt, xbc, dt_bias, a_log, d_skip, norm_w, *, nbatch, nblk):
    T = zdt.shape[0]
    row = lambda b, n: b * nblk + n
    pad_h = lambda t: jnp.pad(t.astype(F32), (0, SSD_DT_PAD - SSD_HEADS)).reshape(1, SSD_DT_PAD)
    head = lax.broadcasted_iota(jnp.int32, (SSD_DT_PAD, SSD_INNER), 0)
    colh = lax.broadcasted_iota(jnp.int32, (SSD_DT_PAD, SSD_INNER), 1) // SSD_HEAD_DIM
    expand = jnp.tile((head == colh).astype(BF16), (2, 1))
    full = lambda shape: pl.BlockSpec(shape, lambda b, n: (0, 0))
    b_blk = SSD_INNER // SSD_BC_COLS
    dt_blk = SSD_INNER // SSD_DT_PAD
    return pl.pallas_call(
        _ssd_kernel,
        grid=(nbatch, nblk),
        in_specs=[pl.BlockSpec((BLOCK, SSD_INNER), lambda b, n: (row(b, n), 0)),
                  pl.BlockSpec((BLOCK, SSD_DT_PAD), lambda b, n: (row(b, n), dt_blk)),
                  pl.BlockSpec((BLOCK, SSD_INNER), lambda b, n: (row(b, n), 0)),
                  pl.BlockSpec((BLOCK, SSD_BC_COLS), lambda b, n: (row(b, n), b_blk)),
                  pl.BlockSpec((BLOCK, SSD_BC_COLS), lambda b, n: (row(b, n), b_blk + 1)),
                  full((1, SSD_DT_PAD)), full((1, SSD_DT_PAD)), full((1, SSD_INNER)), full((1, SSD_INNER)),
                  full((2 * SSD_DT_PAD, SSD_INNER))],
        out_specs=pl.BlockSpec((BLOCK, SSD_INNER), lambda b, n: (row(b, n), 0)),
        out_shape=jax.ShapeDtypeStruct((T, SSD_INNER), BF16),
        scratch_shapes=[pltpu.VMEM((SSD_GROUPS, SSD_STATE, SSD_GROUP_COLS), F32)],
        compiler_params=_params("arbitrary", "arbitrary"),
        name="ssd",
    )(zdt, zdt, xbc, xbc, xbc, pad_h(dt_bias), pad_h(a_log),
      jnp.repeat(d_skip.astype(F32), SSD_HEAD_DIM).reshape(1, SSD_INNER), norm_w.reshape(1, SSD_INNER), expand)


def _pack_ar_w_in(w):
    return jnp.pad(w, ((0, 0), (0, LORA_PAD - LORA_COLS))).astype(BF16)


def _pack_lora_up(w_up, a_up, g_up):
    assert DECAY_LORA + AAA_LORA == LANES
    zero = jnp.zeros((DECAY_LORA, RWKV_DIM), F32)
    wup_wa = jnp.concatenate([jnp.concatenate([w_up, zero], axis=1), jnp.concatenate([zero, a_up], axis=1)], axis=0)
    wup_g = jnp.pad(g_up, ((0, LORA_PAD - LANES - GATE_LORA), (0, 0)))
    return wup_wa.astype(BF16), wup_g.astype(BF16)


def kernel(x, meta_tokens, mix_norm_w, ffn_norm_w, ar_w_in, ar_shift_mu, attn_q_norm_w, attn_k_norm_w, attn_sinks, rwkv_w0, rwkv_w_up, rwkv_a0, rwkv_a_up, rwkv_g_up, rwkv_k_k, rwkv_k_a, rwkv_r_k, rwkv_ln_w, rwkv_ln_b, ar_w_out, ssd_w_in, ssd_conv_w, ssd_conv_b, ssd_dt_bias, ssd_a_log, ssd_d, ssd_norm_w, ssd_w_out, ffn_w_up, ffn_conv_w, ffn_conv_b, ffn_w_down):
    nbatch, seq, d = x.shape
    assert d == D_MODEL and seq % BLOCK == 0
    seq_p = N_PAD + N_META + seq
    nblk = seq_p // BLOCK
    depth = mix_norm_w.shape[0]
    res = jnp.concatenate([
        jnp.zeros((nbatch, N_PAD, d), x.dtype),
        jnp.broadcast_to(meta_tokens.astype(x.dtype)[None], (nbatch, N_META, d)),
        x], axis=1).reshape(nbatch * seq_p, d)
    kw = dict(seq_p=seq_p, nbatch=nbatch)
    for layer in range(depth):
        i = layer // 2
        if layer % 2 == 0:
            zin = _norm_matmul(res, mix_norm_w[layer], _pack_ar_w_in(ar_w_in[i]), tn=AR_COLS // 3, **kw)
            attn = _attention(zin, attn_sinks[i].astype(F32), attn_q_norm_w[i], attn_k_norm_w[i],
                              nbatch=nbatch, nblk=nblk)
            mu = ar_shift_mu[i]
            mu_rkv = mu[:3 * RWKV_DIM].reshape(1, -1)
            mu_lora = jnp.pad(mu[3 * RWKV_DIM:], (0, LORA_PAD - LORA_COLS)).reshape(1, -1)
            vecs = jnp.stack([rwkv_w0[i], rwkv_a0[i], rwkv_k_k[i], rwkv_k_a[i], rwkv_r_k[i].reshape(-1),
                              rwkv_ln_w[i], rwkv_ln_b[i], jnp.zeros((RWKV_DIM,), F32)]).astype(F32)
            tm = _rwkv(zin, mu_rkv, mu_lora, *_pack_lora_up(rwkv_w_up[i], rwkv_a_up[i], rwkv_g_up[i]), vecs,
                       nbatch=nbatch, nblk=nblk)
            res = _proj_residual([attn, tm], ar_w_out[i].astype(BF16), res)
        else:
            w_in = ssd_w_in[i]
            xbc0, dt0 = SSD_INNER, 2 * SSD_INNER + 2 * SSD_BC_COLS
            w_zdt = jnp.concatenate([w_in[:, :xbc0], w_in[:, dt0:],
                                     jnp.zeros((d, SSD_DT_PAD - SSD_HEADS), w_in.dtype)], axis=1).astype(BF16)
            zdt = _norm_matmul(res, mix_norm_w[layer], w_zdt, tn=(SSD_INNER + SSD_DT_PAD) // 3, **kw)
            xbc = _norm_matmul_conv(res, mix_norm_w[layer], w_in[:, xbc0:dt0].astype(BF16), ssd_conv_w[i],
                                    ssd_conv_b[i], tn=(dt0 - xbc0) // 4, **kw)
            y = _ssd(zdt, xbc, ssd_dt_bias[i], ssd_a_log[i], ssd_d[i], ssd_norm_w[i], nbatch=nbatch, nblk=nblk)
            res = _proj_residual([y], ssd_w_out[i].astype(BF16), res)
        res = _ffn(res, ffn_norm_w[layer], ffn_w_up[layer].astype(BF16), ffn_conv_w[layer], ffn_conv_b[layer],
                   ffn_w_down[layer].astype(BF16), **kw)
    return res.reshape(nbatch, seq_p, d)[:, N_PAD + N_META:]
```

```python
import functools

import jax
import jax.numpy as jnp
from jax import lax
from jax.experimental import pallas as pl
from jax.experimental.pallas import tpu as pltpu

F32 = jnp.float32
BF16 = jnp.bfloat16

D_MODEL = 2048
N_META = 16
BLOCK = 128
N_PAD = BLOCK - N_META
NORM_EPS = 1e-6

HEAD_DIM = 64
ATTN_HEADS = 16
ATTN_KV_HEADS = 4
ATTN_GROUP = 4
WINDOW = 128
QK_EPS = 1e-6
Q_COLS = ATTN_HEADS * HEAD_DIM
KV_COLS = ATTN_KV_HEADS * HEAD_DIM
ATTN_COLS = Q_COLS + 2 * KV_COLS

RWKV_DIM = 1024
RWKV_HEAD = 64
DECAY_LORA = 64
AAA_LORA = 64
GATE_LORA = 160
LORA_COLS = DECAY_LORA + AAA_LORA + GATE_LORA
LORA_PAD = 384
RWKV_GN_EPS = 64e-5
RWKV_CHUNK = 64
AR_COLS = ATTN_COLS + 3 * RWKV_DIM + LORA_PAD

SSD_INNER = 4096
SSD_HEAD_DIM = 64
SSD_HEADS = 64
SSD_GROUPS = 8
SSD_HPG = 8
SSD_STATE = 128
SSD_CONV = 4
SSD_GROUP_COLS = SSD_INNER // SSD_GROUPS
SSD_BC_COLS = SSD_GROUPS * SSD_STATE
SSD_DT_PAD = 128
SSD_NORM_EPS = 1e-5

FFN_DIM = 5632
FFN_CONV = 3
FFN_CHUNK = 512
CONV_SLAB = 256

LANES = 128
SUBLANES = 8
VMEM_LIMIT = 56 * 1024 * 1024


def _params(*sem):
    return pltpu.CompilerParams(dimension_semantics=sem, vmem_limit_bytes=VMEM_LIMIT)


def _row_tile(total, target):
    best = None
    for t in range(16, min(total, target) + 1, 16):
        if total % t == 0:
            best = t
    assert best is not None, (total, target)
    return best


def _dot(a, b):
    return jnp.dot(a.astype(BF16), b.astype(BF16), preferred_element_type=F32)


def _dot_nt(a, b):
    return lax.dot_general(a.astype(BF16), b.astype(BF16), (((1,), (1,)), ((), ())),
                           preferred_element_type=F32)


def _dot_tn(a, b):
    return lax.dot_general(a.astype(BF16), b.astype(BF16), (((0,), (0,)), ((), ())),
                           preferred_element_type=F32)


def _split3(x):
    hi = x.astype(BF16)
    r1 = x - hi.astype(F32)
    mid = r1.astype(BF16)
    lo = (r1 - mid.astype(F32)).astype(BF16)
    return hi, mid, lo


def _dot_exact_rhs(sel_bf16, x):
    return jnp.dot(jnp.concatenate([sel_bf16] * 3, axis=1), jnp.concatenate(_split3(x), axis=0),
                   preferred_element_type=F32)


def _half_sums(x):
    hi = x.astype(BF16)
    lo = (x - hi.astype(F32)).astype(BF16)
    i = lax.broadcasted_iota(jnp.int32, (2 * LANES, LANES), 0) & (LANES - 1)
    j = lax.broadcasted_iota(jnp.int32, (2 * LANES, LANES), 1)
    ones2 = jnp.where((i < LANES // 2) == (j < LANES // 2), 1.0, 0.0).astype(BF16)
    return jnp.dot(jnp.concatenate([hi, lo], axis=1), ones2, preferred_element_type=F32)


def _sigmoid(x):
    return 1.0 / (1.0 + jnp.exp(-x))


def _softplus(x):
    return jnp.maximum(x, 0.0) + jnp.log(1.0 + jnp.exp(-jnp.abs(x)))


def _valid_rows(row0, nrows, seq_p, nbatch):
    rows = row0 + lax.broadcasted_iota(jnp.int32, (nrows, 1), 0)
    pos = rows
    for b in range(1, nbatch):
        pos = jnp.where(rows >= b * seq_p, rows - b * seq_p, pos)
    return pos >= N_PAD


def _masked_rmsnorm(x, w, row0, seq_p, nbatch):
    ms = jnp.mean(x * x, axis=-1, keepdims=True)
    y = x * lax.rsqrt(ms + NORM_EPS) * w
    return jnp.where(_valid_rows(row0, x.shape[0], seq_p, nbatch), y, 0.0)


def _norm_mm_kernel(x_ref, nw_ref, w_ref, o_ref, *, tm, seq_p, nbatch):
    h = _masked_rmsnorm(x_ref[...], nw_ref[...], pl.program_id(0) * tm, seq_p, nbatch)
    o_ref[...] = jnp.dot(h.astype(BF16), w_ref[...], preferred_element_type=F32)


def _norm_matmul(res, norm_w, w_bf16, *, seq_p, nbatch):
    T, D = res.shape
    N = w_bf16.shape[1]
    tm = _row_tile(T, 384)
    return pl.pallas_call(
        functools.partial(_norm_mm_kernel, tm=tm, seq_p=seq_p, nbatch=nbatch),
        grid=(T // tm,),
        in_specs=[pl.BlockSpec((tm, D), lambda m: (m, 0)),
                  pl.BlockSpec((1, D), lambda m: (0, 0)),
                  pl.BlockSpec((D, N), lambda m: (0, 0), pipeline_mode=pl.Buffered(1))],
        out_specs=pl.BlockSpec((tm, N), lambda m: (m, 0)),
        out_shape=jax.ShapeDtypeStruct((T, N), F32),
        compiler_params=_params("arbitrary"),
        name="norm_in_proj",
    )(res, norm_w.reshape(1, D), w_bf16)


def _norm_mm_conv_kernel(x_ref, nw_ref, w_ref, cw_ref, cb_ref, o_ref, h_scr, carry_scr, *, tm, seq_p, nbatch):
    m = pl.program_id(0)
    n = pl.program_id(1)

    @pl.when(n == 0)
    def _():
        h = _masked_rmsnorm(x_ref[...], nw_ref[...], m * tm, seq_p, nbatch)
        h_scr[...] = h.astype(BF16)

    @pl.when(m == 0)
    def _():
        carry_scr[n] = jnp.zeros(carry_scr.shape[1:], F32)

    valid = _valid_rows(m * tm, tm, seq_p, nbatch)
    h = h_scr[...]
    for s0 in range(0, o_ref.shape[1], CONV_SLAB):
        sl = slice(s0, s0 + CONV_SLAB)
        xc = jnp.dot(h, w_ref[:, sl], preferred_element_type=F32)
        full = jnp.concatenate([carry_scr[n, :, sl], xc], axis=0)
        carry_scr[n, :, sl] = xc[tm - SUBLANES:]
        acc = cb_ref[:, sl]
        for j in range(SSD_CONV):
            k = SSD_CONV - 1 - j
            term = xc if k == 0 else pltpu.roll(full, k, axis=0)[SUBLANES:]
            acc = acc + cw_ref[j:j + 1, sl] * term
        o_ref[:, sl] = jnp.where(valid, acc * _sigmoid(acc), 0.0)


def _norm_matmul_conv(res, norm_w, w_bf16, conv_w, conv_b, *, tn, seq_p, nbatch):
    T, D = res.shape
    N = w_bf16.shape[1]
    tm = _row_tile(T, 768)
    assert N % tn == 0
    return pl.pallas_call(
        functools.partial(_norm_mm_conv_kernel, tm=tm, seq_p=seq_p, nbatch=nbatch),
        grid=(T // tm, N // tn),
        in_specs=[pl.BlockSpec((tm, D), lambda m, n: (m, 0)),
                  pl.BlockSpec((1, D), lambda m, n: (0, 0)),
                  pl.BlockSpec((D, tn), lambda m, n: (0, n)),
                  pl.BlockSpec((SSD_CONV, tn), lambda m, n: (0, n)),
                  pl.BlockSpec((1, tn), lambda m, n: (0, n))],
        out_specs=pl.BlockSpec((tm, tn), lambda m, n: (m, n)),
        out_shape=jax.ShapeDtypeStruct((T, N), F32),
        scratch_shapes=[pltpu.VMEM((tm, D), BF16), pltpu.VMEM((N // tn, SUBLANES, tn), F32)],
        compiler_params=_params("arbitrary", "arbitrary"),
        name="norm_in_proj_conv",
    )(res, norm_w.reshape(1, D), w_bf16, conv_w, conv_b.reshape(1, N))


def _proj_res_kernel(*refs, n_lhs):
    lhs = refs[:n_lhs]
    ws = refs[n_lhs:2 * n_lhs]
    res_ref, o_ref = refs[2 * n_lhs], refs[2 * n_lhs + 1]
    acc = res_ref[...]
    for l, w in zip(lhs, ws):
        acc = acc + jnp.dot(l[...], w[...], preferred_element_type=F32)
    o_ref[...] = acc


def _proj_residual(lhs_list, w_bf16, res):
    T, D = res.shape
    n_lhs = len(lhs_list)
    kw = lhs_list[0].shape[1]
    assert all(l.shape == (T, kw) for l in lhs_list) and w_bf16.shape == (n_lhs * kw, D)
    tm = _row_tile(T, 384)
    in_specs = [pl.BlockSpec((tm, kw), lambda m: (m, 0)) for _ in range(n_lhs)]
    in_specs += [pl.BlockSpec((kw, D), functools.partial(lambda i, m: (i, 0), i), pipeline_mode=pl.Buffered(1))
                 for i in range(n_lhs)]
    in_specs += [pl.BlockSpec((tm, D), lambda m: (m, 0))]
    return pl.pallas_call(
        functools.partial(_proj_res_kernel, n_lhs=n_lhs),
        grid=(T // tm,),
        in_specs=in_specs,
        out_specs=pl.BlockSpec((tm, D), lambda m: (m, 0)),
        out_shape=jax.ShapeDtypeStruct((T, D), F32),
        compiler_params=_params("arbitrary"),
        name="out_proj_residual",
    )(*lhs_list, *([w_bf16] * n_lhs), res)


def _ffn_kernel(res_ref, nw_ref, wg_ref, wv_ref, cw_ref, cb_ref, wd_ref, o_ref, h_scr, carry_scr,
                *, tm, seq_p, nbatch):
    m = pl.program_id(0)
    f = pl.program_id(1)

    @pl.when(f == 0)
    def _():
        res = res_ref[...]
        h_scr[...] = _masked_rmsnorm(res, nw_ref[...], m * tm, seq_p, nbatch).astype(BF16)
        o_ref[...] = res

    @pl.when(m == 0)
    def _():
        carry_scr[f] = jnp.zeros(carry_scr.shape[1:], F32)

    h = h_scr[...]
    gate = jnp.dot(h, wg_ref[...], preferred_element_type=F32)
    val = jnp.dot(h, wv_ref[...], preferred_element_type=F32)
    prev = carry_scr[f]
    rows = lax.broadcasted_iota(jnp.int32, (tm, 1), 0)
    g1 = jnp.where(rows == 0, prev[7:8], pltpu.roll(gate, 1, axis=0))
    g2 = jnp.where(rows == 0, prev[6:7], jnp.where(rows == 1, prev[7:8], pltpu.roll(gate, 2, axis=0)))
    carry_scr[f] = gate[tm - SUBLANES:]
    cw = cw_ref[...]
    pre = cb_ref[...] + cw[0:1] * g2 + cw[1:2] * g1 + cw[2:3] * gate
    act = pre * _sigmoid(pre) * val
    o_ref[...] += jnp.dot(act.astype(BF16), wd_ref[...], preferred_element_type=F32)


def _ffn(res, norm_w, w_up_bf16, conv_w, conv_b, w_down_bf16, *, seq_p, nbatch):
    T, D = res.shape
    tm = _row_tile(T, 768)
    tf = FFN_CHUNK
    nf = FFN_DIM // tf
    return pl.pallas_call(
        functools.partial(_ffn_kernel, tm=tm, seq_p=seq_p, nbatch=nbatch),
        grid=(T // tm, nf),
        in_specs=[pl.BlockSpec((tm, D), lambda m, f: (m, 0)),
                  pl.BlockSpec((1, D), lambda m, f: (0, 0)),
                  pl.BlockSpec((D, tf), lambda m, f: (0, f)),
                  pl.BlockSpec((D, tf), lambda m, f: (0, nf + f)),
                  pl.BlockSpec((FFN_CONV, tf), lambda m, f: (0, f)),
                  pl.BlockSpec((1, tf), lambda m, f: (0, f)),
                  pl.BlockSpec((tf, D), lambda m, f: (f, 0))],
        out_specs=pl.BlockSpec((tm, D), lambda m, f: (m, 0)),
        out_shape=jax.ShapeDtypeStruct((T, D), F32),
        scratch_shapes=[pltpu.VMEM((tm, D), BF16), pltpu.VMEM((nf, SUBLANES, tf), F32)],
        compiler_params=_params("arbitrary", "arbitrary"),
        name="ffn",
    )(res, norm_w.reshape(1, D), w_up_bf16, w_up_bf16, conv_w, conv_b.reshape(1, FFN_DIM), w_down_bf16)


def _attn_kernel(sinks_ref, q_ref, kvc_ref, kvp_ref, kvm_ref, qw_ref, kw_ref, o_ref):
    n = pl.program_id(1)
    scale = HEAD_DIM ** -0.5
    lane = lax.broadcasted_iota(jnp.int32, (BLOCK, LANES), 1)
    lo_half = lane < HEAD_DIM

    kvs = (kvp_ref[...], kvc_ref[...], kvm_ref[...])
    q = q_ref[...]
    nq, nk = Q_COLS // LANES, KV_COLS // LANES
    tiles = [q[:, c * LANES:(c + 1) * LANES] for c in range(nq)]
    tiles += [kv[:, c * LANES:(c + 1) * LANES] for kv in kvs for c in range(nk)]
    sq = jnp.concatenate([t * t for t in tiles], axis=0)
    ms = _half_sums(sq) * (1.0 / HEAD_DIM)
    inv = lax.rsqrt(ms + QK_EPS)
    qw = qw_ref[...] * scale
    kw = kw_ref[...]
    qn = [tiles[c] * inv[c * BLOCK:(c + 1) * BLOCK] * qw for c in range(nq)]
    kn = [[tiles[nq + i * nk + c] * inv[(nq + i * nk + c) * BLOCK:(nq + i * nk + c + 1) * BLOCK] * kw
           for i in range(3)] for c in range(nk)]

    qi = lax.broadcasted_iota(jnp.int32, (BLOCK, 3 * BLOCK), 0)
    kj = lax.broadcasted_iota(jnp.int32, (BLOCK, 3 * BLOCK), 1)
    j_prev, j_cur, j_meta = kj, kj - BLOCK, kj - 2 * BLOCK
    first_prev = jnp.where(n == 0, BLOCK, jnp.where(n == 1, N_PAD, 0))
    first_cur = jnp.where(n == 0, N_PAD, 0)
    vis_prev = (kj < BLOCK) & (j_prev > qi) & (j_prev >= first_prev)
    vis_cur = (kj >= BLOCK) & (j_cur <= qi) & (j_cur >= first_cur)
    vis_meta = (j_meta >= N_PAD) & (n * BLOCK + qi - j_meta >= WINDOW)
    bias = jnp.where(vis_prev | vis_cur | vis_meta, 0.0, -jnp.inf)
    bias4 = jnp.concatenate([bias] * ATTN_GROUP, axis=0)
    grow = lax.broadcasted_iota(jnp.int32, (ATTN_GROUP * BLOCK, 1), 0)

    def dup_half(x, upper):
        swapped = pltpu.roll(x, HEAD_DIM, axis=1)
        lane_r = lax.broadcasted_iota(jnp.int32, x.shape, 1) < HEAD_DIM
        return jnp.where(lane_r, swapped, x) if upper else jnp.where(lane_r, x, swapped)

    outs = []
    for c in range(nk):
        k_col = jnp.concatenate(kn[c], axis=0)
        v_col = jnp.concatenate([kv[:, KV_COLS + c * LANES:KV_COLS + (c + 1) * LANES] for kv in kvs], axis=0)
        for half in range(2):
            h = 2 * c + half
            k_dup = dup_half(k_col, half == 1).astype(BF16)
            v_dup = dup_half(v_col, half == 1).astype(BF16)
            lhs = jnp.concatenate(
                [jnp.where(lo_half if e == 0 else jnp.logical_not(lo_half), qn[2 * h + pr], 0.0)
                 for pr in range(2) for e in range(2)], axis=0)
            s = _dot_nt(lhs, k_dup) + bias4
            sink = sinks_ref[ATTN_GROUP * h]
            for g in range(1, ATTN_GROUP):
                sink = jnp.where(grow >= g * BLOCK, sinks_ref[ATTN_GROUP * h + g], sink)
            mx = jnp.maximum(jnp.max(s, axis=-1, keepdims=True), sink)
            p = jnp.exp(s - mx)
            den = jnp.sum(p, axis=-1, keepdims=True) + jnp.exp(sink - mx)
            o = jnp.dot(p.astype(BF16), v_dup, preferred_element_type=F32) / den
            for pr in range(2):
                r0 = 2 * pr * BLOCK
                outs.append(jnp.where(lo_half, o[r0:r0 + BLOCK], o[r0 + BLOCK:r0 + 2 * BLOCK]))
    o_ref[...] = jnp.concatenate(outs, axis=-1).astype(o_ref.dtype)


def _attention(zin, sinks, q_norm_w, k_norm_w, *, nbatch, nblk):
    T = zin.shape[0]
    q_blk = 0
    kv_blk = Q_COLS // (2 * KV_COLS)
    row = lambda b, n: b * nblk + n
    return pl.pallas_call(
        _attn_kernel,
        grid=(nbatch, nblk),
        in_specs=[pl.BlockSpec(memory_space=pltpu.SMEM),
                  pl.BlockSpec((BLOCK, Q_COLS), lambda b, n: (row(b, n), q_blk)),
                  pl.BlockSpec((BLOCK, 2 * KV_COLS), lambda b, n: (row(b, n), kv_blk)),
                  pl.BlockSpec((BLOCK, 2 * KV_COLS), lambda b, n: (row(b, jnp.maximum(n - 1, 0)), kv_blk)),
                  pl.BlockSpec((BLOCK, 2 * KV_COLS), lambda b, n: (row(b, 0), kv_blk)),
                  pl.BlockSpec((1, LANES), lambda b, n: (0, 0)),
                  pl.BlockSpec((1, LANES), lambda b, n: (0, 0))],
        out_specs=pl.BlockSpec((BLOCK, Q_COLS), lambda b, n: (row(b, n), 0)),
        out_shape=jax.ShapeDtypeStruct((T, Q_COLS), BF16),
        compiler_params=_params("arbitrary", "arbitrary"),
        name="swa_attention",
    )(sinks, zin, zin, zin, zin, jnp.tile(q_norm_w, LANES // HEAD_DIM).reshape(1, LANES),
      jnp.tile(k_norm_w, LANES // HEAD_DIM).reshape(1, LANES))


def _pair_blockdiag(x):
    xb = x.astype(BF16)
    lane = lax.broadcasted_iota(jnp.int32, xb.shape, 2)
    zero = jnp.zeros_like(xb)
    return jnp.concatenate([jnp.where(lane < RWKV_HEAD, xb, zero), jnp.where(lane >= RWKV_HEAD, xb, zero)], axis=1)


def _bmm(a, b_bf16):
    return lax.dot_general(a.astype(BF16), b_bf16, (((2,), (1,)), ((0,), (0,))), preferred_element_type=F32)


def _bmm_nt(a, b_bf16):
    return lax.dot_general(a.astype(BF16), b_bf16, (((2,), (2,)), ((0,), (0,))), preferred_element_type=F32)


def _pair_pick(full):
    lane = lax.broadcasted_iota(jnp.int32, (full.shape[0], RWKV_HEAD, LANES), 2)
    return jnp.where(lane < RWKV_HEAD, full[:, :RWKV_HEAD], full[:, RWKV_HEAD:])


def _to_pairs(x, nchunk):
    L = RWKV_CHUNK
    return jnp.stack([x[c * L:(c + 1) * L, p * LANES:(p + 1) * LANES]
                      for c in range(nchunk) for p in range(x.shape[1] // LANES)], axis=0)


def _pair_segsum(*xs):
    npair = xs[0].shape[1] // LANES
    rows = xs[0].shape[0]
    stacked = jnp.concatenate([x[:, p * LANES:(p + 1) * LANES] for x in xs for p in range(npair)], axis=0)
    s = _half_sums(stacked)
    outs = [jnp.concatenate([s[(i * npair + p) * rows:(i * npair + p + 1) * rows] for p in range(npair)], axis=-1)
            for i in range(len(xs))]
    return outs[0] if len(xs) == 1 else outs


def _rwkv_kernel(rkv_a_ref, rkv_b_ref, lora_ref, mu_rkv_ref, mu_lora_ref, wup_wa_ref, wup_g_ref, vec_ref, o_ref,
                 s_scr, carry_rkv, carry_lora, *, tb):
    n = pl.program_id(1)
    L = RWKV_CHUNK
    npair = RWKV_DIM // LANES

    @pl.when(n == 0)
    def _():
        s_scr[...] = jnp.zeros(s_scr.shape, F32)
        carry_rkv[...] = jnp.zeros(carry_rkv.shape, F32)
        carry_lora[...] = jnp.zeros(carry_lora.shape, F32)

    rows = lax.broadcasted_iota(jnp.int32, (tb, 1), 0)
    valid = (n > 0) | (rows >= N_PAD)

    def token_shift(z, carry, mu_ref):
        prev = jnp.where(rows == 0, carry[SUBLANES - 1:SUBLANES], pltpu.roll(z, 1, axis=0))
        carry[...] = z[tb - SUBLANES:]
        return jnp.where(valid, z + (prev - z) * mu_ref[...], 0.0)

    zr = token_shift(jnp.concatenate([rkv_a_ref[...], rkv_b_ref[...]], axis=1), carry_rkv, mu_rkv_ref)
    zl = token_shift(lora_ref[...], carry_lora, mu_lora_ref)
    r, k, v = zr[:, :RWKV_DIM], zr[:, RWKV_DIM:2 * RWKV_DIM], zr[:, 2 * RWKV_DIM:]

    z_wa = zl[:, :LANES]
    col = lax.broadcasted_iota(jnp.int32, z_wa.shape, 1)
    act_wa = jnp.where(col < DECAY_LORA, jnp.tanh(z_wa), z_wa)
    up_wa = jnp.dot(act_wa.astype(BF16), wup_wa_ref[...], preferred_element_type=F32)
    g = jnp.dot(_sigmoid(zl[:, LANES:]).astype(BF16), wup_g_ref[...], preferred_element_type=F32)
    vec = vec_ref[...]
    w0, a0, k_k, k_a, r_k, ln_w, ln_b = (vec[i:i + 1] for i in range(7))
    w_log = -jnp.exp(-_softplus(-(w0 + up_wa[:, :RWKV_DIM])) - 0.5)
    w_log = jnp.where(valid, w_log, 0.0)
    a = _sigmoid(a0 + up_wa[:, RWKV_DIM:])

    kk = k * k_k
    k2 = k * (1.0 + (a - 1.0) * k_a)
    kk_sq, bonus = _pair_segsum(kk * kk, r * k2 * r_k)
    kk = kk / jnp.maximum(jnp.sqrt(kk_sq), 1e-12)
    kka = kk * a

    nchunk = tb // L
    nb = nchunk * npair
    t_i = lax.broadcasted_iota(jnp.int32, (nb, L, LANES), 1)
    s_i = lax.broadcasted_iota(jnp.int32, (nb, L, LANES), 2) & (RWKV_HEAD - 1)
    strict = s_i < t_i
    incl = s_i <= t_i
    same16 = (t_i >> 4) == (s_i >> 4)
    same32 = (t_i >> 5) == (s_i >> 5)
    eye = jnp.where(s_i == t_i, 1.0, 0.0)
    ti2 = lax.broadcasted_iota(jnp.int32, (tb, tb), 0)
    si2 = lax.broadcasted_iota(jnp.int32, (tb, tb), 1)
    lshift = L.bit_length() - 1
    tri_incl = jnp.where((si2 <= ti2) & ((si2 >> lshift) == (ti2 >> lshift)), 1.0, 0.0).astype(BF16)
    cum = _dot_exact_rhs(tri_incl, w_log)
    cum_end = jnp.concatenate(
        [jnp.broadcast_to(cum[(c + 1) * L - 1:(c + 1) * L], (L, RWKV_DIM)) for c in range(nchunk)], axis=0)
    e_w = jnp.exp(cum)
    e_iw = jnp.exp(-cum)
    e_prev = jnp.exp(cum - w_log)
    e_end = jnp.exp(cum_end - cum)
    w_end = jnp.exp(cum_end)
    ah = _to_pairs(-kk * e_prev, nchunk)
    bh = _to_pairs(kka * e_iw, nchunk)
    kh = _to_pairs(k2 * e_iw, nchunk)
    rh = _to_pairs(r * e_w, nchunk)
    bt = _to_pairs(kka * e_end, nchunk)
    kt = _to_pairs(k2 * e_end, nchunk)
    vp = _to_pairs(v, nchunk)
    vbd = _pair_blockdiag(vp)

    ar = jnp.concatenate([ah, rh], axis=1)
    m = _bmm_nt(ar, jnp.concatenate([_pair_blockdiag(bh), _pair_blockdiag(kh)], axis=1))
    a_ab = jnp.where(strict, m[:, :L, :LANES], 0.0)
    a_rb = jnp.where(incl, m[:, L:, :LANES], 0.0)
    a_ak = jnp.where(strict, m[:, :L, LANES:], 0.0)
    a_rk = jnp.where(incl, m[:, L:, LANES:], 0.0)
    kv = _bmm(jnp.concatenate([a_ak, a_rk], axis=1), vbd)
    akv, y_rk = kv[:, :L], kv[:, L:]
    a1 = jnp.where(same16, a_ab, 0.0)
    t1 = eye + a1
    a2 = _bmm(a1, _pair_blockdiag(a1))
    x = _bmm(jnp.concatenate([a2, t1], axis=1), _pair_blockdiag(a2))
    a4, t2 = x[:, :L], t1 + x[:, L:]
    x = _bmm(jnp.concatenate([a4, t2], axis=1), _pair_blockdiag(a4))
    a8, t3 = x[:, :L], t2 + x[:, L:]
    t16 = t3 + _bmm(t3, _pair_blockdiag(a8))
    off32 = jnp.where(same32 & jnp.logical_not(same16), a_ab, 0.0)
    t32 = t16 + _bmm(t16, _pair_blockdiag(_bmm(off32, _pair_blockdiag(t16))))
    off64 = jnp.where(same32, 0.0, a_ab)
    tinv = t32 + _bmm(t32, _pair_blockdiag(_bmm(off64, _pair_blockdiag(t32))))
    tg = _bmm(tinv, jnp.concatenate([_pair_blockdiag(ah), _pair_blockdiag(akv)], axis=2))
    ta, gm = tg[:, :, :LANES], tg[:, :, LANES:]
    rg = _bmm(a_rb, jnp.concatenate([_pair_blockdiag(ta), _pair_blockdiag(gm)], axis=2))
    rt = rh + rg[:, :, :LANES]
    y_intra = rg[:, :, LANES:] + y_rk
    tgv_t = jnp.swapaxes(jnp.concatenate([ta, gm, vp], axis=2), 1, 2)
    pp = _bmm(tgv_t[:, :2 * LANES], bt.astype(BF16))
    phi = _pair_pick(pp[:, :LANES])
    psi = _pair_pick(pp[:, LANES:]) + _pair_pick(_bmm(tgv_t[:, 2 * LANES:], kt.astype(BF16)))
    w_end_p = _to_pairs(w_end, nchunk)[:, :1]

    y_chunks = []
    s = s_scr[...]
    for c in range(nchunk):
        cs = slice(c * npair, (c + 1) * npair)
        y_c = _bmm_nt(rt[cs], _pair_blockdiag(s)) + y_intra[cs]
        s = s * w_end_p[cs] + _bmm(s, _pair_blockdiag(phi[cs])) + psi[cs]
        y_chunks.append(jnp.concatenate([y_c[p] for p in range(npair)], axis=-1))
    s_scr[...] = s
    y = jnp.concatenate(y_chunks, axis=0)

    inv_n = 1.0 / RWKV_HEAD
    mu = _pair_segsum(y) * inv_n
    d = y - mu
    var = _pair_segsum(d * d) * inv_n
    yn = d * lax.rsqrt(var + RWKV_GN_EPS) * ln_w + ln_b
    o_ref[...] = ((yn + bonus * v) * g).astype(o_ref.dtype)


def _rwkv(zin, mu_rkv, mu_lora, wup_wa, wup_g, vecs, *, nbatch, nblk):
    T = zin.shape[0]
    tb = BLOCK
    half = 3 * RWKV_DIM // 2
    assert ATTN_COLS == half and (ATTN_COLS + 3 * RWKV_DIM) % LORA_PAD == 0
    lora_blk = (ATTN_COLS + 3 * RWKV_DIM) // LORA_PAD
    row = lambda b, n: b * nblk + n
    return pl.pallas_call(
        functools.partial(_rwkv_kernel, tb=tb),
        grid=(nbatch, nblk),
        in_specs=[pl.BlockSpec((tb, half), lambda b, n: (row(b, n), 1)),
                  pl.BlockSpec((tb, half), lambda b, n: (row(b, n), 2)),
                  pl.BlockSpec((tb, LORA_PAD), lambda b, n: (row(b, n), lora_blk)),
                  pl.BlockSpec((1, 3 * RWKV_DIM), lambda b, n: (0, 0)),
                  pl.BlockSpec((1, LORA_PAD), lambda b, n: (0, 0)),
                  pl.BlockSpec((LANES, 2 * RWKV_DIM), lambda b, n: (0, 0)),
                  pl.BlockSpec((LORA_PAD - LANES, RWKV_DIM), lambda b, n: (0, 0)),
                  pl.BlockSpec((SUBLANES, RWKV_DIM), lambda b, n: (0, 0))],
        out_specs=pl.BlockSpec((tb, RWKV_DIM), lambda b, n: (row(b, n), 0)),
        out_shape=jax.ShapeDtypeStruct((T, RWKV_DIM), BF16),
        scratch_shapes=[pltpu.VMEM((RWKV_DIM // LANES, RWKV_HEAD, LANES), F32),
                        pltpu.VMEM((SUBLANES, 3 * RWKV_DIM), F32),
                        pltpu.VMEM((SUBLANES, LORA_PAD), F32)],
        compiler_params=_params("arbitrary", "arbitrary"),
        name="rwkv7",
    )(zin, zin, zin, mu_rkv, mu_lora, wup_wa, wup_g, vecs)


def _ssd_kernel(z_ref, dt_ref, x_ref, b_ref, c_ref, dtb_ref, alog_ref, dskip_ref, nw_ref, exp_ref, o_ref, st_scr):
    n = pl.program_id(1)

    @pl.when(n == 0)
    def _():
        st_scr[...] = jnp.zeros(st_scr.shape, F32)

    rows = lax.broadcasted_iota(jnp.int32, (BLOCK, 1), 0)
    valid = (n > 0) | (rows >= N_PAD)
    x = x_ref[...]
    bm = b_ref[...]
    cm = c_ref[...]

    dt = jnp.where(valid, _softplus(dt_ref[...] + dtb_ref[...]), 0.0)
    adt = dt * (-jnp.exp(alog_ref[...]))
    ti = lax.broadcasted_iota(jnp.int32, (BLOCK, BLOCK), 0)
    si = lax.broadcasted_iota(jnp.int32, (BLOCK, BLOCK), 1)
    causal = si <= ti
    cum = _dot_exact_rhs(jnp.where(causal, 1.0, 0.0).astype(BF16), adt)
    cum_t = cum.T
    ecum = jnp.exp(cum)
    dt_end = dt * jnp.exp(cum[BLOCK - 1:BLOCK] - cum)
    e_hi = ecum.astype(BF16)
    e_lo = (ecum - e_hi.astype(F32)).astype(BF16)
    ecum_x = jnp.dot(jnp.concatenate([e_hi, e_lo], axis=1), exp_ref[...], preferred_element_type=F32)
    ex = jnp.dot(jnp.concatenate([dt, dt_end], axis=0).astype(BF16), exp_ref[:SSD_DT_PAD],
                 preferred_element_type=F32)
    xdt = x * ex[:BLOCK]
    xend = x * ex[BLOCK:]
    lane = lax.broadcasted_iota(jnp.int32, (BLOCK, LANES), 1)
    lo_half = lane < SSD_HEAD_DIM

    y_groups = []
    for g in range(SSD_GROUPS):
        gs = slice(g * SSD_GROUP_COLS, (g + 1) * SSD_GROUP_COLS)
        bg = bm[:, g * SSD_STATE:(g + 1) * SSD_STATE]
        cg = cm[:, g * SSD_STATE:(g + 1) * SSD_STATE]
        cb = _dot_nt(cg, bg)
        st = st_scr[g]
        y_inter = _dot(cg, st) * ecum_x[:, gs]
        y_pairs = []
        for q in range(SSD_HPG // 2):
            mats = []
            for j in (g * SSD_HPG + 2 * q, g * SSD_HPG + 2 * q + 1):
                seg = cum[:, j:j + 1] - cum_t[j:j + 1, :]
                mats.append(cb * jnp.exp(jnp.where(causal, seg, -jnp.inf)))
            ll = jnp.concatenate(mats, axis=1).astype(BF16)
            c0 = g * SSD_GROUP_COLS + q * LANES
            xp = xdt[:, c0:c0 + LANES]
            bdx = jnp.concatenate([jnp.where(lo_half, xp, 0.0), jnp.where(lo_half, 0.0, xp)], axis=0)
            y_pairs.append(jnp.dot(ll, bdx.astype(BF16), preferred_element_type=F32))
        y_intra = jnp.concatenate(y_pairs, axis=-1)
        st_scr[g] = st * ecum_x[BLOCK - 1:BLOCK, gs] + _dot_tn(bg, xend[:, gs])
        y_groups.append(y_intra + y_inter + dskip_ref[:, gs] * x[:, gs])
    z = z_ref[...]
    nw = nw_ref[...]
    outs = []
    for g in range(SSD_GROUPS):
        gs = slice(g * SSD_GROUP_COLS, (g + 1) * SSD_GROUP_COLS)
        zg = z[:, gs]
        yg = y_groups[g] * (zg * _sigmoid(zg))
        yg = yg * lax.rsqrt(jnp.mean(yg * yg, axis=-1, keepdims=True) + SSD_NORM_EPS)
        outs.append((yg * nw[:, gs]).astype(o_ref.dtype))
    o_ref[...] = jnp.concatenate(outs, axis=-1)


def _ssd(zdt, xbc, dt_bias, a_log, d_skip, norm_w, *, nbatch, nblk):
    T = zdt.shape[0]
    row = lambda b, n: b * nblk + n
    pad_h = lambda t: jnp.pad(t.astype(F32), (0, SSD_DT_PAD - SSD_HEADS)).reshape(1, SSD_DT_PAD)
    head = lax.broadcasted_iota(jnp.int32, (SSD_DT_PAD, SSD_INNER), 0)
    colh = lax.broadcasted_iota(jnp.int32, (SSD_DT_PAD, SSD_INNER), 1) // SSD_HEAD_DIM
    expand = jnp.tile((head == colh).astype(BF16), (2, 1))
    full = lambda shape: pl.BlockSpec(shape, lambda b, n: (0, 0))
    b_blk = SSD_INNER // SSD_BC_COLS
    dt_blk = SSD_INNER // SSD_DT_PAD
    return pl.pallas_call(
        _ssd_kernel,
        grid=(nbatch, nblk),
        in_specs=[pl.BlockSpec((BLOCK, SSD_INNER), lambda b, n: (row(b, n), 0)),
                  pl.BlockSpec((BLOCK, SSD_DT_PAD), lambda b, n: (row(b, n), dt_blk)),
                  pl.BlockSpec((BLOCK, SSD_INNER), lambda b, n: (row(b, n), 0)),
                  pl.BlockSpec((BLOCK, SSD_BC_COLS), lambda b, n: (row(b, n), b_blk)),
                  pl.BlockSpec((BLOCK, SSD_BC_COLS), lambda b, n: (row(b, n), b_blk + 1)),
                  full((1, SSD_DT_PAD)), full((1, SSD_DT_PAD)), full((1, SSD_INNER)), full((1, SSD_INNER)),
                  full((2 * SSD_DT_PAD, SSD_INNER))],
        out_specs=pl.BlockSpec((BLOCK, SSD_INNER), lambda b, n: (row(b, n), 0)),
        out_shape=jax.ShapeDtypeStruct((T, SSD_INNER), BF16),
        scratch_shapes=[pltpu.VMEM((SSD_GROUPS, SSD_STATE, SSD_GROUP_COLS), F32)],
        compiler_params=_params("arbitrary", "arbitrary"),
        name="ssd",
    )(zdt, zdt, xbc, xbc, xbc, pad_h(dt_bias), pad_h(a_log),
      jnp.repeat(d_skip.astype(F32), SSD_HEAD_DIM).reshape(1, SSD_INNER), norm_w.reshape(1, SSD_INNER), expand)


def _pack_ar_w_in(w):
    return jnp.pad(w, ((0, 0), (0, LORA_PAD - LORA_COLS))).astype(BF16)


def _pack_lora_up(w_up, a_up, g_up):
    assert DECAY_LORA + AAA_LORA == LANES
    zero = jnp.zeros((DECAY_LORA, RWKV_DIM), F32)
    wup_wa = jnp.concatenate([jnp.concatenate([w_up, zero], axis=1), jnp.concatenate([zero, a_up], axis=1)], axis=0)
    wup_g = jnp.pad(g_up, ((0, LORA_PAD - LANES - GATE_LORA), (0, 0)))
    return wup_wa.astype(BF16), wup_g.astype(BF16)


def kernel(x, meta_tokens, mix_norm_w, ffn_norm_w, ar_w_in, ar_shift_mu, attn_q_norm_w, attn_k_norm_w, attn_sinks, rwkv_w0, rwkv_w_up, rwkv_a0, rwkv_a_up, rwkv_g_up, rwkv_k_k, rwkv_k_a, rwkv_r_k, rwkv_ln_w, rwkv_ln_b, ar_w_out, ssd_w_in, ssd_conv_w, ssd_conv_b, ssd_dt_bias, ssd_a_log, ssd_d, ssd_norm_w, ssd_w_out, ffn_w_up, ffn_conv_w, ffn_conv_b, ffn_w_down):
    nbatch, seq, d = x.shape
    assert d == D_MODEL and seq % BLOCK == 0
    seq_p = N_PAD + N_META + seq
    nblk = seq_p // BLOCK
    depth = mix_norm_w.shape[0]
    res = jnp.concatenate([
        jnp.zeros((nbatch, N_PAD, d), x.dtype),
        jnp.broadcast_to(meta_tokens.astype(x.dtype)[None], (nbatch, N_META, d)),
        x], axis=1).reshape(nbatch * seq_p, d)
    kw = dict(seq_p=seq_p, nbatch=nbatch)
    for layer in range(depth):
        i = layer // 2
        if layer % 2 == 0:
            zin = _norm_matmul(res, mix_norm_w[layer], _pack_ar_w_in(ar_w_in[i]), **kw)
            attn = _attention(zin, attn_sinks[i].astype(F32), attn_q_norm_w[i], attn_k_norm_w[i],
                              nbatch=nbatch, nblk=nblk)
            mu = ar_shift_mu[i]
            mu_rkv = mu[:3 * RWKV_DIM].reshape(1, -1)
            mu_lora = jnp.pad(mu[3 * RWKV_DIM:], (0, LORA_PAD - LORA_COLS)).reshape(1, -1)
            vecs = jnp.stack([rwkv_w0[i], rwkv_a0[i], rwkv_k_k[i], rwkv_k_a[i], rwkv_r_k[i].reshape(-1),
                              rwkv_ln_w[i], rwkv_ln_b[i], jnp.zeros((RWKV_DIM,), F32)]).astype(F32)
            tm = _rwkv(zin, mu_rkv, mu_lora, *_pack_lora_up(rwkv_w_up[i], rwkv_a_up[i], rwkv_g_up[i]), vecs,
                       nbatch=nbatch, nblk=nblk)
            res = _proj_residual([attn, tm], ar_w_out[i].astype(BF16), res)
        else:
            w_in = ssd_w_in[i]
            xbc0, dt0 = SSD_INNER, 2 * SSD_INNER + 2 * SSD_BC_COLS
            w_zdt = jnp.concatenate([w_in[:, :xbc0], w_in[:, dt0:],
                                     jnp.zeros((d, SSD_DT_PAD - SSD_HEADS), w_in.dtype)], axis=1).astype(BF16)
            zdt = _norm_matmul(res, mix_norm_w[layer], w_zdt, **kw)
            xbc = _norm_matmul_conv(res, mix_norm_w[layer], w_in[:, xbc0:dt0].astype(BF16), ssd_conv_w[i],
                                    ssd_conv_b[i], tn=(dt0 - xbc0) // 4, **kw)
            y = _ssd(zdt, xbc, ssd_dt_bias[i], ssd_a_log[i], ssd_d[i], ssd_norm_w[i], nbatch=nbatch, nblk=nblk)
            res = _proj_residual([y], ssd_w_out[i].astype(BF16), res)
        res = _ffn(res, ffn_norm_w[layer], ffn_w_up[layer].astype(BF16), ffn_conv_w[layer], ffn_conv_b[layer],
                   ffn_w_down[layer].astype(BF16), **kw)
    return res.reshape(nbatch, seq_p, d)[:, N_PAD + N_META:]
```

```python
import functools
import math

import jax
import jax.numpy as jnp
from jax import lax
from jax.experimental import pallas as pl
from jax.experimental.pallas import tpu as pltpu

F32 = jnp.float32
BF16 = jnp.bfloat16

D_MODEL = 2048
N_META = 16
BLOCK = 128
N_PAD = BLOCK - N_META
NORM_EPS = 1e-6

HEAD_DIM = 64
ATTN_HEADS = 16
ATTN_KV_HEADS = 4
ATTN_GROUP = 4
WINDOW = 128
QK_EPS = 1e-6
Q_COLS = ATTN_HEADS * HEAD_DIM
KV_COLS = ATTN_KV_HEADS * HEAD_DIM
ATTN_COLS = Q_COLS + 2 * KV_COLS

RWKV_DIM = 1024
RWKV_HEAD = 64
DECAY_LORA = 64
AAA_LORA = 64
GATE_LORA = 160
LORA_COLS = DECAY_LORA + AAA_LORA + GATE_LORA
LORA_PAD = 384
RWKV_GN_EPS = 64e-5
DECAY_SCALE = math.exp(-0.5)
RWKV_CHUNK = 64
AR_COLS = ATTN_COLS + 3 * RWKV_DIM + LORA_PAD

SSD_INNER = 4096
SSD_HEAD_DIM = 64
SSD_HEADS = 64
SSD_GROUPS = 8
SSD_HPG = 8
SSD_STATE = 128
SSD_CONV = 4
SSD_GROUP_COLS = SSD_INNER // SSD_GROUPS
SSD_BC_COLS = SSD_GROUPS * SSD_STATE
SSD_DT_PAD = 128
SSD_NORM_EPS = 1e-5

FFN_DIM = 5632
FFN_CONV = 3
FFN_CHUNK = 512
CONV_SLAB = 256

LANES = 128
SUBLANES = 8
VMEM_LIMIT = 56 * 1024 * 1024


def _params(*sem):
    return pltpu.CompilerParams(dimension_semantics=sem, vmem_limit_bytes=VMEM_LIMIT)


def _row_tile(total, target):
    best = None
    for t in range(16, min(total, target) + 1, 16):
        if total % t == 0:
            best = t
    assert best is not None, (total, target)
    return best


def _dot(a, b):
    return jnp.dot(a.astype(BF16), b.astype(BF16), preferred_element_type=F32)


def _dot_nt(a, b):
    return lax.dot_general(a.astype(BF16), b.astype(BF16), (((1,), (1,)), ((), ())),
                           preferred_element_type=F32)


def _dot_tn(a, b):
    return lax.dot_general(a.astype(BF16), b.astype(BF16), (((0,), (0,)), ((), ())),
                           preferred_element_type=F32)


def _split3(x):
    hi = x.astype(BF16)
    r1 = x - hi.astype(F32)
    mid = r1.astype(BF16)
    lo = (r1 - mid.astype(F32)).astype(BF16)
    return hi, mid, lo


def _dot_exact_rhs(sel_bf16, x):
    return jnp.dot(jnp.concatenate([sel_bf16] * 3, axis=1), jnp.concatenate(_split3(x), axis=0),
                   preferred_element_type=F32)


def _half_sums(x):
    hi = x.astype(BF16)
    lo = (x - hi.astype(F32)).astype(BF16)
    i = lax.broadcasted_iota(jnp.int32, (2 * LANES, LANES), 0) & (LANES - 1)
    j = lax.broadcasted_iota(jnp.int32, (2 * LANES, LANES), 1)
    ones2 = jnp.where((i < LANES // 2) == (j < LANES // 2), 1.0, 0.0).astype(BF16)
    return jnp.dot(jnp.concatenate([hi, lo], axis=1), ones2, preferred_element_type=F32)


def _sigmoid(x):
    return 1.0 / (1.0 + jnp.exp(-x))


def _softplus(x):
    return jnp.maximum(x, 0.0) + jnp.log(1.0 + jnp.exp(-jnp.abs(x)))


def _valid_rows(row0, nrows, seq_p, nbatch):
    rows = row0 + lax.broadcasted_iota(jnp.int32, (nrows, 1), 0)
    pos = rows
    for b in range(1, nbatch):
        pos = jnp.where(rows >= b * seq_p, rows - b * seq_p, pos)
    return pos >= N_PAD


def _masked_rmsnorm(x, w, row0, seq_p, nbatch):
    ms = jnp.mean(x * x, axis=-1, keepdims=True)
    y = x * lax.rsqrt(ms + NORM_EPS) * w
    return jnp.where(_valid_rows(row0, x.shape[0], seq_p, nbatch), y, 0.0)


def _norm_mm_kernel(x_ref, nw_ref, w_ref, o_ref, *, tm, seq_p, nbatch):
    h = _masked_rmsnorm(x_ref[...], nw_ref[...], pl.program_id(0) * tm, seq_p, nbatch)
    o_ref[...] = jnp.dot(h.astype(BF16), w_ref[...], preferred_element_type=F32)


def _norm_matmul(res, norm_w, w_bf16, *, seq_p, nbatch):
    T, D = res.shape
    N = w_bf16.shape[1]
    tm = _row_tile(T, 384)
    return pl.pallas_call(
        functools.partial(_norm_mm_kernel, tm=tm, seq_p=seq_p, nbatch=nbatch),
        grid=(T // tm,),
        in_specs=[pl.BlockSpec((tm, D), lambda m: (m, 0)),
                  pl.BlockSpec((1, D), lambda m: (0, 0)),
                  pl.BlockSpec((D, N), lambda m: (0, 0), pipeline_mode=pl.Buffered(1))],
        out_specs=pl.BlockSpec((tm, N), lambda m: (m, 0)),
        out_shape=jax.ShapeDtypeStruct((T, N), F32),
        compiler_params=_params("arbitrary"),
        name="norm_in_proj",
    )(res, norm_w.reshape(1, D), w_bf16)


def _ssd_in_proj_kernel(x_ref, nw_ref, wz_ref, wx_ref, cw_ref, cb_ref, oz_ref, ox_ref, h_scr, carry_scr,
                        *, tm, seq_p, nbatch):
    m = pl.program_id(0)
    n = pl.program_id(1)

    @pl.when(n == 0)
    def _():
        h = _masked_rmsnorm(x_ref[...], nw_ref[...], m * tm, seq_p, nbatch)
        h_scr[...] = h.astype(BF16)

    @pl.when(m == 0)
    def _():
        carry_scr[n] = jnp.zeros(carry_scr.shape[1:], F32)

    valid = _valid_rows(m * tm, tm, seq_p, nbatch)
    h = h_scr[...]

    def conv_epilogue(sl, xc):
        full = jnp.concatenate([carry_scr[n, :, sl], xc], axis=0)
        carry_scr[n, :, sl] = xc[tm - SUBLANES:]
        acc = cb_ref[:, sl]
        for j in range(SSD_CONV):
            k = SSD_CONV - 1 - j
            term = xc if k == 0 else pltpu.roll(full, k, axis=0)[SUBLANES:]
            acc = acc + cw_ref[j:j + 1, sl] * term
        ox_ref[:, sl] = jnp.where(valid, acc * _sigmoid(acc), 0.0)

    nslab = ox_ref.shape[1] // CONV_SLAB
    zpieces = [(c, min(c + CONV_SLAB, oz_ref.shape[1])) for c in range(0, oz_ref.shape[1], CONV_SLAB)]
    zdone = 0
    pending = None
    for s in range(nslab):
        sl = slice(s * CONV_SLAB, (s + 1) * CONV_SLAB)
        xc = jnp.dot(h, wx_ref[:, sl], preferred_element_type=F32)
        ztarget = (len(zpieces) * (s + 1)) // nslab
        for c0, c1 in zpieces[zdone:ztarget]:
            oz_ref[:, c0:c1] = jnp.dot(h, wz_ref[:, c0:c1], preferred_element_type=F32)
        zdone = ztarget
        if pending is not None:
            conv_epilogue(*pending)
        pending = (sl, xc)
    conv_epilogue(*pending)


def _ssd_in_proj(res, norm_w, w_zdt, w_xbc, conv_w, conv_b, *, seq_p, nbatch):
    T, D = res.shape
    nz, nx = w_zdt.shape[1], w_xbc.shape[1]
    steps = 3
    tz, tx = nz // steps, nx // steps
    assert tz * steps == nz and tx * steps == nx and tz % LANES == 0 and tx % CONV_SLAB == 0
    tm = _row_tile(T, 384)
    return pl.pallas_call(
        functools.partial(_ssd_in_proj_kernel, tm=tm, seq_p=seq_p, nbatch=nbatch),
        grid=(T // tm, steps),
        in_specs=[pl.BlockSpec((tm, D), lambda m, n: (m, 0)),
                  pl.BlockSpec((1, D), lambda m, n: (0, 0)),
                  pl.BlockSpec((D, tz), lambda m, n: (0, n)),
                  pl.BlockSpec((D, tx), lambda m, n: (0, n)),
                  pl.BlockSpec((SSD_CONV, tx), lambda m, n: (0, n)),
                  pl.BlockSpec((1, tx), lambda m, n: (0, n))],
        out_specs=[pl.BlockSpec((tm, tz), lambda m, n: (m, n)),
                   pl.BlockSpec((tm, tx), lambda m, n: (m, n))],
        out_shape=[jax.ShapeDtypeStruct((T, nz), F32), jax.ShapeDtypeStruct((T, nx), F32)],
        scratch_shapes=[pltpu.VMEM((tm, D), BF16), pltpu.VMEM((steps, SUBLANES, tx), F32)],
        compiler_params=_params("arbitrary", "arbitrary"),
        name="ssd_in_proj",
    )(res, norm_w.reshape(1, D), w_zdt, w_xbc, conv_w, conv_b.reshape(1, nx))


def _proj_res_kernel(*refs, n_lhs):
    lhs = refs[:n_lhs]
    ws = refs[n_lhs:2 * n_lhs]
    res_ref, o_ref = refs[2 * n_lhs], refs[2 * n_lhs + 1]
    acc = res_ref[...]
    for l, w in zip(lhs, ws):
        acc = acc + jnp.dot(l[...], w[...], preferred_element_type=F32)
    o_ref[...] = acc


def _proj_residual(lhs_list, w_bf16, res):
    T, D = res.shape
    n_lhs = len(lhs_list)
    kw = lhs_list[0].shape[1]
    assert all(l.shape == (T, kw) for l in lhs_list) and w_bf16.shape == (n_lhs * kw, D)
    tm = _row_tile(T, 384)
    in_specs = [pl.BlockSpec((tm, kw), lambda m: (m, 0)) for _ in range(n_lhs)]
    in_specs += [pl.BlockSpec((kw, D), functools.partial(lambda i, m: (i, 0), i), pipeline_mode=pl.Buffered(1))
                 for i in range(n_lhs)]
    in_specs += [pl.BlockSpec((tm, D), lambda m: (m, 0))]
    return pl.pallas_call(
        functools.partial(_proj_res_kernel, n_lhs=n_lhs),
        grid=(T // tm,),
        in_specs=in_specs,
        out_specs=pl.BlockSpec((tm, D), lambda m: (m, 0)),
        out_shape=jax.ShapeDtypeStruct((T, D), F32),
        compiler_params=_params("arbitrary"),
        name="out_proj_residual",
    )(*lhs_list, *([w_bf16] * n_lhs), res)


def _ffn_kernel(res_ref, nw_ref, wg_ref, wv_ref, cw_ref, cb_ref, wd_ref, o_ref, h_scr, carry_scr,
                *, tm, seq_p, nbatch):
    m = pl.program_id(0)
    f = pl.program_id(1)

    @pl.when(f == 0)
    def _():
        res = res_ref[...]
        h_scr[...] = _masked_rmsnorm(res, nw_ref[...], m * tm, seq_p, nbatch).astype(BF16)
        o_ref[...] = res

    @pl.when(m == 0)
    def _():
        carry_scr[f] = jnp.zeros(carry_scr.shape[1:], F32)

    h = h_scr[...]
    gate = jnp.dot(h, wg_ref[...], preferred_element_type=F32)
    val = jnp.dot(h, wv_ref[...], preferred_element_type=F32)
    prev = carry_scr[f]
    rows = lax.broadcasted_iota(jnp.int32, (tm, 1), 0)
    g1 = jnp.where(rows == 0, prev[7:8], pltpu.roll(gate, 1, axis=0))
    g2 = jnp.where(rows == 0, prev[6:7], jnp.where(rows == 1, prev[7:8], pltpu.roll(gate, 2, axis=0)))
    carry_scr[f] = gate[tm - SUBLANES:]
    cw = cw_ref[...]
    pre = cb_ref[...] + cw[0:1] * g2 + cw[1:2] * g1 + cw[2:3] * gate
    act = pre * _sigmoid(pre) * val
    o_ref[...] += jnp.dot(act.astype(BF16), wd_ref[...], preferred_element_type=F32)


def _ffn(res, norm_w, w_up_bf16, conv_w, conv_b, w_down_bf16, *, seq_p, nbatch):
    T, D = res.shape
    tm = _row_tile(T, 768)
    tf = FFN_CHUNK
    nf = FFN_DIM // tf
    return pl.pallas_call(
        functools.partial(_ffn_kernel, tm=tm, seq_p=seq_p, nbatch=nbatch),
        grid=(T // tm, nf),
        in_specs=[pl.BlockSpec((tm, D), lambda m, f: (m, 0)),
                  pl.BlockSpec((1, D), lambda m, f: (0, 0)),
                  pl.BlockSpec((D, tf), lambda m, f: (0, f)),
                  pl.BlockSpec((D, tf), lambda m, f: (0, nf + f)),
                  pl.BlockSpec((FFN_CONV, tf), lambda m, f: (0, f)),
                  pl.BlockSpec((1, tf), lambda m, f: (0, f)),
                  pl.BlockSpec((tf, D), lambda m, f: (f, 0))],
        out_specs=pl.BlockSpec((tm, D), lambda m, f: (m, 0)),
        out_shape=jax.ShapeDtypeStruct((T, D), F32),
        scratch_shapes=[pltpu.VMEM((tm, D), BF16), pltpu.VMEM((nf, SUBLANES, tf), F32)],
        compiler_params=_params("arbitrary", "arbitrary"),
        name="ffn",
    )(res, norm_w.reshape(1, D), w_up_bf16, w_up_bf16, conv_w, conv_b.reshape(1, FFN_DIM), w_down_bf16)


def _attn_kernel(sinks_ref, q_ref, kvc_ref, kvp_ref, kvm_ref, qw_ref, kw_ref, o_ref):
    n = pl.program_id(1)
    scale = HEAD_DIM ** -0.5
    lane = lax.broadcasted_iota(jnp.int32, (BLOCK, LANES), 1)
    lo_half = lane < HEAD_DIM

    kvs = (kvp_ref[...], kvc_ref[...], kvm_ref[...])
    q = q_ref[...]
    nq, nk = Q_COLS // LANES, KV_COLS // LANES
    tiles = [q[:, c * LANES:(c + 1) * LANES] for c in range(nq)]
    tiles += [kv[:, c * LANES:(c + 1) * LANES] for kv in kvs for c in range(nk)]
    sq = jnp.concatenate([t * t for t in tiles], axis=0)
    ms = _half_sums(sq) * (1.0 / HEAD_DIM)
    inv = lax.rsqrt(ms + QK_EPS)
    qw = qw_ref[...] * scale
    kw = kw_ref[...]
    qn = [tiles[c] * inv[c * BLOCK:(c + 1) * BLOCK] * qw for c in range(nq)]
    kn = [[tiles[nq + i * nk + c] * inv[(nq + i * nk + c) * BLOCK:(nq + i * nk + c + 1) * BLOCK] * kw
           for i in range(3)] for c in range(nk)]

    qi = lax.broadcasted_iota(jnp.int32, (BLOCK, 3 * BLOCK), 0)
    kj = lax.broadcasted_iota(jnp.int32, (BLOCK, 3 * BLOCK), 1)
    j_prev, j_cur, j_meta = kj, kj - BLOCK, kj - 2 * BLOCK
    first_prev = jnp.where(n == 0, BLOCK, jnp.where(n == 1, N_PAD, 0))
    first_cur = jnp.where(n == 0, N_PAD, 0)
    vis_prev = (kj < BLOCK) & (j_prev > qi) & (j_prev >= first_prev)
    vis_cur = (kj >= BLOCK) & (j_cur <= qi) & (j_cur >= first_cur)
    vis_meta = (j_meta >= N_PAD) & (n * BLOCK + qi - j_meta >= WINDOW)
    bias = jnp.where(vis_prev | vis_cur | vis_meta, 0.0, -jnp.inf)
    bias4 = jnp.concatenate([bias] * ATTN_GROUP, axis=0)
    grow = lax.broadcasted_iota(jnp.int32, (ATTN_GROUP * BLOCK, 1), 0)

    def dup_half(x, upper):
        swapped = pltpu.roll(x, HEAD_DIM, axis=1)
        lane_r = lax.broadcasted_iota(jnp.int32, x.shape, 1) < HEAD_DIM
        return jnp.where(lane_r, swapped, x) if upper else jnp.where(lane_r, x, swapped)

    outs = []
    for c in range(nk):
        k_col = jnp.concatenate(kn[c], axis=0)
        v_col = jnp.concatenate([kv[:, KV_COLS + c * LANES:KV_COLS + (c + 1) * LANES] for kv in kvs], axis=0)
        for half in range(2):
            h = 2 * c + half
            k_dup = dup_half(k_col, half == 1).astype(BF16)
            v_dup = dup_half(v_col, half == 1).astype(BF16)
            lhs = jnp.concatenate(
                [jnp.where(lo_half if e == 0 else jnp.logical_not(lo_half), qn[2 * h + pr], 0.0)
                 for pr in range(2) for e in range(2)], axis=0)
            s = _dot_nt(lhs, k_dup) + bias4
            sink = sinks_ref[ATTN_GROUP * h]
            for g in range(1, ATTN_GROUP):
                sink = jnp.where(grow >= g * BLOCK, sinks_ref[ATTN_GROUP * h + g], sink)
            mx = jnp.maximum(jnp.max(s, axis=-1, keepdims=True), sink)
            p = jnp.exp(s - mx)
            den = jnp.sum(p, axis=-1, keepdims=True) + jnp.exp(sink - mx)
            o = jnp.dot(p.astype(BF16), v_dup, preferred_element_type=F32) / den
            for pr in range(2):
                r0 = 2 * pr * BLOCK
                outs.append(jnp.where(lo_half, o[r0:r0 + BLOCK], o[r0 + BLOCK:r0 + 2 * BLOCK]))
    o_ref[...] = jnp.concatenate(outs, axis=-1).astype(o_ref.dtype)


def _attention(zin, sinks, q_norm_w, k_norm_w, *, nbatch, nblk):
    T = zin.shape[0]
    q_blk = 0
    kv_blk = Q_COLS // (2 * KV_COLS)
    row = lambda b, n: b * nblk + n
    return pl.pallas_call(
        _attn_kernel,
        grid=(nbatch, nblk),
        in_specs=[pl.BlockSpec(memory_space=pltpu.SMEM),
                  pl.BlockSpec((BLOCK, Q_COLS), lambda b, n: (row(b, n), q_blk)),
                  pl.BlockSpec((BLOCK, 2 * KV_COLS), lambda b, n: (row(b, n), kv_blk)),
                  pl.BlockSpec((BLOCK, 2 * KV_COLS), lambda b, n: (row(b, jnp.maximum(n - 1, 0)), kv_blk)),
                  pl.BlockSpec((BLOCK, 2 * KV_COLS), lambda b, n: (row(b, 0), kv_blk)),
                  pl.BlockSpec((1, LANES), lambda b, n: (0, 0)),
                  pl.BlockSpec((1, LANES), lambda b, n: (0, 0))],
        out_specs=pl.BlockSpec((BLOCK, Q_COLS), lambda b, n: (row(b, n), 0)),
        out_shape=jax.ShapeDtypeStruct((T, Q_COLS), BF16),
        compiler_params=_params("arbitrary", "arbitrary"),
        name="swa_attention",
    )(sinks, zin, zin, zin, zin, jnp.tile(q_norm_w, LANES // HEAD_DIM).reshape(1, LANES),
      jnp.tile(k_norm_w, LANES // HEAD_DIM).reshape(1, LANES))


def _pair_blockdiag(x):
    xb = x.astype(BF16)
    lane = lax.broadcasted_iota(jnp.int32, xb.shape, 2)
    zero = jnp.zeros_like(xb)
    return jnp.concatenate([jnp.where(lane < RWKV_HEAD, xb, zero), jnp.where(lane >= RWKV_HEAD, xb, zero)], axis=1)


def _bmm(a, b_bf16):
    return lax.dot_general(a.astype(BF16), b_bf16, (((2,), (1,)), ((0,), (0,))), preferred_element_type=F32)


def _bmm_nt(a, b_bf16):
    return lax.dot_general(a.astype(BF16), b_bf16, (((2,), (2,)), ((0,), (0,))), preferred_element_type=F32)


def _pair_pick(full):
    lane = lax.broadcasted_iota(jnp.int32, (full.shape[0], RWKV_HEAD, LANES), 2)
    return jnp.where(lane < RWKV_HEAD, full[:, :RWKV_HEAD], full[:, RWKV_HEAD:])


def _to_pairs(x, nchunk):
    L = RWKV_CHUNK
    return jnp.stack([x[c * L:(c + 1) * L, p * LANES:(p + 1) * LANES]
                      for c in range(nchunk) for p in range(x.shape[1] // LANES)], axis=0)


def _pair_segsum(*xs):
    npair = xs[0].shape[1] // LANES
    rows = xs[0].shape[0]
    stacked = jnp.concatenate([x[:, p * LANES:(p + 1) * LANES] for x in xs for p in range(npair)], axis=0)
    s = _half_sums(stacked)
    outs = [jnp.concatenate([s[(i * npair + p) * rows:(i * npair + p + 1) * rows] for p in range(npair)], axis=-1)
            for i in range(len(xs))]
    return outs[0] if len(xs) == 1 else outs


def _rwkv_kernel(rkv_a_ref, rkv_b_ref, lora_ref, mu_rkv_ref, mu_lora_ref, wup_wa_ref, wup_g_ref, vec_ref, o_ref,
                 s_scr, carry_rkv, carry_lora, *, tb):
    n = pl.program_id(1)
    L = RWKV_CHUNK
    npair = RWKV_DIM // LANES

    @pl.when(n == 0)
    def _():
        s_scr[...] = jnp.zeros(s_scr.shape, F32)
        carry_rkv[...] = jnp.zeros(carry_rkv.shape, F32)
        carry_lora[...] = jnp.zeros(carry_lora.shape, F32)

    rows = lax.broadcasted_iota(jnp.int32, (tb, 1), 0)
    valid = (n > 0) | (rows >= N_PAD)

    def token_shift(z, carry, mu_ref):
        prev = jnp.where(rows == 0, carry[SUBLANES - 1:SUBLANES], pltpu.roll(z, 1, axis=0))
        carry[...] = z[tb - SUBLANES:]
        return jnp.where(valid, z + (prev - z) * mu_ref[...], 0.0)

    zr = token_shift(jnp.concatenate([rkv_a_ref[...], rkv_b_ref[...]], axis=1), carry_rkv, mu_rkv_ref)
    zl = token_shift(lora_ref[...], carry_lora, mu_lora_ref)
    r, k, v = zr[:, :RWKV_DIM], zr[:, RWKV_DIM:2 * RWKV_DIM], zr[:, 2 * RWKV_DIM:]

    z_wa = zl[:, :LANES]
    col = lax.broadcasted_iota(jnp.int32, z_wa.shape, 1)
    act_wa = jnp.where(col < DECAY_LORA, jnp.tanh(z_wa), z_wa)
    up_wa = jnp.dot(act_wa.astype(BF16), wup_wa_ref[...], preferred_element_type=F32)
    g = jnp.dot(_sigmoid(zl[:, LANES:]).astype(BF16), wup_g_ref[...], preferred_element_type=F32)
    vec = vec_ref[...]
    w0, a0, k_k, k_a, r_k, ln_w, ln_b = (vec[i:i + 1] for i in range(7))
    w_log = jnp.where(valid, -DECAY_SCALE * _sigmoid(w0 + up_wa[:, :RWKV_DIM]), 0.0)
    a = _sigmoid(a0 + up_wa[:, RWKV_DIM:])

    kk = k * k_k
    k2 = k * (1.0 + (a - 1.0) * k_a)
    kk_sq, bonus = _pair_segsum(kk * kk, r * k2 * r_k)
    kk = kk * lax.rsqrt(jnp.maximum(kk_sq, 1e-24))
    kka = kk * a

    nchunk = tb // L
    nb = nchunk * npair
    t_i = lax.broadcasted_iota(jnp.int32, (nb, L, LANES), 1)
    s_i = lax.broadcasted_iota(jnp.int32, (nb, L, LANES), 2) & (RWKV_HEAD - 1)
    strict = s_i < t_i
    incl = s_i <= t_i
    same16 = (t_i >> 4) == (s_i >> 4)
    same32 = (t_i >> 5) == (s_i >> 5)
    eye = jnp.where(s_i == t_i, 1.0, 0.0)
    ti2 = lax.broadcasted_iota(jnp.int32, (tb, tb), 0)
    si2 = lax.broadcasted_iota(jnp.int32, (tb, tb), 1)
    lshift = L.bit_length() - 1
    tri_incl = jnp.where((si2 <= ti2) & ((si2 >> lshift) == (ti2 >> lshift)), 1.0, 0.0).astype(BF16)
    cum = _dot_exact_rhs(tri_incl, w_log)
    cum_end = jnp.concatenate(
        [jnp.broadcast_to(cum[(c + 1) * L - 1:(c + 1) * L], (L, RWKV_DIM)) for c in range(nchunk)], axis=0)
    e_w = jnp.exp(cum)
    e_iw = jnp.exp(-cum)
    e_prev = jnp.exp(cum - w_log)
    e_end = jnp.exp(cum_end - cum)
    w_end = jnp.exp(cum_end)
    ah = _to_pairs(-kk * e_prev, nchunk)
    bh = _to_pairs(kka * e_iw, nchunk)
    kh = _to_pairs(k2 * e_iw, nchunk)
    rh = _to_pairs(r * e_w, nchunk)
    bt = _to_pairs(kka * e_end, nchunk)
    kt = _to_pairs(k2 * e_end, nchunk)
    vp = _to_pairs(v, nchunk)
    vbd = _pair_blockdiag(vp)

    ar = jnp.concatenate([ah, rh], axis=1)
    m = _bmm_nt(ar, jnp.concatenate([_pair_blockdiag(bh), _pair_blockdiag(kh)], axis=1))
    a_ab = jnp.where(strict, m[:, :L, :LANES], 0.0)
    a_rb = jnp.where(incl, m[:, L:, :LANES], 0.0)
    a_ak = jnp.where(strict, m[:, :L, LANES:], 0.0)
    a_rk = jnp.where(incl, m[:, L:, LANES:], 0.0)
    kv = _bmm(jnp.concatenate([a_ak, a_rk], axis=1), vbd)
    akv, y_rk = kv[:, :L], kv[:, L:]
    a1 = jnp.where(same16, a_ab, 0.0)
    t1 = eye + a1
    a2 = _bmm(a1, _pair_blockdiag(a1))
    x = _bmm(jnp.concatenate([a2, t1], axis=1), _pair_blockdiag(a2))
    a4, t2 = x[:, :L], t1 + x[:, L:]
    x = _bmm(jnp.concatenate([a4, t2], axis=1), _pair_blockdiag(a4))
    a8, t3 = x[:, :L], t2 + x[:, L:]
    t16 = t3 + _bmm(t3, _pair_blockdiag(a8))
    off32 = jnp.where(same32 & jnp.logical_not(same16), a_ab, 0.0)
    t32 = t16 + _bmm(t16, _pair_blockdiag(_bmm(off32, _pair_blockdiag(t16))))
    off64 = jnp.where(same32, 0.0, a_ab)
    tinv = t32 + _bmm(t32, _pair_blockdiag(_bmm(off64, _pair_blockdiag(t32))))
    tg = _bmm(tinv, jnp.concatenate([_pair_blockdiag(ah), _pair_blockdiag(akv)], axis=2))
    ta, gm = tg[:, :, :LANES], tg[:, :, LANES:]
    rg = _bmm(a_rb, jnp.concatenate([_pair_blockdiag(ta), _pair_blockdiag(gm)], axis=2))
    rt = rh + rg[:, :, :LANES]
    y_intra = rg[:, :, LANES:] + y_rk
    tgv_t = jnp.swapaxes(jnp.concatenate([ta, gm, vp], axis=2), 1, 2)
    pp = _bmm(tgv_t[:, :2 * LANES], bt.astype(BF16))
    phi = _pair_pick(pp[:, :LANES])
    psi = _pair_pick(pp[:, LANES:]) + _pair_pick(_bmm(tgv_t[:, 2 * LANES:], kt.astype(BF16)))
    w_end_p = _to_pairs(w_end, nchunk)[:, :1]

    y_chunks = []
    s = s_scr[...]
    for c in range(nchunk):
        cs = slice(c * npair, (c + 1) * npair)
        y_c = _bmm_nt(rt[cs], _pair_blockdiag(s)) + y_intra[cs]
        s = s * w_end_p[cs] + _bmm(s, _pair_blockdiag(phi[cs])) + psi[cs]
        y_chunks.append(jnp.concatenate([y_c[p] for p in range(npair)], axis=-1))
    s_scr[...] = s
    y = jnp.concatenate(y_chunks, axis=0)

    inv_n = 1.0 / RWKV_HEAD
    mu = _pair_segsum(y) * inv_n
    d = y - mu
    var = _pair_segsum(d * d) * inv_n
    yn = d * lax.rsqrt(var + RWKV_GN_EPS) * ln_w + ln_b
    o_ref[...] = ((yn + bonus * v) * g).astype(o_ref.dtype)


def _rwkv(zin, mu_rkv, mu_lora, wup_wa, wup_g, vecs, *, nbatch, nblk):
    T = zin.shape[0]
    tb = BLOCK
    half = 3 * RWKV_DIM // 2
    assert ATTN_COLS == half and (ATTN_COLS + 3 * RWKV_DIM) % LORA_PAD == 0
    lora_blk = (ATTN_COLS + 3 * RWKV_DIM) // LORA_PAD
    row = lambda b, n: b * nblk + n
    return pl.pallas_call(
        functools.partial(_rwkv_kernel, tb=tb),
        grid=(nbatch, nblk),
        in_specs=[pl.BlockSpec((tb, half), lambda b, n: (row(b, n), 1)),
                  pl.BlockSpec((tb, half), lambda b, n: (row(b, n), 2)),
                  pl.BlockSpec((tb, LORA_PAD), lambda b, n: (row(b, n), lora_blk)),
                  pl.BlockSpec((1, 3 * RWKV_DIM), lambda b, n: (0, 0)),
                  pl.BlockSpec((1, LORA_PAD), lambda b, n: (0, 0)),
                  pl.BlockSpec((LANES, 2 * RWKV_DIM), lambda b, n: (0, 0)),
                  pl.BlockSpec((LORA_PAD - LANES, RWKV_DIM), lambda b, n: (0, 0)),
                  pl.BlockSpec((SUBLANES, RWKV_DIM), lambda b, n: (0, 0))],
        out_specs=pl.BlockSpec((tb, RWKV_DIM), lambda b, n: (row(b, n), 0)),
        out_shape=jax.ShapeDtypeStruct((T, RWKV_DIM), BF16),
        scratch_shapes=[pltpu.VMEM((RWKV_DIM // LANES, RWKV_HEAD, LANES), F32),
                        pltpu.VMEM((SUBLANES, 3 * RWKV_DIM), F32),
                        pltpu.VMEM((SUBLANES, LORA_PAD), F32)],
        compiler_params=_params("arbitrary", "arbitrary"),
        name="rwkv7",
    )(zin, zin, zin, mu_rkv, mu_lora, wup_wa, wup_g, vecs)


def _ssd_kernel(z_ref, dt_ref, x_ref, b_ref, c_ref, dtb_ref, alog_ref, dskip_ref, nw_ref, exp_ref, o_ref, st_scr):
    n = pl.program_id(1)

    @pl.when(n == 0)
    def _():
        st_scr[...] = jnp.zeros(st_scr.shape, F32)

    rows = lax.broadcasted_iota(jnp.int32, (BLOCK, 1), 0)
    valid = (n > 0) | (rows >= N_PAD)
    x = x_ref[...]
    bm = b_ref[...]
    cm = c_ref[...]

    dt = jnp.where(valid, _softplus(dt_ref[...] + dtb_ref[...]), 0.0)
    adt = dt * (-jnp.exp(alog_ref[...]))
    ti = lax.broadcasted_iota(jnp.int32, (BLOCK, BLOCK), 0)
    si = lax.broadcasted_iota(jnp.int32, (BLOCK, BLOCK), 1)
    causal = si <= ti
    cum = _dot_exact_rhs(jnp.where(causal, 1.0, 0.0).astype(BF16), adt)
    cum_t = cum.T
    ecum = jnp.exp(cum)
    dt_end = dt * jnp.exp(cum[BLOCK - 1:BLOCK] - cum)
    e_hi = ecum.astype(BF16)
    e_lo = (ecum - e_hi.astype(F32)).astype(BF16)
    ecum_x = jnp.dot(jnp.concatenate([e_hi, e_lo], axis=1), exp_ref[...], preferred_element_type=F32)
    ex = jnp.dot(jnp.concatenate([dt, dt_end], axis=0).astype(BF16), exp_ref[:SSD_DT_PAD],
                 preferred_element_type=F32)
    xdt = x * ex[:BLOCK]
    xend = x * ex[BLOCK:]
    lane = lax.broadcasted_iota(jnp.int32, (BLOCK, LANES), 1)
    lo_half = lane < SSD_HEAD_DIM

    y_groups = []
    for g in range(SSD_GROUPS):
        gs = slice(g * SSD_GROUP_COLS, (g + 1) * SSD_GROUP_COLS)
        bg = bm[:, g * SSD_STATE:(g + 1) * SSD_STATE]
        cg = cm[:, g * SSD_STATE:(g + 1) * SSD_STATE]
        cb = _dot_nt(cg, bg)
        st = st_scr[g]
        y_inter = _dot(cg, st) * ecum_x[:, gs]
        y_pairs = []
        for q in range(SSD_HPG // 2):
            mats = []
            for j in (g * SSD_HPG + 2 * q, g * SSD_HPG + 2 * q + 1):
                seg = cum[:, j:j + 1] - cum_t[j:j + 1, :]
                mats.append(cb * jnp.exp(jnp.where(causal, seg, -jnp.inf)))
            ll = jnp.concatenate(mats, axis=1).astype(BF16)
            c0 = g * SSD_GROUP_COLS + q * LANES
            xp = xdt[:, c0:c0 + LANES]
            bdx = jnp.concatenate([jnp.where(lo_half, xp, 0.0), jnp.where(lo_half, 0.0, xp)], axis=0)
            y_pairs.append(jnp.dot(ll, bdx.astype(BF16), preferred_element_type=F32))
        y_intra = jnp.concatenate(y_pairs, axis=-1)
        st_scr[g] = st * ecum_x[BLOCK - 1:BLOCK, gs] + _dot_tn(bg, xend[:, gs])
        y_groups.append(y_intra + y_inter + dskip_ref[:, gs] * x[:, gs])
    z = z_ref[...]
    nw = nw_ref[...]
    outs = []
    for g in range(SSD_GROUPS):
        gs = slice(g * SSD_GROUP_COLS, (g + 1) * SSD_GROUP_COLS)
        zg = z[:, gs]
        yg = y_groups[g] * (zg * _sigmoid(zg))
        yg = yg * lax.rsqrt(jnp.mean(yg * yg, axis=-1, keepdims=True) + SSD_NORM_EPS)
        outs.append((yg * nw[:, gs]).astype(o_ref.dtype))
    o_ref[...] = jnp.concatenate(outs, axis=-1)


def _ssd(zdt, xbc, dt_bias, a_log, d_skip, norm_w, *, nbatch, nblk):
    T = zdt.shape[0]
    row = lambda b, n: b * nblk + n
    pad_h = lambda t: jnp.pad(t.astype(F32), (0, SSD_DT_PAD - SSD_HEADS)).reshape(1, SSD_DT_PAD)
    head = lax.broadcasted_iota(jnp.int32, (SSD_DT_PAD, SSD_INNER), 0)
    colh = lax.broadcasted_iota(jnp.int32, (SSD_DT_PAD, SSD_INNER), 1) // SSD_HEAD_DIM
    expand = jnp.tile((head == colh).astype(BF16), (2, 1))
    full = lambda shape: pl.BlockSpec(shape, lambda b, n: (0, 0))
    b_blk = SSD_INNER // SSD_BC_COLS
    dt_blk = SSD_INNER // SSD_DT_PAD
    return pl.pallas_call(
        _ssd_kernel,
        grid=(nbatch, nblk),
        in_specs=[pl.BlockSpec((BLOCK, SSD_INNER), lambda b, n: (row(b, n), 0)),
                  pl.BlockSpec((BLOCK, SSD_DT_PAD), lambda b, n: (row(b, n), dt_blk)),
                  pl.BlockSpec((BLOCK, SSD_INNER), lambda b, n: (row(b, n), 0)),
                  pl.BlockSpec((BLOCK, SSD_BC_COLS), lambda b, n: (row(b, n), b_blk)),
                  pl.BlockSpec((BLOCK, SSD_BC_COLS), lambda b, n: (row(b, n), b_blk + 1)),
                  full((1, SSD_DT_PAD)), full((1, SSD_DT_PAD)), full((1, SSD_INNER)), full((1, SSD_INNER)),
                  full((2 * SSD_DT_PAD, SSD_INNER))],
        out_specs=pl.BlockSpec((BLOCK, SSD_INNER), lambda b, n: (row(b, n), 0)),
        out_shape=jax.ShapeDtypeStruct((T, SSD_INNER), BF16),
        scratch_shapes=[pltpu.VMEM((SSD_GROUPS, SSD_STATE, SSD_GROUP_COLS), F32)],
        compiler_params=_params("arbitrary", "arbitrary"),
        name="ssd",
    )(zdt, zdt, xbc, xbc, xbc, pad_h(dt_bias), pad_h(a_log),
      jnp.repeat(d_skip.astype(F32), SSD_HEAD_DIM).reshape(1, SSD_INNER), norm_w.reshape(1, SSD_INNER), expand)


def _pack_ar_w_in(w):
    return jnp.pad(w, ((0, 0), (0, LORA_PAD - LORA_COLS))).astype(BF16)


def _pack_lora_up(w_up, a_up, g_up):
    assert DECAY_LORA + AAA_LORA == LANES
    zero = jnp.zeros((DECAY_LORA, RWKV_DIM), F32)
    wup_wa = jnp.concatenate([jnp.concatenate([w_up, zero], axis=1), jnp.concatenate([zero, a_up], axis=1)], axis=0)
    wup_g = jnp.pad(g_up, ((0, LORA_PAD - LANES - GATE_LORA), (0, 0)))
    return wup_wa.astype(BF16), wup_g.astype(BF16)


def kernel(x, meta_tokens, mix_norm_w, ffn_norm_w, ar_w_in, ar_shift_mu, attn_q_norm_w, attn_k_norm_w, attn_sinks, rwkv_w0, rwkv_w_up, rwkv_a0, rwkv_a_up, rwkv_g_up, rwkv_k_k, rwkv_k_a, rwkv_r_k, rwkv_ln_w, rwkv_ln_b, ar_w_out, ssd_w_in, ssd_conv_w, ssd_conv_b, ssd_dt_bias, ssd_a_log, ssd_d, ssd_norm_w, ssd_w_out, ffn_w_up, ffn_conv_w, ffn_conv_b, ffn_w_down):
    nbatch, seq, d = x.shape
    assert d == D_MODEL and seq % BLOCK == 0
    seq_p = N_PAD + N_META + seq
    nblk = seq_p // BLOCK
    depth = mix_norm_w.shape[0]
    res = jnp.concatenate([
        jnp.zeros((nbatch, N_PAD, d), x.dtype),
        jnp.broadcast_to(meta_tokens.astype(x.dtype)[None], (nbatch, N_META, d)),
        x], axis=1).reshape(nbatch * seq_p, d)
    kw = dict(seq_p=seq_p, nbatch=nbatch)
    for layer in range(depth):
        i = layer // 2
        if layer % 2 == 0:
            zin = _norm_matmul(res, mix_norm_w[layer], _pack_ar_w_in(ar_w_in[i]), **kw)
            attn = _attention(zin, attn_sinks[i].astype(F32), attn_q_norm_w[i], attn_k_norm_w[i],
                              nbatch=nbatch, nblk=nblk)
            mu = ar_shift_mu[i]
            mu_rkv = mu[:3 * RWKV_DIM].reshape(1, -1)
            mu_lora = jnp.pad(mu[3 * RWKV_DIM:], (0, LORA_PAD - LORA_COLS)).reshape(1, -1)
            vecs = jnp.stack([rwkv_w0[i], rwkv_a0[i], rwkv_k_k[i], rwkv_k_a[i], rwkv_r_k[i].reshape(-1),
                              rwkv_ln_w[i], rwkv_ln_b[i], jnp.zeros((RWKV_DIM,), F32)]).astype(F32)
            tm = _rwkv(zin, mu_rkv, mu_lora, *_pack_lora_up(rwkv_w_up[i], rwkv_a_up[i], rwkv_g_up[i]), vecs,
                       nbatch=nbatch, nblk=nblk)
            res = _proj_residual([attn, tm], ar_w_out[i].astype(BF16), res)
        else:
            w_in = ssd_w_in[i]
            xbc0, dt0 = SSD_INNER, 2 * SSD_INNER + 2 * SSD_BC_COLS
            w_zdt = jnp.concatenate([w_in[:, :xbc0], w_in[:, dt0:],
                                     jnp.zeros((d, SSD_DT_PAD - SSD_HEADS), w_in.dtype)], axis=1).astype(BF16)
            zdt, xbc = _ssd_in_proj(res, mix_norm_w[layer], w_zdt, w_in[:, xbc0:dt0].astype(BF16), ssd_conv_w[i],
                                    ssd_conv_b[i], **kw)
            y = _ssd(zdt, xbc, ssd_dt_bias[i], ssd_a_log[i], ssd_d[i], ssd_norm_w[i], nbatch=nbatch, nblk=nblk)
            res = _proj_residual([y], ssd_w_out[i].astype(BF16), res)
        res = _ffn(res, ffn_norm_w[layer], ffn_w_up[layer].astype(BF16), ffn_conv_w[layer], ffn_conv_b[layer],
                   ffn_w_down[layer].astype(BF16), **kw)
    return res.reshape(nbatch, seq_p, d)[:, N_PAD + N_META:]
```

```python
import functools
import math

import jax
import jax.numpy as jnp
from jax import lax
from jax.experimental import pallas as pl
from jax.experimental.pallas import tpu as pltpu

F32 = jnp.float32
BF16 = jnp.bfloat16

D_MODEL = 2048
N_META = 16
BLOCK = 128
N_PAD = BLOCK - N_META
NORM_EPS = 1e-6

HEAD_DIM = 64
ATTN_HEADS = 16
ATTN_KV_HEADS = 4
ATTN_GROUP = 4
WINDOW = 128
QK_EPS = 1e-6
Q_COLS = ATTN_HEADS * HEAD_DIM
KV_COLS = ATTN_KV_HEADS * HEAD_DIM
ATTN_COLS = Q_COLS + 2 * KV_COLS

RWKV_DIM = 1024
RWKV_HEAD = 64
DECAY_LORA = 64
AAA_LORA = 64
GATE_LORA = 160
LORA_COLS = DECAY_LORA + AAA_LORA + GATE_LORA
LORA_PAD = 384
RWKV_GN_EPS = 64e-5
DECAY_SCALE = math.exp(-0.5)
RWKV_CHUNK = 64
AR_COLS = ATTN_COLS + 3 * RWKV_DIM + LORA_PAD

SSD_INNER = 4096
SSD_HEAD_DIM = 64
SSD_HEADS = 64
SSD_GROUPS = 8
SSD_HPG = 8
SSD_STATE = 128
SSD_CONV = 4
SSD_GROUP_COLS = SSD_INNER // SSD_GROUPS
SSD_BC_COLS = SSD_GROUPS * SSD_STATE
SSD_DT_PAD = 128
SSD_NORM_EPS = 1e-5

FFN_DIM = 5632
FFN_CONV = 3
FFN_CHUNK = 512
CONV_SLAB = 256

LANES = 128
SUBLANES = 8
VMEM_LIMIT = 56 * 1024 * 1024


def _params(*sem):
    return pltpu.CompilerParams(dimension_semantics=sem, vmem_limit_bytes=VMEM_LIMIT)


def _row_tile(total, target):
    best = None
    for t in range(16, min(total, target) + 1, 16):
        if total % t == 0:
            best = t
    assert best is not None, (total, target)
    return best


def _dot(a, b):
    return jnp.dot(a.astype(BF16), b.astype(BF16), preferred_element_type=F32)


def _dot_nt(a, b):
    return lax.dot_general(a.astype(BF16), b.astype(BF16), (((1,), (1,)), ((), ())),
                           preferred_element_type=F32)


def _dot_tn(a, b):
    return lax.dot_general(a.astype(BF16), b.astype(BF16), (((0,), (0,)), ((), ())),
                           preferred_element_type=F32)


def _split3(x):
    hi = x.astype(BF16)
    r1 = x - hi.astype(F32)
    mid = r1.astype(BF16)
    lo = (r1 - mid.astype(F32)).astype(BF16)
    return hi, mid, lo


def _dot_exact_rhs(sel_bf16, x):
    return jnp.dot(jnp.concatenate([sel_bf16] * 3, axis=1), jnp.concatenate(_split3(x), axis=0),
                   preferred_element_type=F32)


def _half_sums(x):
    hi = x.astype(BF16)
    lo = (x - hi.astype(F32)).astype(BF16)
    i = lax.broadcasted_iota(jnp.int32, (2 * LANES, LANES), 0) & (LANES - 1)
    j = lax.broadcasted_iota(jnp.int32, (2 * LANES, LANES), 1)
    ones2 = jnp.where((i < LANES // 2) == (j < LANES // 2), 1.0, 0.0).astype(BF16)
    return jnp.dot(jnp.concatenate([hi, lo], axis=1), ones2, preferred_element_type=F32)


def _sigmoid(x):
    return 1.0 / (1.0 + jnp.exp(-x))


def _softplus(x):
    return jnp.maximum(x, 0.0) + jnp.log(1.0 + jnp.exp(-jnp.abs(x)))


def _valid_rows(row0, nrows, seq_p, nbatch):
    rows = row0 + lax.broadcasted_iota(jnp.int32, (nrows, 1), 0)
    pos = rows
    for b in range(1, nbatch):
        pos = jnp.where(rows >= b * seq_p, rows - b * seq_p, pos)
    return pos >= N_PAD


def _masked_rmsnorm(x, w, row0, seq_p, nbatch):
    ms = jnp.mean(x * x, axis=-1, keepdims=True)
    y = x * lax.rsqrt(ms + NORM_EPS) * w
    return jnp.where(_valid_rows(row0, x.shape[0], seq_p, nbatch), y, 0.0)


def _norm_mm_shift_kernel(x_ref, nw_ref, w_ref, mu_ref, o_ref, carry_scr, *, tm, seq_p, nbatch, shift0):
    m = pl.program_id(0)

    @pl.when(m == 0)
    def _():
        carry_scr[...] = jnp.zeros(carry_scr.shape, F32)

    valid = _valid_rows(m * tm, tm, seq_p, nbatch)
    h = _masked_rmsnorm(x_ref[...], nw_ref[...], m * tm, seq_p, nbatch)
    acc = jnp.dot(h.astype(BF16), w_ref[...], preferred_element_type=F32)
    o_ref[:, :shift0] = acc[:, :shift0]
    z = acc[:, shift0:]
    rows = lax.broadcasted_iota(jnp.int32, (tm, 1), 0)
    prev = jnp.where(rows == 0, carry_scr[SUBLANES - 1:SUBLANES], pltpu.roll(z, 1, axis=0))
    carry_scr[...] = z[tm - SUBLANES:]
    o_ref[:, shift0:] = jnp.where(valid, z + (prev - z) * mu_ref[...], 0.0)


def _norm_matmul_shift(res, norm_w, w_bf16, mu, *, shift0, seq_p, nbatch):
    T, D = res.shape
    N = w_bf16.shape[1]
    tm = _row_tile(T, 384)
    return pl.pallas_call(
        functools.partial(_norm_mm_shift_kernel, tm=tm, seq_p=seq_p, nbatch=nbatch, shift0=shift0),
        grid=(T // tm,),
        in_specs=[pl.BlockSpec((tm, D), lambda m: (m, 0)),
                  pl.BlockSpec((1, D), lambda m: (0, 0)),
                  pl.BlockSpec((D, N), lambda m: (0, 0), pipeline_mode=pl.Buffered(1)),
                  pl.BlockSpec((1, N - shift0), lambda m: (0, 0))],
        out_specs=pl.BlockSpec((tm, N), lambda m: (m, 0)),
        out_shape=jax.ShapeDtypeStruct((T, N), F32),
        scratch_shapes=[pltpu.VMEM((SUBLANES, N - shift0), F32)],
        compiler_params=_params("arbitrary"),
        name="norm_in_proj",
    )(res, norm_w.reshape(1, D), w_bf16, mu)


def _ssd_in_proj_kernel(x_ref, nw_ref, wz_ref, wx_ref, cw_ref, cb_ref, oz_ref, ox_ref, h_scr, carry_scr,
                        *, tm, seq_p, nbatch):
    m = pl.program_id(0)
    n = pl.program_id(1)

    @pl.when(n == 0)
    def _():
        h = _masked_rmsnorm(x_ref[...], nw_ref[...], m * tm, seq_p, nbatch)
        h_scr[...] = h.astype(BF16)

    @pl.when(m == 0)
    def _():
        carry_scr[n] = jnp.zeros(carry_scr.shape[1:], F32)

    valid = _valid_rows(m * tm, tm, seq_p, nbatch)
    h = h_scr[...]

    def conv_epilogue(sl, xc):
        full = jnp.concatenate([carry_scr[n, :, sl], xc], axis=0)
        carry_scr[n, :, sl] = xc[tm - SUBLANES:]
        acc = cb_ref[:, sl]
        for j in range(SSD_CONV):
            k = SSD_CONV - 1 - j
            term = xc if k == 0 else pltpu.roll(full, k, axis=0)[SUBLANES:]
            acc = acc + cw_ref[j:j + 1, sl] * term
        ox_ref[:, sl] = jnp.where(valid, acc * _sigmoid(acc), 0.0)

    nslab = ox_ref.shape[1] // CONV_SLAB
    zpieces = [(c, min(c + CONV_SLAB, oz_ref.shape[1])) for c in range(0, oz_ref.shape[1], CONV_SLAB)]
    zdone = 0
    pending = None
    for s in range(nslab):
        sl = slice(s * CONV_SLAB, (s + 1) * CONV_SLAB)
        xc = jnp.dot(h, wx_ref[:, sl], preferred_element_type=F32)
        ztarget = (len(zpieces) * (s + 1)) // nslab
        for c0, c1 in zpieces[zdone:ztarget]:
            oz_ref[:, c0:c1] = jnp.dot(h, wz_ref[:, c0:c1], preferred_element_type=F32)
        zdone = ztarget
        if pending is not None:
            conv_epilogue(*pending)
        pending = (sl, xc)
    conv_epilogue(*pending)


def _ssd_in_proj(res, norm_w, w_zdt, w_xbc, conv_w, conv_b, *, seq_p, nbatch):
    T, D = res.shape
    nz, nx = w_zdt.shape[1], w_xbc.shape[1]
    steps = 3
    tz, tx = nz // steps, nx // steps
    assert tz * steps == nz and tx * steps == nx and tz % LANES == 0 and tx % CONV_SLAB == 0
    tm = _row_tile(T, 384)
    return pl.pallas_call(
        functools.partial(_ssd_in_proj_kernel, tm=tm, seq_p=seq_p, nbatch=nbatch),
        grid=(T // tm, steps),
        in_specs=[pl.BlockSpec((tm, D), lambda m, n: (m, 0)),
                  pl.BlockSpec((1, D), lambda m, n: (0, 0)),
                  pl.BlockSpec((D, tz), lambda m, n: (0, n)),
                  pl.BlockSpec((D, tx), lambda m, n: (0, n)),
                  pl.BlockSpec((SSD_CONV, tx), lambda m, n: (0, n)),
                  pl.BlockSpec((1, tx), lambda m, n: (0, n))],
        out_specs=[pl.BlockSpec((tm, tz), lambda m, n: (m, n)),
                   pl.BlockSpec((tm, tx), lambda m, n: (m, n))],
        out_shape=[jax.ShapeDtypeStruct((T, nz), F32), jax.ShapeDtypeStruct((T, nx), F32)],
        scratch_shapes=[pltpu.VMEM((tm, D), BF16), pltpu.VMEM((steps, SUBLANES, tx), F32)],
        compiler_params=_params("arbitrary", "arbitrary"),
        name="ssd_in_proj",
    )(res, norm_w.reshape(1, D), w_zdt, w_xbc, conv_w, conv_b.reshape(1, nx))


def _proj_res_kernel(*refs, n_lhs, tm, seq_p, nbatch):
    lhs = refs[:n_lhs]
    ws = refs[n_lhs:2 * n_lhs]
    res_ref, nw_ref, o_ref, h_ref = refs[2 * n_lhs:]
    acc = res_ref[...]
    for l, w in zip(lhs, ws):
        acc = acc + jnp.dot(l[...], w[...], preferred_element_type=F32)
    o_ref[...] = acc
    h_ref[...] = _masked_rmsnorm(acc, nw_ref[...], pl.program_id(0) * tm, seq_p, nbatch).astype(h_ref.dtype)


def _proj_residual(lhs_list, w_bf16, res, next_norm_w, *, seq_p, nbatch):
    T, D = res.shape
    n_lhs = len(lhs_list)
    kw = lhs_list[0].shape[1]
    assert all(l.shape == (T, kw) for l in lhs_list) and w_bf16.shape == (n_lhs * kw, D)
    tm = _row_tile(T, 384)
    in_specs = [pl.BlockSpec((tm, kw), lambda m: (m, 0)) for _ in range(n_lhs)]
    in_specs += [pl.BlockSpec((kw, D), functools.partial(lambda i, m: (i, 0), i), pipeline_mode=pl.Buffered(1))
                 for i in range(n_lhs)]
    in_specs += [pl.BlockSpec((tm, D), lambda m: (m, 0)), pl.BlockSpec((1, D), lambda m: (0, 0))]
    return pl.pallas_call(
        functools.partial(_proj_res_kernel, n_lhs=n_lhs, tm=tm, seq_p=seq_p, nbatch=nbatch),
        grid=(T // tm,),
        in_specs=in_specs,
        out_specs=[pl.BlockSpec((tm, D), lambda m: (m, 0)), pl.BlockSpec((tm, D), lambda m: (m, 0))],
        out_shape=[jax.ShapeDtypeStruct((T, D), F32), jax.ShapeDtypeStruct((T, D), BF16)],
        compiler_params=_params("arbitrary"),
        name="out_proj_residual",
    )(*lhs_list, *([w_bf16] * n_lhs), res, next_norm_w.reshape(1, D))


def _ffn_kernel(res_ref, h_ref, wg_ref, wv_ref, cw_ref, cb_ref, wd_ref, o_ref, carry_scr, *, tm):
    m = pl.program_id(0)
    f = pl.program_id(1)

    @pl.when(f == 0)
    def _():
        o_ref[...] = res_ref[...]

    @pl.when(m == 0)
    def _():
        carry_scr[f] = jnp.zeros(carry_scr.shape[1:], F32)

    h = h_ref[...]
    gate = jnp.dot(h, wg_ref[...], preferred_element_type=F32)
    val = jnp.dot(h, wv_ref[...], preferred_element_type=F32)
    prev = carry_scr[f]
    rows = lax.broadcasted_iota(jnp.int32, (tm, 1), 0)
    g1 = jnp.where(rows == 0, prev[7:8], pltpu.roll(gate, 1, axis=0))
    g2 = jnp.where(rows == 0, prev[6:7], jnp.where(rows == 1, prev[7:8], pltpu.roll(gate, 2, axis=0)))
    carry_scr[f] = gate[tm - SUBLANES:]
    cw = cw_ref[...]
    pre = cb_ref[...] + cw[0:1] * g2 + cw[1:2] * g1 + cw[2:3] * gate
    act = pre * _sigmoid(pre) * val
    o_ref[...] += jnp.dot(act.astype(BF16), wd_ref[...], preferred_element_type=F32)


def _ffn(res, h_bf16, w_up_bf16, conv_w, conv_b, w_down_bf16):
    T, D = res.shape
    tm = _row_tile(T, 768)
    tf = FFN_CHUNK
    nf = FFN_DIM // tf
    return pl.pallas_call(
        functools.partial(_ffn_kernel, tm=tm),
        grid=(T // tm, nf),
        in_specs=[pl.BlockSpec((tm, D), lambda m, f: (m, 0)),
                  pl.BlockSpec((tm, D), lambda m, f: (m, 0)),
                  pl.BlockSpec((D, tf), lambda m, f: (0, f)),
                  pl.BlockSpec((D, tf), lambda m, f: (0, nf + f)),
                  pl.BlockSpec((FFN_CONV, tf), lambda m, f: (0, f)),
                  pl.BlockSpec((1, tf), lambda m, f: (0, f)),
                  pl.BlockSpec((tf, D), lambda m, f: (f, 0))],
        out_specs=pl.BlockSpec((tm, D), lambda m, f: (m, 0)),
        out_shape=jax.ShapeDtypeStruct((T, D), F32),
        scratch_shapes=[pltpu.VMEM((nf, SUBLANES, tf), F32)],
        compiler_params=_params("arbitrary", "arbitrary"),
        name="ffn",
    )(res, h_bf16, w_up_bf16, w_up_bf16, conv_w, conv_b.reshape(1, FFN_DIM), w_down_bf16)


def _attn_kernel(sinks_ref, q_ref, kvc_ref, kvp_ref, kvm_ref, qw_ref, kw_ref, o_ref):
    n = pl.program_id(1)
    scale = HEAD_DIM ** -0.5
    lane = lax.broadcasted_iota(jnp.int32, (BLOCK, LANES), 1)
    lo_half = lane < HEAD_DIM

    kvs = (kvp_ref[...], kvc_ref[...], kvm_ref[...])
    q = q_ref[...]
    nq, nk = Q_COLS // LANES, KV_COLS // LANES
    tiles = [q[:, c * LANES:(c + 1) * LANES] for c in range(nq)]
    tiles += [kv[:, c * LANES:(c + 1) * LANES] for kv in kvs for c in range(nk)]
    sq = jnp.concatenate([t * t for t in tiles], axis=0)
    ms = _half_sums(sq) * (1.0 / HEAD_DIM)
    inv = lax.rsqrt(ms + QK_EPS)
    qw = qw_ref[...] * scale
    kw = kw_ref[...]
    qn = [tiles[c] * inv[c * BLOCK:(c + 1) * BLOCK] * qw for c in range(nq)]
    kn = [[tiles[nq + i * nk + c] * inv[(nq + i * nk + c) * BLOCK:(nq + i * nk + c + 1) * BLOCK] * kw
           for i in range(3)] for c in range(nk)]

    qi = lax.broadcasted_iota(jnp.int32, (BLOCK, 3 * BLOCK), 0)
    kj = lax.broadcasted_iota(jnp.int32, (BLOCK, 3 * BLOCK), 1)
    j_prev, j_cur, j_meta = kj, kj - BLOCK, kj - 2 * BLOCK
    first_prev = jnp.where(n == 0, BLOCK, jnp.where(n == 1, N_PAD, 0))
    first_cur = jnp.where(n == 0, N_PAD, 0)
    vis_prev = (kj < BLOCK) & (j_prev > qi) & (j_prev >= first_prev)
    vis_cur = (kj >= BLOCK) & (j_cur <= qi) & (j_cur >= first_cur)
    vis_meta = (j_meta >= N_PAD) & (n * BLOCK + qi - j_meta >= WINDOW)
    bias = jnp.where(vis_prev | vis_cur | vis_meta, 0.0, -jnp.inf)
    bias4 = jnp.concatenate([bias] * ATTN_GROUP, axis=0)
    grow = lax.broadcasted_iota(jnp.int32, (ATTN_GROUP * BLOCK, 1), 0)

    def dup_half(x, upper):
        swapped = pltpu.roll(x, HEAD_DIM, axis=1)
        lane_r = lax.broadcasted_iota(jnp.int32, x.shape, 1) < HEAD_DIM
        return jnp.where(lane_r, swapped, x) if upper else jnp.where(lane_r, x, swapped)

    outs = []
    for c in range(nk):
        k_col = jnp.concatenate(kn[c], axis=0)
        v_col = jnp.concatenate([kv[:, KV_COLS + c * LANES:KV_COLS + (c + 1) * LANES] for kv in kvs], axis=0)
        for half in range(2):
            h = 2 * c + half
            k_dup = dup_half(k_col, half == 1).astype(BF16)
            v_dup = dup_half(v_col, half == 1).astype(BF16)
            lhs = jnp.concatenate(
                [jnp.where(lo_half if e == 0 else jnp.logical_not(lo_half), qn[2 * h + pr], 0.0)
                 for pr in range(2) for e in range(2)], axis=0)
            s = _dot_nt(lhs, k_dup) + bias4
            sink = sinks_ref[ATTN_GROUP * h]
            for g in range(1, ATTN_GROUP):
                sink = jnp.where(grow >= g * BLOCK, sinks_ref[ATTN_GROUP * h + g], sink)
            mx = jnp.maximum(jnp.max(s, axis=-1, keepdims=True), sink)
            p = jnp.exp(s - mx)
            den = jnp.sum(p, axis=-1, keepdims=True) + jnp.exp(sink - mx)
            o = jnp.dot(p.astype(BF16), v_dup, preferred_element_type=F32) / den
            for pr in range(2):
                r0 = 2 * pr * BLOCK
                outs.append(jnp.where(lo_half, o[r0:r0 + BLOCK], o[r0 + BLOCK:r0 + 2 * BLOCK]))
    o_ref[...] = jnp.concatenate(outs, axis=-1).astype(o_ref.dtype)


def _attention(zin, sinks, q_norm_w, k_norm_w, *, nbatch, nblk):
    T = zin.shape[0]
    q_blk = 0
    kv_blk = Q_COLS // (2 * KV_COLS)
    row = lambda b, n: b * nblk + n
    return pl.pallas_call(
        _attn_kernel,
        grid=(nbatch, nblk),
        in_specs=[pl.BlockSpec(memory_space=pltpu.SMEM),
                  pl.BlockSpec((BLOCK, Q_COLS), lambda b, n: (row(b, n), q_blk)),
                  pl.BlockSpec((BLOCK, 2 * KV_COLS), lambda b, n: (row(b, n), kv_blk)),
                  pl.BlockSpec((BLOCK, 2 * KV_COLS), lambda b, n: (row(b, jnp.maximum(n - 1, 0)), kv_blk)),
                  pl.BlockSpec((BLOCK, 2 * KV_COLS), lambda b, n: (row(b, 0), kv_blk)),
                  pl.BlockSpec((1, LANES), lambda b, n: (0, 0)),
                  pl.BlockSpec((1, LANES), lambda b, n: (0, 0))],
        out_specs=pl.BlockSpec((BLOCK, Q_COLS), lambda b, n: (row(b, n), 0)),
        out_shape=jax.ShapeDtypeStruct((T, Q_COLS), BF16),
        compiler_params=_params("arbitrary", "arbitrary"),
        name="swa_attention",
    )(sinks, zin, zin, zin, zin, jnp.tile(q_norm_w, LANES // HEAD_DIM).reshape(1, LANES),
      jnp.tile(k_norm_w, LANES // HEAD_DIM).reshape(1, LANES))


def _pair_blockdiag(x):
    xb = x.astype(BF16)
    lane = lax.broadcasted_iota(jnp.int32, xb.shape, 2)
    zero = jnp.zeros_like(xb)
    return jnp.concatenate([jnp.where(lane < RWKV_HEAD, xb, zero), jnp.where(lane >= RWKV_HEAD, xb, zero)], axis=1)


def _bmm(a, b_bf16):
    return lax.dot_general(a.astype(BF16), b_bf16, (((2,), (1,)), ((0,), (0,))), preferred_element_type=F32)


def _bmm_nt(a, b_bf16):
    return lax.dot_general(a.astype(BF16), b_bf16, (((2,), (2,)), ((0,), (0,))), preferred_element_type=F32)


def _pair_pick(full):
    lane = lax.broadcasted_iota(jnp.int32, (full.shape[0], RWKV_HEAD, LANES), 2)
    return jnp.where(lane < RWKV_HEAD, full[:, :RWKV_HEAD], full[:, RWKV_HEAD:])


def _to_pairs(x, nchunk):
    L = RWKV_CHUNK
    return jnp.stack([x[c * L:(c + 1) * L, p * LANES:(p + 1) * LANES]
                      for c in range(nchunk) for p in range(x.shape[1] // LANES)], axis=0)


def _pair_segsum(*xs):
    npair = xs[0].shape[1] // LANES
    rows = xs[0].shape[0]
    stacked = jnp.concatenate([x[:, p * LANES:(p + 1) * LANES] for x in xs for p in range(npair)], axis=0)
    s = _half_sums(stacked)
    outs = [jnp.concatenate([s[(i * npair + p) * rows:(i * npair + p + 1) * rows] for p in range(npair)], axis=-1)
            for i in range(len(xs))]
    return outs[0] if len(xs) == 1 else outs


def _rwkv_kernel(rkv_a_ref, rkv_b_ref, lora_ref, wup_wa_ref, wup_g_ref, vec_ref, o_ref, s_scr, *, tb):
    n = pl.program_id(1)
    L = RWKV_CHUNK
    npair = RWKV_DIM // LANES

    @pl.when(n == 0)
    def _():
        s_scr[...] = jnp.zeros(s_scr.shape, F32)

    rows = lax.broadcasted_iota(jnp.int32, (tb, 1), 0)
    valid = (n > 0) | (rows >= N_PAD)
    zr = jnp.concatenate([rkv_a_ref[...], rkv_b_ref[...]], axis=1)
    zl = lora_ref[...]
    r, k, v = zr[:, :RWKV_DIM], zr[:, RWKV_DIM:2 * RWKV_DIM], zr[:, 2 * RWKV_DIM:]

    z_wa = zl[:, :LANES]
    col = lax.broadcasted_iota(jnp.int32, z_wa.shape, 1)
    act_wa = jnp.where(col < DECAY_LORA, jnp.tanh(z_wa), z_wa)
    up_wa = jnp.dot(act_wa.astype(BF16), wup_wa_ref[...], preferred_element_type=F32)
    g = jnp.dot(_sigmoid(zl[:, LANES:]).astype(BF16), wup_g_ref[...], preferred_element_type=F32)
    vec = vec_ref[...]
    w0, a0, k_k, k_a, r_k, ln_w, ln_b = (vec[i:i + 1] for i in range(7))
    w_log = jnp.where(valid, -DECAY_SCALE * _sigmoid(w0 + up_wa[:, :RWKV_DIM]), 0.0)
    a = _sigmoid(a0 + up_wa[:, RWKV_DIM:])

    kk = k * k_k
    k2 = k * (1.0 + (a - 1.0) * k_a)
    kk_sq, bonus = _pair_segsum(kk * kk, r * k2 * r_k)
    kk = kk * lax.rsqrt(jnp.maximum(kk_sq, 1e-24))
    kka = kk * a

    nchunk = tb // L
    nb = nchunk * npair
    t_i = lax.broadcasted_iota(jnp.int32, (nb, L, LANES), 1)
    s_i = lax.broadcasted_iota(jnp.int32, (nb, L, LANES), 2) & (RWKV_HEAD - 1)
    strict = s_i < t_i
    incl = s_i <= t_i
    same16 = (t_i >> 4) == (s_i >> 4)
    same32 = (t_i >> 5) == (s_i >> 5)
    eye = jnp.where(s_i == t_i, 1.0, 0.0)
    ti2 = lax.broadcasted_iota(jnp.int32, (tb, tb), 0)
    si2 = lax.broadcasted_iota(jnp.int32, (tb, tb), 1)
    lshift = L.bit_length() - 1
    tri_incl = jnp.where((si2 <= ti2) & ((si2 >> lshift) == (ti2 >> lshift)), 1.0, 0.0).astype(BF16)
    cum = _dot_exact_rhs(tri_incl, w_log)
    cum_end = jnp.concatenate(
        [jnp.broadcast_to(cum[(c + 1) * L - 1:(c + 1) * L], (L, RWKV_DIM)) for c in range(nchunk)], axis=0)
    e_w = jnp.exp(cum)
    e_iw = jnp.exp(-cum)
    e_prev = jnp.exp(cum - w_log)
    e_end = jnp.exp(cum_end - cum)
    w_end = jnp.exp(cum_end)
    ah = _to_pairs(-kk * e_prev, nchunk)
    bh = _to_pairs(kka * e_iw, nchunk)
    kh = _to_pairs(k2 * e_iw, nchunk)
    rh = _to_pairs(r * e_w, nchunk)
    bt = _to_pairs(kka * e_end, nchunk)
    kt = _to_pairs(k2 * e_end, nchunk)
    vp = _to_pairs(v, nchunk)
    vbd = _pair_blockdiag(vp)

    ar = jnp.concatenate([ah, rh], axis=1)
    m = _bmm_nt(ar, jnp.concatenate([_pair_blockdiag(bh), _pair_blockdiag(kh)], axis=1))
    a_ab = jnp.where(strict, m[:, :L, :LANES], 0.0)
    a_rb = jnp.where(incl, m[:, L:, :LANES], 0.0)
    a_ak = jnp.where(strict, m[:, :L, LANES:], 0.0)
    a_rk = jnp.where(incl, m[:, L:, LANES:], 0.0)
    kv = _bmm(jnp.concatenate([a_ak, a_rk], axis=1), vbd)
    akv, y_rk = kv[:, :L], kv[:, L:]
    a1 = jnp.where(same16, a_ab, 0.0)
    t1 = eye + a1
    a2 = _bmm(a1, _pair_blockdiag(a1))
    x = _bmm(jnp.concatenate([a2, t1], axis=1), _pair_blockdiag(a2))
    a4, t2 = x[:, :L], t1 + x[:, L:]
    x = _bmm(jnp.concatenate([a4, t2], axis=1), _pair_blockdiag(a4))
    a8, t3 = x[:, :L], t2 + x[:, L:]
    t16 = t3 + _bmm(t3, _pair_blockdiag(a8))
    off32 = jnp.where(same32 & jnp.logical_not(same16), a_ab, 0.0)
    t32 = t16 + _bmm(t16, _pair_blockdiag(_bmm(off32, _pair_blockdiag(t16))))
    off64 = jnp.where(same32, 0.0, a_ab)
    tinv = t32 + _bmm(t32, _pair_blockdiag(_bmm(off64, _pair_blockdiag(t32))))
    tg = _bmm(tinv, jnp.concatenate([_pair_blockdiag(ah), _pair_blockdiag(akv)], axis=2))
    ta, gm = tg[:, :, :LANES], tg[:, :, LANES:]
    rg = _bmm(a_rb, jnp.concatenate([_pair_blockdiag(ta), _pair_blockdiag(gm)], axis=2))
    rt = rh + rg[:, :, :LANES]
    y_intra = rg[:, :, LANES:] + y_rk
    tgv_t = jnp.swapaxes(jnp.concatenate([ta, gm, vp], axis=2), 1, 2)
    pp = _bmm(tgv_t[:, :2 * LANES], bt.astype(BF16))
    phi = _pair_pick(pp[:, :LANES])
    psi = _pair_pick(pp[:, LANES:]) + _pair_pick(_bmm(tgv_t[:, 2 * LANES:], kt.astype(BF16)))
    w_end_p = _to_pairs(w_end, nchunk)[:, :1]

    y_chunks = []
    s = s_scr[...]
    for c in range(nchunk):
        cs = slice(c * npair, (c + 1) * npair)
        y_c = _bmm_nt(rt[cs], _pair_blockdiag(s)) + y_intra[cs]
        s = s * w_end_p[cs] + _bmm(s, _pair_blockdiag(phi[cs])) + psi[cs]
        y_chunks.append(jnp.concatenate([y_c[p] for p in range(npair)], axis=-1))
    s_scr[...] = s
    y = jnp.concatenate(y_chunks, axis=0)

    inv_n = 1.0 / RWKV_HEAD
    mu = _pair_segsum(y) * inv_n
    d = y - mu
    var = _pair_segsum(d * d) * inv_n
    yn = d * lax.rsqrt(var + RWKV_GN_EPS) * ln_w + ln_b
    o_ref[...] = ((yn + bonus * v) * g).astype(o_ref.dtype)


def _rwkv(zin, wup_wa, wup_g, vecs, *, nbatch, nblk):
    T = zin.shape[0]
    tb = BLOCK
    half = 3 * RWKV_DIM // 2
    assert ATTN_COLS == half and (ATTN_COLS + 3 * RWKV_DIM) % LORA_PAD == 0
    lora_blk = (ATTN_COLS + 3 * RWKV_DIM) // LORA_PAD
    row = lambda b, n: b * nblk + n
    return pl.pallas_call(
        functools.partial(_rwkv_kernel, tb=tb),
        grid=(nbatch, nblk),
        in_specs=[pl.BlockSpec((tb, half), lambda b, n: (row(b, n), 1)),
                  pl.BlockSpec((tb, half), lambda b, n: (row(b, n), 2)),
                  pl.BlockSpec((tb, LORA_PAD), lambda b, n: (row(b, n), lora_blk)),
                  pl.BlockSpec((LANES, 2 * RWKV_DIM), lambda b, n: (0, 0)),
                  pl.BlockSpec((LORA_PAD - LANES, RWKV_DIM), lambda b, n: (0, 0)),
                  pl.BlockSpec((SUBLANES, RWKV_DIM), lambda b, n: (0, 0))],
        out_specs=pl.BlockSpec((tb, RWKV_DIM), lambda b, n: (row(b, n), 0)),
        out_shape=jax.ShapeDtypeStruct((T, RWKV_DIM), BF16),
        scratch_shapes=[pltpu.VMEM((RWKV_DIM // LANES, RWKV_HEAD, LANES), F32)],
        compiler_params=_params("arbitrary", "arbitrary"),
        name="rwkv7",
    )(zin, zin, zin, wup_wa, wup_g, vecs)


def _ssd_kernel(z_ref, dt_ref, x_ref, b_ref, c_ref, dtb_ref, alog_ref, dskip_ref, nw_ref, exp_ref, o_ref, st_scr):
    n = pl.program_id(1)

    @pl.when(n == 0)
    def _():
        st_scr[...] = jnp.zeros(st_scr.shape, F32)

    rows = lax.broadcasted_iota(jnp.int32, (BLOCK, 1), 0)
    valid = (n > 0) | (rows >= N_PAD)
    x = x_ref[...]
    bm = b_ref[...]
    cm = c_ref[...]

    dt = jnp.where(valid, _softplus(dt_ref[...] + dtb_ref[...]), 0.0)
    adt = dt * (-jnp.exp(alog_ref[...]))
    ti = lax.broadcasted_iota(jnp.int32, (BLOCK, BLOCK), 0)
    si = lax.broadcasted_iota(jnp.int32, (BLOCK, BLOCK), 1)
    causal = si <= ti
    cum = _dot_exact_rhs(jnp.where(causal, 1.0, 0.0).astype(BF16), adt)
    cum_t = cum.T
    ecum = jnp.exp(cum)
    dt_end = dt * jnp.exp(cum[BLOCK - 1:BLOCK] - cum)
    e_hi = ecum.astype(BF16)
    e_lo = (ecum - e_hi.astype(F32)).astype(BF16)
    ecum_x = jnp.dot(jnp.concatenate([e_hi, e_lo], axis=1), exp_ref[...], preferred_element_type=F32)
    ex = jnp.dot(jnp.concatenate([dt, dt_end], axis=0).astype(BF16), exp_ref[:SSD_DT_PAD],
                 preferred_element_type=F32)
    xdt = x * ex[:BLOCK]
    xend = x * ex[BLOCK:]
    lane = lax.broadcasted_iota(jnp.int32, (BLOCK, LANES), 1)
    lo_half = lane < SSD_HEAD_DIM

    y_groups = []
    for g in range(SSD_GROUPS):
        gs = slice(g * SSD_GROUP_COLS, (g + 1) * SSD_GROUP_COLS)
        bg = bm[:, g * SSD_STATE:(g + 1) * SSD_STATE]
        cg = cm[:, g * SSD_STATE:(g + 1) * SSD_STATE]
        cb = _dot_nt(cg, bg)
        st = st_scr[g]
        y_inter = _dot(cg, st) * ecum_x[:, gs]
        y_pairs = []
        for q in range(SSD_HPG // 2):
            mats = []
            for j in (g * SSD_HPG + 2 * q, g * SSD_HPG + 2 * q + 1):
                seg = cum[:, j:j + 1] - cum_t[j:j + 1, :]
                mats.append(cb * jnp.exp(jnp.where(causal, seg, -jnp.inf)))
            ll = jnp.concatenate(mats, axis=1).astype(BF16)
            c0 = g * SSD_GROUP_COLS + q * LANES
            xp = xdt[:, c0:c0 + LANES]
            bdx = jnp.concatenate([jnp.where(lo_half, xp, 0.0), jnp.where(lo_half, 0.0, xp)], axis=0)
            y_pairs.append(jnp.dot(ll, bdx.astype(BF16), preferred_element_type=F32))
        y_intra = jnp.concatenate(y_pairs, axis=-1)
        st_scr[g] = st * ecum_x[BLOCK - 1:BLOCK, gs] + _dot_tn(bg, xend[:, gs])
        y_groups.append(y_intra + y_inter + dskip_ref[:, gs] * x[:, gs])
    z = z_ref[...]
    nw = nw_ref[...]
    outs = []
    for g in range(SSD_GROUPS):
        gs = slice(g * SSD_GROUP_COLS, (g + 1) * SSD_GROUP_COLS)
        zg = z[:, gs]
        yg = y_groups[g] * (zg * _sigmoid(zg))
        yg = yg * lax.rsqrt(jnp.mean(yg * yg, axis=-1, keepdims=True) + SSD_NORM_EPS)
        outs.append((yg * nw[:, gs]).astype(o_ref.dtype))
    o_ref[...] = jnp.concatenate(outs, axis=-1)


def _ssd(zdt, xbc, dt_bias, a_log, d_skip, norm_w, *, nbatch, nblk):
    T = zdt.shape[0]
    row = lambda b, n: b * nblk + n
    pad_h = lambda t: jnp.pad(t.astype(F32), (0, SSD_DT_PAD - SSD_HEADS)).reshape(1, SSD_DT_PAD)
    head = lax.broadcasted_iota(jnp.int32, (SSD_DT_PAD, SSD_INNER), 0)
    colh = lax.broadcasted_iota(jnp.int32, (SSD_DT_PAD, SSD_INNER), 1) // SSD_HEAD_DIM
    expand = jnp.tile((head == colh).astype(BF16), (2, 1))
    full = lambda shape: pl.BlockSpec(shape, lambda b, n: (0, 0))
    b_blk = SSD_INNER // SSD_BC_COLS
    dt_blk = SSD_INNER // SSD_DT_PAD
    return pl.pallas_call(
        _ssd_kernel,
        grid=(nbatch, nblk),
        in_specs=[pl.BlockSpec((BLOCK, SSD_INNER), lambda b, n: (row(b, n), 0)),
                  pl.BlockSpec((BLOCK, SSD_DT_PAD), lambda b, n: (row(b, n), dt_blk)),
                  pl.BlockSpec((BLOCK, SSD_INNER), lambda b, n: (row(b, n), 0)),
                  pl.BlockSpec((BLOCK, SSD_BC_COLS), lambda b, n: (row(b, n), b_blk)),
                  pl.BlockSpec((BLOCK, SSD_BC_COLS), lambda b, n: (row(b, n), b_blk + 1)),
                  full((1, SSD_DT_PAD)), full((1, SSD_DT_PAD)), full((1, SSD_INNER)), full((1, SSD_INNER)),
                  full((2 * SSD_DT_PAD, SSD_INNER))],
        out_specs=pl.BlockSpec((BLOCK, SSD_INNER), lambda b, n: (row(b, n), 0)),
        out_shape=jax.ShapeDtypeStruct((T, SSD_INNER), BF16),
        scratch_shapes=[pltpu.VMEM((SSD_GROUPS, SSD_STATE, SSD_GROUP_COLS), F32)],
        compiler_params=_params("arbitrary", "arbitrary"),
        name="ssd",
    )(zdt, zdt, xbc, xbc, xbc, pad_h(dt_bias), pad_h(a_log),
      jnp.repeat(d_skip.astype(F32), SSD_HEAD_DIM).reshape(1, SSD_INNER), norm_w.reshape(1, SSD_INNER), expand)


def _pack_ar_w_in(w):
    return jnp.pad(w, ((0, 0), (0, LORA_PAD - LORA_COLS))).astype(BF16)


def _pack_lora_up(w_up, a_up, g_up):
    assert DECAY_LORA + AAA_LORA == LANES
    zero = jnp.zeros((DECAY_LORA, RWKV_DIM), F32)
    wup_wa = jnp.concatenate([jnp.concatenate([w_up, zero], axis=1), jnp.concatenate([zero, a_up], axis=1)], axis=0)
    wup_g = jnp.pad(g_up, ((0, LORA_PAD - LANES - GATE_LORA), (0, 0)))
    return wup_wa.astype(BF16), wup_g.astype(BF16)


def kernel(x, meta_tokens, mix_norm_w, ffn_norm_w, ar_w_in, ar_shift_mu, attn_q_norm_w, attn_k_norm_w, attn_sinks, rwkv_w0, rwkv_w_up, rwkv_a0, rwkv_a_up, rwkv_g_up, rwkv_k_k, rwkv_k_a, rwkv_r_k, rwkv_ln_w, rwkv_ln_b, ar_w_out, ssd_w_in, ssd_conv_w, ssd_conv_b, ssd_dt_bias, ssd_a_log, ssd_d, ssd_norm_w, ssd_w_out, ffn_w_up, ffn_conv_w, ffn_conv_b, ffn_w_down):
    nbatch, seq, d = x.shape
    assert d == D_MODEL and seq % BLOCK == 0
    seq_p = N_PAD + N_META + seq
    nblk = seq_p // BLOCK
    depth = mix_norm_w.shape[0]
    res = jnp.concatenate([
        jnp.zeros((nbatch, N_PAD, d), x.dtype),
        jnp.broadcast_to(meta_tokens.astype(x.dtype)[None], (nbatch, N_META, d)),
        x], axis=1).reshape(nbatch * seq_p, d)
    kw = dict(seq_p=seq_p, nbatch=nbatch)
    for layer in range(depth):
        i = layer // 2
        if layer % 2 == 0:
            mu = jnp.pad(ar_shift_mu[i], (0, LORA_PAD - LORA_COLS)).reshape(1, -1)
            zin = _norm_matmul_shift(res, mix_norm_w[layer], _pack_ar_w_in(ar_w_in[i]), mu, shift0=ATTN_COLS, **kw)
            attn = _attention(zin, attn_sinks[i].astype(F32), attn_q_norm_w[i], attn_k_norm_w[i],
                              nbatch=nbatch, nblk=nblk)
            vecs = jnp.stack([rwkv_w0[i], rwkv_a0[i], rwkv_k_k[i], rwkv_k_a[i], rwkv_r_k[i].reshape(-1),
                              rwkv_ln_w[i], rwkv_ln_b[i], jnp.zeros((RWKV_DIM,), F32)]).astype(F32)
            tm = _rwkv(zin, *_pack_lora_up(rwkv_w_up[i], rwkv_a_up[i], rwkv_g_up[i]), vecs,
                       nbatch=nbatch, nblk=nblk)
            res, h_ffn = _proj_residual([attn, tm], ar_w_out[i].astype(BF16), res, ffn_norm_w[layer], **kw)
        else:
            w_in = ssd_w_in[i]
            xbc0, dt0 = SSD_INNER, 2 * SSD_INNER + 2 * SSD_BC_COLS
            w_zdt = jnp.concatenate([w_in[:, :xbc0], w_in[:, dt0:],
                                     jnp.zeros((d, SSD_DT_PAD - SSD_HEADS), w_in.dtype)], axis=1).astype(BF16)
            zdt, xbc = _ssd_in_proj(res, mix_norm_w[layer], w_zdt, w_in[:, xbc0:dt0].astype(BF16), ssd_conv_w[i],
                                    ssd_conv_b[i], **kw)
            y = _ssd(zdt, xbc, ssd_dt_bias[i], ssd_a_log[i], ssd_d[i], ssd_norm_w[i], nbatch=nbatch, nblk=nblk)
            res, h_ffn = _proj_residual([y], ssd_w_out[i].astype(BF16), res, ffn_norm_w[layer], **kw)
        res = _ffn(res, h_ffn, ffn_w_up[layer].astype(BF16), ffn_conv_w[layer], ffn_conv_b[layer],
                   ffn_w_down[layer].astype(BF16))
    return res.reshape(nbatch, seq_p, d)[:, N_PAD + N_META:]
```

```python
import functools
import math

import jax
import jax.numpy as jnp
from jax import lax
from jax.experimental import pallas as pl
from jax.experimental.pallas import tpu as pltpu

F32 = jnp.float32
BF16 = jnp.bfloat16

D_MODEL = 2048
N_META = 16
BLOCK = 128
N_PAD = BLOCK - N_META
NORM_EPS = 1e-6

HEAD_DIM = 64
ATTN_HEADS = 16
ATTN_KV_HEADS = 4
ATTN_GROUP = 4
WINDOW = 128
QK_EPS = 1e-6
Q_COLS = ATTN_HEADS * HEAD_DIM
KV_COLS = ATTN_KV_HEADS * HEAD_DIM
ATTN_COLS = Q_COLS + 2 * KV_COLS

RWKV_DIM = 1024
RWKV_HEAD = 64
DECAY_LORA = 64
AAA_LORA = 64
GATE_LORA = 160
LORA_COLS = DECAY_LORA + AAA_LORA + GATE_LORA
LORA_PAD = 384
RWKV_GN_EPS = 64e-5
DECAY_SCALE = math.exp(-0.5)
RWKV_CHUNK = 64
AR_COLS = ATTN_COLS + 3 * RWKV_DIM + LORA_PAD

SSD_INNER = 4096
SSD_HEAD_DIM = 64
SSD_HEADS = 64
SSD_GROUPS = 8
SSD_HPG = 8
SSD_STATE = 128
SSD_CONV = 4
SSD_GROUP_COLS = SSD_INNER // SSD_GROUPS
SSD_BC_COLS = SSD_GROUPS * SSD_STATE
SSD_DT_PAD = 128
SSD_NORM_EPS = 1e-5

FFN_DIM = 5632
FFN_CONV = 3
FFN_CHUNK = 512
ROW_TILE_STREAMED = 768
ROW_TILE_RESIDENT = 384
SSD_PROJ_STEPS = 3
SSD_STEP_CHUNKS = 3
RWKV_STEP_BLOCKS = 3
ATTN_SUBBLOCKS = 3
CONV_SLAB = 256

LANES = 128
SUBLANES = 8
VMEM_LIMIT = 56 * 1024 * 1024


def _params(*sem):
    return pltpu.CompilerParams(dimension_semantics=sem, vmem_limit_bytes=VMEM_LIMIT)


def _row_tile(total, target):
    best = None
    for t in range(16, min(total, target) + 1, 16):
        if total % t == 0:
            best = t
    assert best is not None, (total, target)
    return best


def _dot(a, b):
    return jnp.dot(a.astype(BF16), b.astype(BF16), preferred_element_type=F32)


def _dot_nt(a, b):
    return lax.dot_general(a.astype(BF16), b.astype(BF16), (((1,), (1,)), ((), ())),
                           preferred_element_type=F32)


def _dot_tn(a, b):
    return lax.dot_general(a.astype(BF16), b.astype(BF16), (((0,), (0,)), ((), ())),
                           preferred_element_type=F32)


def _split3(x):
    hi = x.astype(BF16)
    r1 = x - hi.astype(F32)
    mid = r1.astype(BF16)
    lo = (r1 - mid.astype(F32)).astype(BF16)
    return hi, mid, lo


def _dot_exact_rhs(sel_bf16, x):
    return jnp.dot(jnp.concatenate([sel_bf16] * 3, axis=1), jnp.concatenate(_split3(x), axis=0),
                   preferred_element_type=F32)


def _half_sums(x):
    hi = x.astype(BF16)
    lo = (x - hi.astype(F32)).astype(BF16)
    i = lax.broadcasted_iota(jnp.int32, (2 * LANES, LANES), 0) & (LANES - 1)
    j = lax.broadcasted_iota(jnp.int32, (2 * LANES, LANES), 1)
    ones2 = jnp.where((i < LANES // 2) == (j < LANES // 2), 1.0, 0.0).astype(BF16)
    return jnp.dot(jnp.concatenate([hi, lo], axis=1), ones2, preferred_element_type=F32)


def _sigmoid(x):
    return 1.0 / (1.0 + jnp.exp(-x))


def _softplus(x):
    return jnp.maximum(x, 0.0) + jnp.log(1.0 + jnp.exp(-jnp.abs(x)))


def _valid_rows(row0, nrows, seq_p, nbatch):
    rows = row0 + lax.broadcasted_iota(jnp.int32, (nrows, 1), 0)
    pos = rows
    for b in range(1, nbatch):
        pos = jnp.where(rows >= b * seq_p, rows - b * seq_p, pos)
    return pos >= N_PAD


def _masked_rmsnorm(x, w, row0, seq_p, nbatch):
    ms = jnp.mean(x * x, axis=-1, keepdims=True)
    y = x * lax.rsqrt(ms + NORM_EPS) * w
    return jnp.where(_valid_rows(row0, x.shape[0], seq_p, nbatch), y, 0.0)


def _norm_mm_shift_kernel(x_ref, nw_ref, w_ref, mu_ref, o_ref, carry_scr, *, tm, seq_p, nbatch, shift0):
    m = pl.program_id(0)

    @pl.when(m == 0)
    def _():
        carry_scr[...] = jnp.zeros(carry_scr.shape, F32)

    valid = _valid_rows(m * tm, tm, seq_p, nbatch)
    h = _masked_rmsnorm(x_ref[...], nw_ref[...], m * tm, seq_p, nbatch)
    acc = jnp.dot(h.astype(BF16), w_ref[...], preferred_element_type=F32)
    o_ref[:, :shift0] = acc[:, :shift0]
    z = acc[:, shift0:]
    rows = lax.broadcasted_iota(jnp.int32, (tm, 1), 0)
    prev = jnp.where(rows == 0, carry_scr[SUBLANES - 1:SUBLANES], pltpu.roll(z, 1, axis=0))
    carry_scr[...] = z[tm - SUBLANES:]
    o_ref[:, shift0:] = jnp.where(valid, z + (prev - z) * mu_ref[...], 0.0)


def _norm_matmul_shift(res, norm_w, w_bf16, mu, *, shift0, seq_p, nbatch):
    T, D = res.shape
    N = w_bf16.shape[1]
    tm = _row_tile(T, ROW_TILE_RESIDENT)
    return pl.pallas_call(
        functools.partial(_norm_mm_shift_kernel, tm=tm, seq_p=seq_p, nbatch=nbatch, shift0=shift0),
        grid=(T // tm,),
        in_specs=[pl.BlockSpec((tm, D), lambda m: (m, 0)),
                  pl.BlockSpec((1, D), lambda m: (0, 0)),
                  pl.BlockSpec((D, N), lambda m: (0, 0), pipeline_mode=pl.Buffered(1)),
                  pl.BlockSpec((1, N - shift0), lambda m: (0, 0))],
        out_specs=pl.BlockSpec((tm, N), lambda m: (m, 0)),
        out_shape=jax.ShapeDtypeStruct((T, N), F32),
        scratch_shapes=[pltpu.VMEM((SUBLANES, N - shift0), F32)],
        compiler_params=_params("arbitrary"),
        name="norm_in_proj",
    )(res, norm_w.reshape(1, D), w_bf16, mu)


def _ssd_in_proj_kernel(x_ref, nw_ref, wz_ref, wx_ref, cw_ref, cb_ref, oz_ref, ox_ref, h_scr, carry_scr,
                        *, tm, seq_p, nbatch):
    m = pl.program_id(0)
    n = pl.program_id(1)

    @pl.when(n == 0)
    def _():
        h = _masked_rmsnorm(x_ref[...], nw_ref[...], m * tm, seq_p, nbatch)
        h_scr[...] = h.astype(BF16)

    @pl.when(m == 0)
    def _():
        carry_scr[n] = jnp.zeros(carry_scr.shape[1:], F32)

    valid = _valid_rows(m * tm, tm, seq_p, nbatch)
    h = h_scr[...]

    def conv_epilogue(sl, xc):
        full = jnp.concatenate([carry_scr[n, :, sl], xc], axis=0)
        carry_scr[n, :, sl] = xc[tm - SUBLANES:]
        acc = cb_ref[:, sl]
        for j in range(SSD_CONV):
            k = SSD_CONV - 1 - j
            term = xc if k == 0 else pltpu.roll(full, k, axis=0)[SUBLANES:]
            acc = acc + cw_ref[j:j + 1, sl] * term
        ox_ref[:, sl] = jnp.where(valid, acc * _sigmoid(acc), 0.0)

    nslab = ox_ref.shape[1] // CONV_SLAB
    zpieces = [(c, min(c + CONV_SLAB, oz_ref.shape[1])) for c in range(0, oz_ref.shape[1], CONV_SLAB)]
    zdone = 0
    pending = None
    for s in range(nslab):
        sl = slice(s * CONV_SLAB, (s + 1) * CONV_SLAB)
        xc = jnp.dot(h, wx_ref[:, sl], preferred_element_type=F32)
        ztarget = (len(zpieces) * (s + 1)) // nslab
        for c0, c1 in zpieces[zdone:ztarget]:
            oz_ref[:, c0:c1] = jnp.dot(h, wz_ref[:, c0:c1], preferred_element_type=F32)
        zdone = ztarget
        if pending is not None:
            conv_epilogue(*pending)
        pending = (sl, xc)
    conv_epilogue(*pending)


def _ssd_in_proj(res, norm_w, w_zdt, w_xbc, conv_w, conv_b, *, seq_p, nbatch):
    T, D = res.shape
    nz, nx = w_zdt.shape[1], w_xbc.shape[1]
    steps = SSD_PROJ_STEPS
    tz, tx = nz // steps, nx // steps
    assert tz * steps == nz and tx * steps == nx and tz % LANES == 0 and tx % CONV_SLAB == 0
    tm = _row_tile(T, ROW_TILE_RESIDENT)
    return pl.pallas_call(
        functools.partial(_ssd_in_proj_kernel, tm=tm, seq_p=seq_p, nbatch=nbatch),
        grid=(T // tm, steps),
        in_specs=[pl.BlockSpec((tm, D), lambda m, n: (m, 0)),
                  pl.BlockSpec((1, D), lambda m, n: (0, 0)),
                  pl.BlockSpec((D, tz), lambda m, n: (0, n)),
                  pl.BlockSpec((D, tx), lambda m, n: (0, n)),
                  pl.BlockSpec((SSD_CONV, tx), lambda m, n: (0, n)),
                  pl.BlockSpec((1, tx), lambda m, n: (0, n))],
        out_specs=[pl.BlockSpec((tm, tz), lambda m, n: (m, n)),
                   pl.BlockSpec((tm, tx), lambda m, n: (m, n))],
        out_shape=[jax.ShapeDtypeStruct((T, nz), F32), jax.ShapeDtypeStruct((T, nx), F32)],
        scratch_shapes=[pltpu.VMEM((tm, D), BF16), pltpu.VMEM((steps, SUBLANES, tx), F32)],
        compiler_params=_params("arbitrary", "arbitrary"),
        name="ssd_in_proj",
    )(res, norm_w.reshape(1, D), w_zdt, w_xbc, conv_w, conv_b.reshape(1, nx))


def _proj_res_kernel(*refs, n_lhs, tm, seq_p, nbatch):
    lhs = refs[:n_lhs]
    ws = refs[n_lhs:2 * n_lhs]
    res_ref, nw_ref, o_ref, h_ref = refs[2 * n_lhs:]
    acc = res_ref[...]
    for l, w in zip(lhs, ws):
        acc = acc + jnp.dot(l[...], w[...], preferred_element_type=F32)
    o_ref[...] = acc
    h_ref[...] = _masked_rmsnorm(acc, nw_ref[...], pl.program_id(0) * tm, seq_p, nbatch).astype(h_ref.dtype)


def _proj_residual(lhs_list, w_bf16, res, next_norm_w, *, seq_p, nbatch):
    T, D = res.shape
    n_lhs = len(lhs_list)
    kw = lhs_list[0].shape[1]
    assert all(l.shape == (T, kw) for l in lhs_list) and w_bf16.shape == (n_lhs * kw, D)
    tm = _row_tile(T, ROW_TILE_RESIDENT)
    in_specs = [pl.BlockSpec((tm, kw), lambda m: (m, 0)) for _ in range(n_lhs)]
    in_specs += [pl.BlockSpec((kw, D), functools.partial(lambda i, m: (i, 0), i), pipeline_mode=pl.Buffered(1))
                 for i in range(n_lhs)]
    in_specs += [pl.BlockSpec((tm, D), lambda m: (m, 0)), pl.BlockSpec((1, D), lambda m: (0, 0))]
    return pl.pallas_call(
        functools.partial(_proj_res_kernel, n_lhs=n_lhs, tm=tm, seq_p=seq_p, nbatch=nbatch),
        grid=(T // tm,),
        in_specs=in_specs,
        out_specs=[pl.BlockSpec((tm, D), lambda m: (m, 0)), pl.BlockSpec((tm, D), lambda m: (m, 0))],
        out_shape=[jax.ShapeDtypeStruct((T, D), F32), jax.ShapeDtypeStruct((T, D), BF16)],
        compiler_params=_params("arbitrary"),
        name="out_proj_residual",
    )(*lhs_list, *([w_bf16] * n_lhs), res, next_norm_w.reshape(1, D))


def _ffn_kernel(res_ref, h_ref, wg_ref, wv_ref, cw_ref, cb_ref, wd_ref, o_ref, carry_scr, *, tm):
    m = pl.program_id(0)
    f = pl.program_id(1)

    @pl.when(f == 0)
    def _():
        o_ref[...] = res_ref[...]

    @pl.when(m == 0)
    def _():
        carry_scr[f] = jnp.zeros(carry_scr.shape[1:], F32)

    h = h_ref[...]
    gate = jnp.dot(h, wg_ref[...], preferred_element_type=F32)
    val = jnp.dot(h, wv_ref[...], preferred_element_type=F32)
    prev = carry_scr[f]
    rows = lax.broadcasted_iota(jnp.int32, (tm, 1), 0)
    last, last2 = prev[SUBLANES - 1:SUBLANES], prev[SUBLANES - 2:SUBLANES - 1]
    g1 = jnp.where(rows == 0, last, pltpu.roll(gate, 1, axis=0))
    g2 = jnp.where(rows == 0, last2, jnp.where(rows == 1, last, pltpu.roll(gate, 2, axis=0)))
    carry_scr[f] = gate[tm - SUBLANES:]
    cw = cw_ref[...]
    pre = cb_ref[...] + cw[0:1] * g2 + cw[1:2] * g1 + cw[2:3] * gate
    act = pre * _sigmoid(pre) * val
    o_ref[...] += jnp.dot(act.astype(BF16), wd_ref[...], preferred_element_type=F32)


def _ffn(res, h_bf16, w_up_bf16, conv_w, conv_b, w_down_bf16):
    T, D = res.shape
    tm = _row_tile(T, ROW_TILE_STREAMED)
    tf = FFN_CHUNK
    nf = FFN_DIM // tf
    return pl.pallas_call(
        functools.partial(_ffn_kernel, tm=tm),
        grid=(T // tm, nf),
        in_specs=[pl.BlockSpec((tm, D), lambda m, f: (m, 0)),
                  pl.BlockSpec((tm, D), lambda m, f: (m, 0)),
                  pl.BlockSpec((D, tf), lambda m, f: (0, f)),
                  pl.BlockSpec((D, tf), lambda m, f: (0, nf + f)),
                  pl.BlockSpec((FFN_CONV, tf), lambda m, f: (0, f)),
                  pl.BlockSpec((1, tf), lambda m, f: (0, f)),
                  pl.BlockSpec((tf, D), lambda m, f: (f, 0))],
        out_specs=pl.BlockSpec((tm, D), lambda m, f: (m, 0)),
        out_shape=jax.ShapeDtypeStruct((T, D), F32),
        scratch_shapes=[pltpu.VMEM((nf, SUBLANES, tf), F32)],
        compiler_params=_params("arbitrary", "arbitrary"),
        name="ffn",
    )(res, h_bf16, w_up_bf16, w_up_bf16, conv_w, conv_b.reshape(1, FFN_DIM), w_down_bf16)


def _attn_kernel(sinks_ref, q_ref, kvc_ref, kvp_ref, kvm_ref, qw_ref, kw_ref, o_ref, *, nsub):
    n = pl.program_id(1)
    scale = HEAD_DIM ** -0.5
    lane = lax.broadcasted_iota(jnp.int32, (BLOCK, LANES), 1)
    lo_half = lane < HEAD_DIM
    rq = nsub * BLOCK

    kv_blocks = ([kvp_ref[...]] + [kvc_ref[i * BLOCK:(i + 1) * BLOCK] for i in range(nsub)] + [kvm_ref[...]])
    q = q_ref[...]
    nq, nk = Q_COLS // LANES, KV_COLS // LANES
    q_tiles = [q[:, c * LANES:(c + 1) * LANES] for c in range(nq)]
    k_tiles = [kv[:, c * LANES:(c + 1) * LANES] for kv in kv_blocks for c in range(nk)]
    sq = jnp.concatenate([t * t for t in q_tiles + k_tiles], axis=0)
    inv = lax.rsqrt(_half_sums(sq) * (1.0 / HEAD_DIM) + QK_EPS)
    qw = qw_ref[...] * scale
    kw = kw_ref[...]
    qn = [q_tiles[c] * inv[c * rq:(c + 1) * rq] * qw for c in range(nq)]
    k0 = nq * rq
    kn = [[k_tiles[j * nk + c] * inv[k0 + (j * nk + c) * BLOCK:k0 + (j * nk + c + 1) * BLOCK] * kw
           for c in range(nk)] for j in range(len(kv_blocks))]

    qi = lax.broadcasted_iota(jnp.int32, (BLOCK, 3 * BLOCK), 0)
    kj = lax.broadcasted_iota(jnp.int32, (BLOCK, 3 * BLOCK), 1)
    j_prev, j_cur, j_meta = kj, kj - BLOCK, kj - 2 * BLOCK
    grow = lax.broadcasted_iota(jnp.int32, (ATTN_GROUP * BLOCK, 1), 0)

    def dup_half(x, upper):
        swapped = pltpu.roll(x, HEAD_DIM, axis=1)
        lane_r = lax.broadcasted_iota(jnp.int32, x.shape, 1) < HEAD_DIM
        return jnp.where(lane_r, swapped, x) if upper else jnp.where(lane_r, x, swapped)

    for i in range(nsub):
        blk = n * nsub + i
        rs = slice(i * BLOCK, (i + 1) * BLOCK)
        blocks = (i, i + 1, nsub + 1)
        first_prev = jnp.where(blk == 0, BLOCK, jnp.where(blk == 1, N_PAD, 0))
        first_cur = jnp.where(blk == 0, N_PAD, 0)
        vis_prev = (kj < BLOCK) & (j_prev > qi) & (j_prev >= first_prev)
        vis_cur = (kj >= BLOCK) & (j_cur <= qi) & (j_cur >= first_cur)
        vis_meta = (j_meta >= N_PAD) & (blk * BLOCK + qi - j_meta >= WINDOW)
        bias = jnp.where(vis_prev | vis_cur | vis_meta, 0.0, -jnp.inf)
        bias4 = jnp.concatenate([bias] * ATTN_GROUP, axis=0)
        outs = []
        for c in range(nk):
            k_col = jnp.concatenate([kn[j][c] for j in blocks], axis=0)
            v_col = jnp.concatenate([kv_blocks[j][:, KV_COLS + c * LANES:KV_COLS + (c + 1) * LANES] for j in blocks],
                                    axis=0)
            for half in range(2):
                h = 2 * c + half
                k_dup = dup_half(k_col, half == 1).astype(BF16)
                v_dup = dup_half(v_col, half == 1).astype(BF16)
                lhs = jnp.concatenate(
                    [jnp.where(lo_half if e == 0 else jnp.logical_not(lo_half), qn[2 * h + pr][rs], 0.0)
                     for pr in range(2) for e in range(2)], axis=0)
                s = _dot_nt(lhs, k_dup) + bias4
                sink = sinks_ref[ATTN_GROUP * h]
                for g in range(1, ATTN_GROUP):
                    sink = jnp.where(grow >= g * BLOCK, sinks_ref[ATTN_GROUP * h + g], sink)
                mx = jnp.maximum(jnp.max(s, axis=-1, keepdims=True), sink)
                p = jnp.exp(s - mx)
                den = jnp.sum(p, axis=-1, keepdims=True) + jnp.exp(sink - mx)
                o = jnp.dot(p.astype(BF16), v_dup, preferred_element_type=F32) / den
                for pr in range(2):
                    r0 = 2 * pr * BLOCK
                    outs.append(jnp.where(lo_half, o[r0:r0 + BLOCK], o[r0 + BLOCK:r0 + 2 * BLOCK]))
        o_ref[rs, :] = jnp.concatenate(outs, axis=-1).astype(o_ref.dtype)


def _attention(zin, sinks, q_norm_w, k_norm_w, *, nbatch, nblk):
    T = zin.shape[0]
    nsub = ATTN_SUBBLOCKS
    assert nblk % nsub == 0
    nstep = nblk // nsub
    kv_blk = Q_COLS // (2 * KV_COLS)
    return pl.pallas_call(
        functools.partial(_attn_kernel, nsub=nsub),
        grid=(nbatch, nstep),
        in_specs=[pl.BlockSpec(memory_space=pltpu.SMEM),
                  pl.BlockSpec((nsub * BLOCK, Q_COLS), lambda b, n: (b * nstep + n, 0)),
                  pl.BlockSpec((nsub * BLOCK, 2 * KV_COLS), lambda b, n: (b * nstep + n, kv_blk)),
                  pl.BlockSpec((BLOCK, 2 * KV_COLS), lambda b, n: (b * nblk + jnp.maximum(n * nsub - 1, 0), kv_blk)),
                  pl.BlockSpec((BLOCK, 2 * KV_COLS), lambda b, n: (b * nblk, kv_blk)),
                  pl.BlockSpec((1, LANES), lambda b, n: (0, 0)),
                  pl.BlockSpec((1, LANES), lambda b, n: (0, 0))],
        out_specs=pl.BlockSpec((nsub * BLOCK, Q_COLS), lambda b, n: (b * nstep + n, 0)),
        out_shape=jax.ShapeDtypeStruct((T, Q_COLS), BF16),
        compiler_params=_params("arbitrary", "arbitrary"),
        name="swa_attention",
    )(sinks, zin, zin, zin, zin, jnp.tile(q_norm_w, LANES // HEAD_DIM).reshape(1, LANES),
      jnp.tile(k_norm_w, LANES // HEAD_DIM).reshape(1, LANES))


def _pair_blockdiag(x):
    xb = x.astype(BF16)
    lane = lax.broadcasted_iota(jnp.int32, xb.shape, 2)
    zero = jnp.zeros_like(xb)
    return jnp.concatenate([jnp.where(lane < RWKV_HEAD, xb, zero), jnp.where(lane >= RWKV_HEAD, xb, zero)], axis=1)


def _bmm(a, b_bf16):
    return lax.dot_general(a.astype(BF16), b_bf16, (((2,), (1,)), ((0,), (0,))), preferred_element_type=F32)


def _bmm_nt(a, b_bf16):
    return lax.dot_general(a.astype(BF16), b_bf16, (((2,), (2,)), ((0,), (0,))), preferred_element_type=F32)


def _pair_pick(full):
    lane = lax.broadcasted_iota(jnp.int32, (full.shape[0], RWKV_HEAD, LANES), 2)
    return jnp.where(lane < RWKV_HEAD, full[:, :RWKV_HEAD], full[:, RWKV_HEAD:])


def _to_pairs(x, nchunk):
    L = RWKV_CHUNK
    return jnp.stack([x[c * L:(c + 1) * L, p * LANES:(p + 1) * LANES]
                      for c in range(nchunk) for p in range(x.shape[1] // LANES)], axis=0)


def _pair_segsum(*xs):
    npair = xs[0].shape[1] // LANES
    rows = xs[0].shape[0]
    stacked = jnp.concatenate([x[:, p * LANES:(p + 1) * LANES] for x in xs for p in range(npair)], axis=0)
    s = _half_sums(stacked)
    outs = [jnp.concatenate([s[(i * npair + p) * rows:(i * npair + p + 1) * rows] for p in range(npair)], axis=-1)
            for i in range(len(xs))]
    return outs[0] if len(xs) == 1 else outs


def _rwkv_kernel(rkv_a_ref, rkv_b_ref, lora_ref, wup_wa_ref, wup_g_ref, vec_ref, o_ref, s_scr, *, tb):
    n = pl.program_id(1)
    L = RWKV_CHUNK
    npair = RWKV_DIM // LANES

    @pl.when(n == 0)
    def _():
        s_scr[...] = jnp.zeros(s_scr.shape, F32)

    rows = lax.broadcasted_iota(jnp.int32, (tb, 1), 0)
    valid = (n > 0) | (rows >= N_PAD)
    zr = jnp.concatenate([rkv_a_ref[...], rkv_b_ref[...]], axis=1)
    zl = lora_ref[...]
    r, k, v = zr[:, :RWKV_DIM], zr[:, RWKV_DIM:2 * RWKV_DIM], zr[:, 2 * RWKV_DIM:]

    z_wa = zl[:, :LANES]
    col = lax.broadcasted_iota(jnp.int32, z_wa.shape, 1)
    act_wa = jnp.where(col < DECAY_LORA, jnp.tanh(z_wa), z_wa)
    up_wa = jnp.dot(act_wa.astype(BF16), wup_wa_ref[...], preferred_element_type=F32)
    g = jnp.dot(_sigmoid(zl[:, LANES:]).astype(BF16), wup_g_ref[...], preferred_element_type=F32)
    vec = vec_ref[...]
    w0, a0, k_k, k_a, r_k, ln_w, ln_b = (vec[i:i + 1] for i in range(7))
    w_log = jnp.where(valid, -DECAY_SCALE * _sigmoid(w0 + up_wa[:, :RWKV_DIM]), 0.0)
    a = _sigmoid(a0 + up_wa[:, RWKV_DIM:])

    kk = k * k_k
    k2 = k * (1.0 + (a - 1.0) * k_a)
    kk_sq, bonus = _pair_segsum(kk * kk, r * k2 * r_k)
    kk = kk * lax.rsqrt(jnp.maximum(kk_sq, 1e-24))
    kka = kk * a

    nchunk = tb // L
    nb = nchunk * npair
    t_i = lax.broadcasted_iota(jnp.int32, (nb, L, LANES), 1)
    s_i = lax.broadcasted_iota(jnp.int32, (nb, L, LANES), 2) & (RWKV_HEAD - 1)
    strict = s_i < t_i
    incl = s_i <= t_i
    same16 = (t_i >> 4) == (s_i >> 4)
    same32 = (t_i >> 5) == (s_i >> 5)
    eye = jnp.where(s_i == t_i, 1.0, 0.0)
    ti2 = lax.broadcasted_iota(jnp.int32, (tb, tb), 0)
    si2 = lax.broadcasted_iota(jnp.int32, (tb, tb), 1)
    lshift = L.bit_length() - 1
    tri_incl = jnp.where((si2 <= ti2) & ((si2 >> lshift) == (ti2 >> lshift)), 1.0, 0.0).astype(BF16)
    cum = _dot_exact_rhs(tri_incl, w_log)
    cum_end = jnp.concatenate(
        [jnp.broadcast_to(cum[(c + 1) * L - 1:(c + 1) * L], (L, RWKV_DIM)) for c in range(nchunk)], axis=0)
    e_w = jnp.exp(cum)
    e_iw = jnp.exp(-cum)
    e_prev = jnp.exp(cum - w_log)
    e_end = jnp.exp(cum_end - cum)
    w_end = jnp.exp(cum_end)
    ah = _to_pairs(-kk * e_prev, nchunk)
    bh = _to_pairs(kka * e_iw, nchunk)
    kh = _to_pairs(k2 * e_iw, nchunk)
    rh = _to_pairs(r * e_w, nchunk)
    bt = _to_pairs(kka * e_end, nchunk)
    kt = _to_pairs(k2 * e_end, nchunk)
    vp = _to_pairs(v, nchunk)
    vbd = _pair_blockdiag(vp)

    ar = jnp.concatenate([ah, rh], axis=1)
    m = _bmm_nt(ar, jnp.concatenate([_pair_blockdiag(bh), _pair_blockdiag(kh)], axis=1))
    a_ab = jnp.where(strict, m[:, :L, :LANES], 0.0)
    a_rb = jnp.where(incl, m[:, L:, :LANES], 0.0)
    a_ak = jnp.where(strict, m[:, :L, LANES:], 0.0)
    a_rk = jnp.where(incl, m[:, L:, LANES:], 0.0)
    kv = _bmm(jnp.concatenate([a_ak, a_rk], axis=1), vbd)
    akv, y_rk = kv[:, :L], kv[:, L:]
    a1 = jnp.where(same16, a_ab, 0.0)
    t1 = eye + a1
    a2 = _bmm(a1, _pair_blockdiag(a1))
    x = _bmm(jnp.concatenate([a2, t1], axis=1), _pair_blockdiag(a2))
    a4, t2 = x[:, :L], t1 + x[:, L:]
    x = _bmm(jnp.concatenate([a4, t2], axis=1), _pair_blockdiag(a4))
    a8, t3 = x[:, :L], t2 + x[:, L:]
    t16 = t3 + _bmm(t3, _pair_blockdiag(a8))
    off32 = jnp.where(same32 & jnp.logical_not(same16), a_ab, 0.0)
    t32 = t16 + _bmm(t16, _pair_blockdiag(_bmm(off32, _pair_blockdiag(t16))))
    off64 = jnp.where(same32, 0.0, a_ab)
    tinv = t32 + _bmm(t32, _pair_blockdiag(_bmm(off64, _pair_blockdiag(t32))))
    tg = _bmm(tinv, jnp.concatenate([_pair_blockdiag(ah), _pair_blockdiag(akv)], axis=2))
    ta, gm = tg[:, :, :LANES], tg[:, :, LANES:]
    rg = _bmm(a_rb, jnp.concatenate([_pair_blockdiag(ta), _pair_blockdiag(gm)], axis=2))
    rt = rh + rg[:, :, :LANES]
    y_intra = rg[:, :, LANES:] + y_rk
    tgv_t = jnp.swapaxes(jnp.concatenate([ta, gm, vp], axis=2), 1, 2)
    pp = _bmm(tgv_t[:, :2 * LANES], bt.astype(BF16))
    phi = _pair_pick(pp[:, :LANES])
    psi = _pair_pick(pp[:, LANES:]) + _pair_pick(_bmm(tgv_t[:, 2 * LANES:], kt.astype(BF16)))
    w_end_p = _to_pairs(w_end, nchunk)[:, :1]

    y_chunks = []
    s = s_scr[...]
    for c in range(nchunk):
        cs = slice(c * npair, (c + 1) * npair)
        y_c = _bmm_nt(rt[cs], _pair_blockdiag(s)) + y_intra[cs]
        s = s * w_end_p[cs] + _bmm(s, _pair_blockdiag(phi[cs])) + psi[cs]
        y_chunks.append(jnp.concatenate([y_c[p] for p in range(npair)], axis=-1))
    s_scr[...] = s
    y = jnp.concatenate(y_chunks, axis=0)

    inv_n = 1.0 / RWKV_HEAD
    mu = _pair_segsum(y) * inv_n
    d = y - mu
    var = _pair_segsum(d * d) * inv_n
    yn = d * lax.rsqrt(var + RWKV_GN_EPS) * ln_w + ln_b
    o_ref[...] = ((yn + bonus * v) * g).astype(o_ref.dtype)


def _rwkv(zin, wup_wa, wup_g, vecs, *, nbatch, nblk):
    T = zin.shape[0]
    tb = RWKV_STEP_BLOCKS * BLOCK
    assert nblk % RWKV_STEP_BLOCKS == 0
    nstep = nblk // RWKV_STEP_BLOCKS
    half = 3 * RWKV_DIM // 2
    assert ATTN_COLS == half and (ATTN_COLS + 3 * RWKV_DIM) % LORA_PAD == 0
    lora_blk = (ATTN_COLS + 3 * RWKV_DIM) // LORA_PAD
    row = lambda b, n: b * nstep + n
    return pl.pallas_call(
        functools.partial(_rwkv_kernel, tb=tb),
        grid=(nbatch, nstep),
        in_specs=[pl.BlockSpec((tb, half), lambda b, n: (row(b, n), 1)),
                  pl.BlockSpec((tb, half), lambda b, n: (row(b, n), 2)),
                  pl.BlockSpec((tb, LORA_PAD), lambda b, n: (row(b, n), lora_blk)),
                  pl.BlockSpec((LANES, 2 * RWKV_DIM), lambda b, n: (0, 0)),
                  pl.BlockSpec((LORA_PAD - LANES, RWKV_DIM), lambda b, n: (0, 0)),
                  pl.BlockSpec((SUBLANES, RWKV_DIM), lambda b, n: (0, 0))],
        out_specs=pl.BlockSpec((tb, RWKV_DIM), lambda b, n: (row(b, n), 0)),
        out_shape=jax.ShapeDtypeStruct((T, RWKV_DIM), BF16),
        scratch_shapes=[pltpu.VMEM((RWKV_DIM // LANES, RWKV_HEAD, LANES), F32)],
        compiler_params=_params("arbitrary", "arbitrary"),
        name="rwkv7",
    )(zin, zin, zin, wup_wa, wup_g, vecs)


def _ssd_kernel(z_ref, dt_ref, x_ref, b_ref, c_ref, dtb_ref, alog_ref, dskip_ref, nw_ref, exp_ref, o_ref, st_scr,
                *, nchunk):
    n = pl.program_id(1)

    @pl.when(n == 0)
    def _():
        st_scr[...] = jnp.zeros(st_scr.shape, F32)

    for c in range(nchunk):
        _ssd_chunk(n, c, z_ref, dt_ref, x_ref, b_ref, c_ref, dtb_ref, alog_ref, dskip_ref, nw_ref, exp_ref, o_ref, st_scr)


def _ssd_chunk(n, c, z_ref, dt_ref, x_ref, b_ref, c_ref, dtb_ref, alog_ref, dskip_ref, nw_ref, exp_ref, o_ref, st_scr):
    rs = slice(c * BLOCK, (c + 1) * BLOCK)
    x = x_ref[rs, :]
    bm = b_ref[rs, :]
    cm = c_ref[rs, :]

    dt = _softplus(dt_ref[rs, :] + dtb_ref[...])
    if c == 0:
        rows = lax.broadcasted_iota(jnp.int32, (BLOCK, 1), 0)
        dt = jnp.where((n > 0) | (rows >= N_PAD), dt, 0.0)
    adt = dt * (-jnp.exp(alog_ref[...]))
    ti = lax.broadcasted_iota(jnp.int32, (BLOCK, BLOCK), 0)
    si = lax.broadcasted_iota(jnp.int32, (BLOCK, BLOCK), 1)
    causal = si <= ti
    cum = _dot_exact_rhs(jnp.where(causal, 1.0, 0.0).astype(BF16), adt)
    cum_t = cum.T
    ecum = jnp.exp(cum)
    dt_end = dt * jnp.exp(cum[BLOCK - 1:BLOCK] - cum)
    e_hi = ecum.astype(BF16)
    e_lo = (ecum - e_hi.astype(F32)).astype(BF16)
    ecum_x = jnp.dot(jnp.concatenate([e_hi, e_lo], axis=1), exp_ref[...], preferred_element_type=F32)
    ex = jnp.dot(jnp.concatenate([dt, dt_end], axis=0).astype(BF16), exp_ref[:SSD_DT_PAD],
                 preferred_element_type=F32)
    xdt = x * ex[:BLOCK]
    xend = x * ex[BLOCK:]
    lane = lax.broadcasted_iota(jnp.int32, (BLOCK, LANES), 1)
    lo_half = lane < SSD_HEAD_DIM

    y_groups = []
    for g in range(SSD_GROUPS):
        gs = slice(g * SSD_GROUP_COLS, (g + 1) * SSD_GROUP_COLS)
        bg = bm[:, g * SSD_STATE:(g + 1) * SSD_STATE]
        cg = cm[:, g * SSD_STATE:(g + 1) * SSD_STATE]
        cb = _dot_nt(cg, bg)
        st = st_scr[g]
        y_inter = _dot(cg, st) * ecum_x[:, gs]
        y_pairs = []
        for q in range(SSD_HPG // 2):
            mats = []
            for j in (g * SSD_HPG + 2 * q, g * SSD_HPG + 2 * q + 1):
                seg = cum[:, j:j + 1] - cum_t[j:j + 1, :]
                mats.append(cb * jnp.exp(jnp.where(causal, seg, -jnp.inf)))
            ll = jnp.concatenate(mats, axis=1).astype(BF16)
            c0 = g * SSD_GROUP_COLS + q * LANES
            xp = xdt[:, c0:c0 + LANES]
            bdx = jnp.concatenate([jnp.where(lo_half, xp, 0.0), jnp.where(lo_half, 0.0, xp)], axis=0)
            y_pairs.append(jnp.dot(ll, bdx.astype(BF16), preferred_element_type=F32))
        y_intra = jnp.concatenate(y_pairs, axis=-1)
        st_scr[g] = st * ecum_x[BLOCK - 1:BLOCK, gs] + _dot_tn(bg, xend[:, gs])
        y_groups.append(y_intra + y_inter + dskip_ref[:, gs] * x[:, gs])
    z = z_ref[rs, :]
    nw = nw_ref[...]
    outs = []
    for g in range(SSD_GROUPS):
        gs = slice(g * SSD_GROUP_COLS, (g + 1) * SSD_GROUP_COLS)
        zg = z[:, gs]
        yg = y_groups[g] * (zg * _sigmoid(zg))
        yg = yg * lax.rsqrt(jnp.mean(yg * yg, axis=-1, keepdims=True) + SSD_NORM_EPS)
        outs.append((yg * nw[:, gs]).astype(o_ref.dtype))
    o_ref[rs, :] = jnp.concatenate(outs, axis=-1)


def _ssd(zdt, xbc, dt_bias, a_log, d_skip, norm_w, *, nbatch, nblk):
    T = zdt.shape[0]
    nchunk = SSD_STEP_CHUNKS
    assert nblk % nchunk == 0
    nstep = nblk // nchunk
    tb = nchunk * BLOCK
    row = lambda b, n: b * nstep + n
    pad_h = lambda t: jnp.pad(t.astype(F32), (0, SSD_DT_PAD - SSD_HEADS)).reshape(1, SSD_DT_PAD)
    head = lax.broadcasted_iota(jnp.int32, (SSD_DT_PAD, SSD_INNER), 0)
    colh = lax.broadcasted_iota(jnp.int32, (SSD_DT_PAD, SSD_INNER), 1) // SSD_HEAD_DIM
    expand = jnp.tile((head == colh).astype(BF16), (2, 1))
    full = lambda shape: pl.BlockSpec(shape, lambda b, n: (0, 0))
    b_blk = SSD_INNER // SSD_BC_COLS
    dt_blk = SSD_INNER // SSD_DT_PAD
    return pl.pallas_call(
        functools.partial(_ssd_kernel, nchunk=nchunk),
        grid=(nbatch, nstep),
        in_specs=[pl.BlockSpec((tb, SSD_INNER), lambda b, n: (row(b, n), 0)),
                  pl.BlockSpec((tb, SSD_DT_PAD), lambda b, n: (row(b, n), dt_blk)),
                  pl.BlockSpec((tb, SSD_INNER), lambda b, n: (row(b, n), 0)),
                  pl.BlockSpec((tb, SSD_BC_COLS), lambda b, n: (row(b, n), b_blk)),
                  pl.BlockSpec((tb, SSD_BC_COLS), lambda b, n: (row(b, n), b_blk + 1)),
                  full((1, SSD_DT_PAD)), full((1, SSD_DT_PAD)), full((1, SSD_INNER)), full((1, SSD_INNER)),
                  full((2 * SSD_DT_PAD, SSD_INNER))],
        out_specs=pl.BlockSpec((tb, SSD_INNER), lambda b, n: (row(b, n), 0)),
        out_shape=jax.ShapeDtypeStruct((T, SSD_INNER), BF16),
        scratch_shapes=[pltpu.VMEM((SSD_GROUPS, SSD_STATE, SSD_GROUP_COLS), F32)],
        compiler_params=_params("arbitrary", "arbitrary"),
        name="ssd",
    )(zdt, zdt, xbc, xbc, xbc, pad_h(dt_bias), pad_h(a_log),
      jnp.repeat(d_skip.astype(F32), SSD_HEAD_DIM).reshape(1, SSD_INNER), norm_w.reshape(1, SSD_INNER), expand)


def _pack_ar_w_in(w):
    return jnp.pad(w, ((0, 0), (0, LORA_PAD - LORA_COLS))).astype(BF16)


def _pack_lora_up(w_up, a_up, g_up):
    assert DECAY_LORA + AAA_LORA == LANES
    zero = jnp.zeros((DECAY_LORA, RWKV_DIM), F32)
    wup_wa = jnp.concatenate([jnp.concatenate([w_up, zero], axis=1), jnp.concatenate([zero, a_up], axis=1)], axis=0)
    wup_g = jnp.pad(g_up, ((0, LORA_PAD - LANES - GATE_LORA), (0, 0)))
    return wup_wa.astype(BF16), wup_g.astype(BF16)


def kernel(x, meta_tokens, mix_norm_w, ffn_norm_w, ar_w_in, ar_shift_mu, attn_q_norm_w, attn_k_norm_w, attn_sinks, rwkv_w0, rwkv_w_up, rwkv_a0, rwkv_a_up, rwkv_g_up, rwkv_k_k, rwkv_k_a, rwkv_r_k, rwkv_ln_w, rwkv_ln_b, ar_w_out, ssd_w_in, ssd_conv_w, ssd_conv_b, ssd_dt_bias, ssd_a_log, ssd_d, ssd_norm_w, ssd_w_out, ffn_w_up, ffn_conv_w, ffn_conv_b, ffn_w_down):
    nbatch, seq, d = x.shape
    assert d == D_MODEL and seq % BLOCK == 0
    seq_p = N_PAD + N_META + seq
    nblk = seq_p // BLOCK
    depth = mix_norm_w.shape[0]
    res = jnp.concatenate([
        jnp.zeros((nbatch, N_PAD, d), x.dtype),
        jnp.broadcast_to(meta_tokens.astype(x.dtype)[None], (nbatch, N_META, d)),
        x], axis=1).reshape(nbatch * seq_p, d)
    kw = dict(seq_p=seq_p, nbatch=nbatch)
    for layer in range(depth):
        i = layer // 2
        if layer % 2 == 0:
            mu = jnp.pad(ar_shift_mu[i], (0, LORA_PAD - LORA_COLS)).reshape(1, -1)
            zin = _norm_matmul_shift(res, mix_norm_w[layer], _pack_ar_w_in(ar_w_in[i]), mu, shift0=ATTN_COLS, **kw)
            attn = _attention(zin, attn_sinks[i].astype(F32), attn_q_norm_w[i], attn_k_norm_w[i],
                              nbatch=nbatch, nblk=nblk)
            vecs = jnp.stack([rwkv_w0[i], rwkv_a0[i], rwkv_k_k[i], rwkv_k_a[i], rwkv_r_k[i].reshape(-1),
                              rwkv_ln_w[i], rwkv_ln_b[i], jnp.zeros((RWKV_DIM,), F32)]).astype(F32)
            tm = _rwkv(zin, *_pack_lora_up(rwkv_w_up[i], rwkv_a_up[i], rwkv_g_up[i]), vecs,
                       nbatch=nbatch, nblk=nblk)
            res, h_ffn = _proj_residual([attn, tm], ar_w_out[i].astype(BF16), res, ffn_norm_w[layer], **kw)
        else:
            w_in = ssd_w_in[i]
            xbc0, dt0 = SSD_INNER, 2 * SSD_INNER + 2 * SSD_BC_COLS
            w_zdt = jnp.concatenate([w_in[:, :xbc0], w_in[:, dt0:],
                                     jnp.zeros((d, SSD_DT_PAD - SSD_HEADS), w_in.dtype)], axis=1).astype(BF16)
            zdt, xbc = _ssd_in_proj(res, mix_norm_w[layer], w_zdt, w_in[:, xbc0:dt0].astype(BF16), ssd_conv_w[i],
                                    ssd_conv_b[i], **kw)
            y = _ssd(zdt, xbc, ssd_dt_bias[i], ssd_a_log[i], ssd_d[i], ssd_norm_w[i], nbatch=nbatch, nblk=nblk)
            res, h_ffn = _proj_residual([y], ssd_w_out[i].astype(BF16), res, ffn_norm_w[layer], **kw)
        res = _ffn(res, h_ffn, ffn_w_up[layer].astype(BF16), ffn_conv_w[layer], ffn_conv_b[layer],
                   ffn_w_down[layer].astype(BF16))
    return res.reshape(nbatch, seq_p, d)[:, N_PAD + N_META:]
```
